```python
import math
import jax
import jax.numpy as jnp
from jax import lax
import numpy as np

D_MODEL = 1024
BATCH = 4
SEQ = 4096
DEPTH = 4
DEC_BATCH = 128
DEC_SEQ = 4
PAST_LEN = 2048
PAGE_SIZE = 128

ATTN_GROUPS = ((128, 1), (512, 4), (2048, 16))
N_GROUPS = len(ATTN_GROUPS)
HEADS_PER_GROUP = 8
HEAD_DIM = 64
ATTN_BLOCK = 128
ATTN_QKV = N_GROUPS * HEADS_PER_GROUP * HEAD_DIM
ATTN_OUT = HEADS_PER_GROUP * HEAD_DIM
MLSTM_HEADS = 4
MLSTM_DV = D_MODEL // MLSTM_HEADS
MLSTM_DK = MLSTM_DV // 2
MLSTM_CHUNK = 64
MLSTM_QK = MLSTM_HEADS * MLSTM_DK
MLSTM_V = MLSTM_HEADS * MLSTM_DV
N_EXPERTS = 32
TOP_K = 4
D_EXPERT = D_MODEL
SWIGLU_LIMIT = 7.0
SWIGLU_ALPHA = 1.702
MOE_BLOCK = 128
RMS_EPS = 1e-6
IN_SIZES = (ATTN_QKV, ATTN_QKV, ATTN_QKV, MLSTM_QK, MLSTM_QK, MLSTM_V, MLSTM_V, 2 * MLSTM_HEADS, D_MODEL, D_MODEL)
IN_SPLITS = tuple(int(s) for s in np.cumsum(IN_SIZES)[:-1])
D_IN = sum(IN_SIZES)

kernel_name = 'dilated_attn_mlstm_moe_hybrid_step'


def rms_norm(x, g):
    xf = x.astype(jnp.float32)
    y = xf * lax.rsqrt(jnp.mean(xf * xf, axis=-1, keepdims=True) + RMS_EPS)
    return (y * g.astype(jnp.float32)).astype(x.dtype)


def modulated_norm(x, g, shift, scale):
    return rms_norm(x, g) * (1 + scale) + shift


def alibi_slopes():
    return jnp.asarray(2.0 ** (-8.0 * np.arange(1, HEADS_PER_GROUP + 1) / HEADS_PER_GROUP), dtype=jnp.float32)


def dilated_attention_prompt(q, k, v, window, dil):
    B, S, H, Dh = q.shape
    L = S // dil
    nb = -(-L // ATTN_BLOCK)
    Lp = nb * ATTN_BLOCK
    span = window // dil

    def to_blocks(t):
        t = jnp.swapaxes(t.reshape(B, L, dil, H, Dh), 1, 2)
        t = jnp.pad(t, ((0, 0), (0, 0), (0, Lp - L), (0, 0), (0, 0)))
        return t.reshape(B, dil, nb, ATTN_BLOCK, H, Dh)

    def with_prev(t):
        prev = jnp.pad(t[:, :, :-1], ((0, 0), (0, 0), (1, 0), (0, 0), (0, 0), (0, 0)))
        return jnp.concatenate([prev, t], axis=3)

    qb = to_blocks(q)
    kb = with_prev(to_blocks(k))
    vb = with_prev(to_blocks(v))
    s = jnp.einsum('brnqhd,brnkhd->brnqhk', qb, kb, preferred_element_type=jnp.float32) * (Dh ** -0.5)
    qi = np.arange(ATTN_BLOCK)[:, None]
    ki = np.arange(2 * ATTN_BLOCK)[None, :]
    dist = qi + ATTN_BLOCK - ki
    key_pos = np.arange(nb)[:, None, None] * ATTN_BLOCK - ATTN_BLOCK + ki[None]
    mask = (dist >= 0) & (dist <= span) & (key_pos >= 0)
    bias = -alibi_slopes()[None, :, None] * jnp.asarray(dil * dist, jnp.float32)[:, None, :]
    s = jnp.where(mask[:, :, None, :], s + bias, -jnp.inf)
    lse = jax.nn.logsumexp(s, axis=-1)
    p = jnp.exp(s - lse[..., None]).astype(v.dtype)
    o = jnp.einsum('brnqhk,brnkhd->brnqhd', p, vb)

    def from_blocks(t):
        t = t.reshape(B, dil, Lp, *t.shape[4:])[:, :, :L]
        return jnp.swapaxes(t, 1, 2).reshape(B, S, *t.shape[3:])

    return from_blocks(o), from_blocks(lse)


def dilated_attention_sample(q, k, v, buf, window, dil):
    N, T, H, Dh = q.shape
    Wb = buf.shape[1]
    span = window // dil
    k_all = jnp.concatenate([buf[:, :, 0].astype(k.dtype), k], axis=1)
    v_all = jnp.concatenate([buf[:, :, 1].astype(v.dtype), v], axis=1)
    jj = np.arange(span + 1)
    idx = Wb + np.arange(T)[:, None] - dil * jj[None, :]
    valid = idx >= 0
    idx_c = np.maximum(idx, 0)
    kg = k_all[:, idx_c]
    vg = v_all[:, idx_c]
    s = jnp.einsum('nthd,ntjhd->nthj', q, kg, preferred_element_type=jnp.float32) * (Dh ** -0.5)
    bias = -alibi_slopes()[:, None] * jnp.asarray(dil * jj, jnp.float32)[None, :]
    s = jnp.where(valid[:, None, :], s + bias, -jnp.inf)
    lse = jax.nn.logsumexp(s, axis=-1)
    p = jnp.exp(s - lse[..., None]).astype(v.dtype)
    o = jnp.einsum('nthj,ntjhd->nthd', p, vg)
    return o, lse


def mlstm_chunkwise(q, k, v, i_pre, f_pre, C0, n0, m0):
    B, S, H, DK = q.shape
    DV = v.shape[-1]
    Lc = math.gcd(S, MLSTM_CHUNK)
    nc = S // Lc
    f32 = jnp.float32

    def chunks(t):
        t = t.astype(f32).reshape(B, nc, Lc, H, *t.shape[3:])
        return jnp.moveaxis(t, (1, 3), (0, 2))

    qc, kc, vc = chunks(q), chunks(k) * (DK ** -0.5), chunks(v)
    ic, fc = chunks(i_pre), jax.nn.log_sigmoid(chunks(f_pre))
    causal = np.tril(np.ones((Lc, Lc), dtype=bool))

    def step(carry, xs):
        C, n, m = carry
        q_, k_, v_, i_, lf = xs
        b = jnp.cumsum(lf, axis=-1)
        D = jnp.where(causal, b[..., :, None] - b[..., None, :] + i_[..., None, :], -jnp.inf)
        m_inter = b + m[..., None]
        m_t = jnp.maximum(m_inter, jnp.max(D, axis=-1))
        w_intra = jnp.exp(D - m_t[..., None]) * jnp.einsum('bhtd,bhsd->bhts', q_, k_)
        w_inter = jnp.exp(m_inter - m_t)
        num = jnp.einsum('bhts,bhsv->bhtv', w_intra, v_) + w_inter[..., None] * jnp.einsum('bhtd,bhdv->bhtv', q_, C)
        den = jnp.sum(w_intra, axis=-1) + w_inter * jnp.einsum('bhtd,bhd->bht', q_, n)
        h = num / jnp.maximum(jnp.abs(den), jnp.exp(-m_t))[..., None]
        b_last = b[..., -1]
        w_src = b_last[..., None] - b + i_
        m_new = jnp.maximum(b_last + m, jnp.max(w_src, axis=-1))
        decay = jnp.exp(b_last + m - m_new)
        p_src = jnp.exp(w_src - m_new[..., None])
        C_new = decay[..., None, None] * C + jnp.einsum('bhs,bhsd,bhsv->bhdv', p_src, k_, v_)
        n_new = decay[..., None] * n + jnp.einsum('bhs,bhsd->bhd', p_src, k_)
        return (C_new, n_new, m_new), h

    (C, n, m), h = lax.scan(step, (C0.astype(f32), n0.astype(f32), m0.astype(f32)), (qc, kc, vc, ic, fc))
    h = jnp.moveaxis(h, (0, 2), (1, 3)).reshape(B, S, H, DV)
    return h, (C, n, m)


def moe_ffn(h, w_router, b_router, w_gu, b_gu, w_down, b_down):
    T, D = h.shape
    logits = jnp.dot(h, w_router, preferred_element_type=jnp.float32) + b_router.astype(jnp.float32)
    top_val, top_idx = lax.top_k(logits, TOP_K)
    gates = jax.nn.softmax(top_val, axis=-1)
    A = T * TOP_K
    e_flat = top_idx.reshape(A)
    order = jnp.argsort(e_flat)
    e_sorted = e_flat[order]
    tok_sorted = order // TOP_K
    counts = jnp.zeros((N_EXPERTS,), jnp.int32).at[e_flat].add(1)
    padded = (counts + MOE_BLOCK - 1) // MOE_BLOCK * MOE_BLOCK
    start = jnp.cumsum(counts) - counts
    pad_end = jnp.cumsum(padded)
    pad_start = pad_end - padded
    dest = pad_start[e_sorted] + jnp.arange(A, dtype=jnp.int32) - start[e_sorted]
    n_blocks = (A + N_EXPERTS * (MOE_BLOCK - 1) + MOE_BLOCK - 1) // MOE_BLOCK
    rows = jnp.zeros((n_blocks * MOE_BLOCK, D), h.dtype).at[dest].set(h[tok_sorted])
    block_expert = jnp.minimum(
        jnp.searchsorted(pad_end, jnp.arange(n_blocks, dtype=jnp.int32) * MOE_BLOCK, side='right'), N_EXPERTS - 1)

    def expert_block(args):
        xb, e = args
        gu = xb @ w_gu[e] + b_gu[e]
        gate = jnp.minimum(gu[:, :D_EXPERT], SWIGLU_LIMIT)
        up = jnp.clip(gu[:, D_EXPERT:], -SWIGLU_LIMIT, SWIGLU_LIMIT)
        act = (up + 1) * gate * jax.nn.sigmoid(SWIGLU_ALPHA * gate)
        return act @ w_down[e] + b_down[e]

    y = lax.map(expert_block, (rows.reshape(n_blocks, MOE_BLOCK, D), block_expert)).reshape(-1, D)
    contrib = y[dest].astype(jnp.float32) * gates.reshape(A)[order][:, None]
    return jax.ops.segment_sum(contrib, tok_sorted, num_segments=T).astype(h.dtype)


def trunk_layer(x, c, kv_bufs, mstate, w_ada, b_ada, g_mix, g_ffn, w_in, b_gates, g_mlstm,
                w_br_attn, w_br_mlstm, w_out, w_router, b_router, w_gu, b_gu, w_down, b_down):
    B, S, D = x.shape
    mod = (jax.nn.silu(c) @ w_ada + b_ada)[:, None, :]
    sh1, sc1, gt1, sh2, sc2, gt2 = jnp.split(mod, 6, axis=-1)
    h = modulated_norm(x, g_mix, sh1, sc1)
    z = h @ w_in
    q_a, k_a, v_a, q_m, k_m, v_m, o_m, if_m, g_a, g_m = jnp.split(z, IN_SPLITS, axis=-1)

    def grp(t):
        return t.reshape(B, S, N_GROUPS, HEADS_PER_GROUP, HEAD_DIM)
    q_a, k_a, v_a = grp(q_a), grp(k_a), grp(v_a)
    outs, lses, new_kv = [], [], []
    for gi, (win, dil) in enumerate(ATTN_GROUPS):
        qg, kg, vg = q_a[:, :, gi], k_a[:, :, gi], v_a[:, :, gi]
        if kv_bufs is None:
            o, lse = dilated_attention_prompt(qg, kg, vg, win, dil)
            new_kv.append(jnp.stack([kg, vg], axis=2)[:, S - min(win, S):])
        else:
            o, lse = dilated_attention_sample(qg, kg, vg, kv_bufs[gi], win, dil)
            new_kv.append(jnp.stack([kg, vg], axis=2))
        outs.append(o)
        lses.append(lse)
    wgt = jax.nn.softmax(jnp.stack(lses, axis=0), axis=0)
    attn = jnp.sum(wgt[..., None] * jnp.stack(outs, axis=0).astype(jnp.float32), axis=0)
    y_attn = attn.reshape(B, S, ATTN_OUT).astype(x.dtype) @ w_br_attn

    gates = (if_m + b_gates).astype(jnp.float32)
    hm, new_m = mlstm_chunkwise(q_m.reshape(B, S, MLSTM_HEADS, MLSTM_DK), k_m.reshape(B, S, MLSTM_HEADS, MLSTM_DK),
                                v_m.reshape(B, S, MLSTM_HEADS, MLSTM_DV), gates[..., :MLSTM_HEADS],
                                gates[..., MLSTM_HEADS:], *mstate)
    hm = rms_norm(hm, g_mlstm.reshape(MLSTM_HEADS, MLSTM_DV)).reshape(B, S, MLSTM_V)
    y_mlstm = (hm * jax.nn.sigmoid(o_m.astype(jnp.float32))).astype(x.dtype) @ w_br_mlstm

    merged = jax.nn.sigmoid(g_a) * y_attn + jax.nn.sigmoid(g_m) * y_mlstm
    x = x + gt1 * (merged @ w_out)

    h2 = modulated_norm(x, g_ffn, sh2, sc2)
    x = x + gt2 * moe_ffn(h2.reshape(B * S, D), w_router, b_router, w_gu, b_gu, w_down, b_down).reshape(B, S, D)
    return x, new_kv, new_m


def setup_inputs(seed: int = 0) -> dict:
    key = jax.random.key(seed)
    ks = jax.random.split(key, 32)
    counter = [0]

    def nrm(shape, scale):
        kk = ks[counter[0]]
        counter[0] += 1
        return scale * jax.random.normal(kk, shape, jnp.float32)

    D = D_MODEL
    inp = {}
    inp['x_prompt'] = nrm((BATCH, SEQ, D), 1.0)
    inp['x_sample'] = nrm((DEC_BATCH, DEC_SEQ, D), 1.0)
    inp['cache_kv_w128'] = nrm((DEPTH, DEC_BATCH, min(128, PAST_LEN), 2, HEADS_PER_GROUP, HEAD_DIM), 1.0)
    inp['cache_kv_w512'] = nrm((DEPTH, DEC_BATCH, min(512, PAST_LEN), 2, HEADS_PER_GROUP, HEAD_DIM), 1.0)
    inp['cache_kv_w2048'] = nrm((DEPTH, DEC_BATCH, min(2048, PAST_LEN), 2, HEADS_PER_GROUP, HEAD_DIM), 1.0)
    inp['state_mlstm_C'] = nrm((DEPTH, DEC_BATCH, MLSTM_HEADS, MLSTM_DK, MLSTM_DV), 0.1)
    inp['state_mlstm_n'] = nrm((DEPTH, DEC_BATCH, MLSTM_HEADS, MLSTM_DK), 0.1)
    inp['state_mlstm_m'] = nrm((DEPTH, DEC_BATCH, MLSTM_HEADS), 0.5)
    inp['c_prompt'] = nrm((BATCH, D), 1.0)
    inp['c_sample'] = nrm((DEC_BATCH, D), 1.0)
    inp['w_ada'] = nrm((DEPTH, D, 6 * D), 0.5 * D ** -0.5)
    inp['b_ada'] = nrm((DEPTH, 6 * D), 0.01)
    inp['g_mix'] = 1.0 + nrm((DEPTH, D), 0.02)
    inp['g_ffn'] = 1.0 + nrm((DEPTH, D), 0.02)
    inp['w_in'] = nrm((DEPTH, D, D_IN), D ** -0.5)
    inp['b_gates'] = jnp.concatenate(
        [nrm((DEPTH, MLSTM_HEADS), 0.1),
         jnp.linspace(3.0, 6.0, MLSTM_HEADS, dtype=jnp.float32)[None, :] + nrm((DEPTH, MLSTM_HEADS), 0.1)], axis=-1)
    inp['g_mlstm'] = 1.0 + nrm((DEPTH, MLSTM_V), 0.02)
    inp['w_br_attn'] = nrm((DEPTH, ATTN_OUT, D), ATTN_OUT ** -0.5)
    inp['w_br_mlstm'] = nrm((DEPTH, MLSTM_V, D), MLSTM_V ** -0.5)
    inp['w_out'] = nrm((DEPTH, D, D), D ** -0.5)
    inp['w_router'] = nrm((DEPTH, D, N_EXPERTS), D ** -0.5)
    inp['b_router'] = nrm((DEPTH, N_EXPERTS), 0.01)
    inp['w_gu'] = nrm((DEPTH, N_EXPERTS, D, 2 * D_EXPERT), D ** -0.5)
    inp['b_gu'] = nrm((DEPTH, N_EXPERTS, 2 * D_EXPERT), 0.01)
    inp['w_down'] = nrm((DEPTH, N_EXPERTS, D_EXPERT, D), D_EXPERT ** -0.5)
    inp['b_down'] = nrm((DEPTH, N_EXPERTS, D), 0.01)
    inp['g_final'] = 1.0 + nrm((D,), 0.02)
    return inp


def reference(x_prompt, x_sample, cache_kv_w128, cache_kv_w512, cache_kv_w2048, state_mlstm_C, state_mlstm_n,
              state_mlstm_m, c_prompt, c_sample, w_ada, b_ada, g_mix, g_ffn, w_in, b_gates, g_mlstm, w_br_attn,
              w_br_mlstm, w_out, w_router, b_router, w_gu, b_gu, w_down, b_down, g_final):
    B = x_prompt.shape[0]
    zero_state = (jnp.zeros((B, MLSTM_HEADS, MLSTM_DK, MLSTM_DV), jnp.float32),
                  jnp.zeros((B, MLSTM_HEADS, MLSTM_DK), jnp.float32),
                  jnp.zeros((B, MLSTM_HEADS), jnp.float32))
    xp, xs = x_prompt, x_sample
    kvp = ([], [], [])
    kvs = ([], [], [])
    mp = ([], [], [])
    ms = ([], [], [])
    for l in range(DEPTH):
        lw = (w_ada[l], b_ada[l], g_mix[l], g_ffn[l], w_in[l], b_gates[l], g_mlstm[l], w_br_attn[l],
              w_br_mlstm[l], w_out[l], w_router[l], b_router[l], w_gu[l], b_gu[l], w_down[l], b_down[l])
        xp, kv_new, st_new = trunk_layer(xp, c_prompt, None, zero_state, *lw)
        for gi in range(N_GROUPS):
            kvp[gi].append(kv_new[gi])
        for si in range(3):
            mp[si].append(st_new[si])
        bufs = (cache_kv_w128[l], cache_kv_w512[l], cache_kv_w2048[l])
        xs, kv_new, st_new = trunk_layer(xs, c_sample, bufs, (state_mlstm_C[l], state_mlstm_n[l], state_mlstm_m[l]), *lw)
        for gi in range(N_GROUPS):
            kvs[gi].append(kv_new[gi])
        for si in range(3):
            ms[si].append(st_new[si])
    y_prompt = rms_norm(xp, g_final)
    y_sample = rms_norm(xs, g_final)
    return (y_prompt, y_sample,
            jnp.stack(kvp[0]), jnp.stack(kvp[1]), jnp.stack(kvp[2]),
            jnp.stack(mp[0]), jnp.stack(mp[1]), jnp.stack(mp[2]),
            jnp.stack(kvs[0]), jnp.stack(kvs[1]), jnp.stack(kvs[2]),
            jnp.stack(ms[0]), jnp.stack(ms[1]), jnp.stack(ms[2]))
```

```python
import functools

import jax
import jax.numpy as jnp
import numpy as np
from jax import lax
from jax.experimental import pallas as pl
from jax.experimental.pallas import tpu as pltpu

F32 = jnp.float32
BF16 = jnp.bfloat16

ATTN_GROUPS = ((128, 1), (512, 4), (2048, 16))
N_GROUPS = len(ATTN_GROUPS)
HEADS = 8
HEAD_DIM = 64
ATTN_BLOCK = 128
GROUP_W = HEADS * HEAD_DIM
ATTN_QKV = N_GROUPS * GROUP_W
MLSTM_HEADS = 4
MLSTM_DK = 128
MLSTM_DV = 256
N_EXPERTS = 32
TOP_K = 4
SWIGLU_LIMIT = 7.0
SWIGLU_ALPHA = 1.702
RMS_EPS = 1e-6
NEG_BIG = -1e30

LANES = 128
SUBLANES = 8
VMEM_LIMIT_BYTES = 56 * 1024 * 1024

ROW_TILE = 256
ATTN_TILE = 2048
MLSTM_CHUNK = 256
MOE_BLOCK = 256

ML_W = 2 * MLSTM_HEADS * MLSTM_DK + 2 * MLSTM_HEADS * MLSTM_DV
ML_PAD_W = ML_W + LANES
GATE_W = 2048
W_QKV0 = 0
W_ML0 = 3 * ATTN_QKV
W_GT0 = W_ML0 + ML_PAD_W
W_TOTAL = W_GT0 + GATE_W


def _cparams(*sem):
    return pltpu.CompilerParams(dimension_semantics=sem, vmem_limit_bytes=VMEM_LIMIT_BYTES)


def _dot(a, b):
    return jnp.dot(a, b, preferred_element_type=F32)


def _dot_nt(a, b):
    return lax.dot_general(a, b, (((1,), (1,)), ((), ())), preferred_element_type=F32)


def _dot_tn(a, b):
    return lax.dot_general(a, b, (((0,), (0,)), ((), ())), preferred_element_type=F32)


def _ada_kernel(c_ref, w_ref, b_ref, o_ref):
    c = c_ref[...]
    s = c * jax.nn.sigmoid(c)
    o_ref[...] = _dot(s.astype(BF16), w_ref[...].astype(BF16)) + b_ref[...]


def _ada_mod(c_all, w_ada, b_ada):
    depth, d, n = w_ada.shape
    bc = c_all.shape[0]
    tn = 1024
    return pl.pallas_call(
        _ada_kernel,
        grid=(depth, n // tn),
        in_specs=[
            pl.BlockSpec((bc, d), lambda l, j: (0, 0)),
            pl.BlockSpec((None, d, tn), lambda l, j: (l, 0, j)),
            pl.BlockSpec((None, 1, tn), lambda l, j: (l, 0, j)),
        ],
        out_specs=pl.BlockSpec((None, bc, tn), lambda l, j: (l, 0, j)),
        out_shape=jax.ShapeDtypeStruct((depth, bc, n), F32),
        compiler_params=_cparams("parallel", "parallel"),
        name="ada_mod",
    )(c_all, w_ada, b_ada.reshape(depth, 1, n))


def _mod_specs(chunk, tm, n_prompt_tiles, tiles_per_batch, n_batch):
    d = 1024
    sp = pl.BlockSpec((None, 1, d), lambda i: (jnp.minimum(i // tiles_per_batch, n_batch - 1), 0, chunk))
    ss = pl.BlockSpec((tm, d), lambda i: (jnp.maximum(i - n_prompt_tiles, 0), chunk))
    return sp, ss


def _rms(x):
    return x * lax.rsqrt(jnp.mean(x * x, axis=-1, keepdims=True) + RMS_EPS)


def _inproj_kernel(x_ref, g_ref, shp_ref, scp_ref, shs_ref, scs_ref, w_ref, qkv_ref, ml_ref, gt_ref,
                   *, n_prompt_tiles):
    is_sample = pl.program_id(0) >= n_prompt_tiles
    sh = jnp.where(is_sample, shs_ref[...], shp_ref[...])
    sc = jnp.where(is_sample, scs_ref[...], scp_ref[...])
    h = (_rms(x_ref[...]) * g_ref[...] * (1.0 + sc) + sh).astype(BF16)
    cw = 512
    for c0 in range(0, 3 * ATTN_QKV, cw):
        qkv_ref[:, c0:c0 + cw] = _dot(h, w_ref[:, W_QKV0 + c0:W_QKV0 + c0 + cw])
    for c0 in range(0, ML_PAD_W, 640):
        ml_ref[:, c0:c0 + 640] = _dot(h, w_ref[:, W_ML0 + c0:W_ML0 + c0 + 640])
    for c0 in range(0, GATE_W, cw):
        gt_ref[:, c0:c0 + cw] = _dot(h, w_ref[:, W_GT0 + c0:W_GT0 + c0 + cw])


def _inproj(x_all, g_mix, mod_p, mod_s, w_packed, tp, n_batch):
    t_all, d = x_all.shape
    tm = ROW_TILE
    n_prompt_tiles = tp // tm
    tiles_per_batch = n_prompt_tiles // n_batch
    shp, shs = _mod_specs(0, tm, n_prompt_tiles, tiles_per_batch, n_batch)
    scp, scs = _mod_specs(1, tm, n_prompt_tiles, tiles_per_batch, n_batch)
    row = lambda w: pl.BlockSpec((tm, w), lambda i: (i, 0))
    return pl.pallas_call(
        functools.partial(_inproj_kernel, n_prompt_tiles=n_prompt_tiles),
        grid=(t_all // tm,),
        in_specs=[
            row(d),
            pl.BlockSpec((1, d), lambda i: (0, 0)),
            shp, scp, shs, scs,
            pl.BlockSpec((d, W_TOTAL), lambda i: (0, 0), pipeline_mode=pl.Buffered(1)),
        ],
        out_specs=[row(3 * ATTN_QKV), row(ML_PAD_W), row(GATE_W)],
        out_shape=[
            jax.ShapeDtypeStruct((t_all, 3 * ATTN_QKV), F32),
            jax.ShapeDtypeStruct((t_all, ML_PAD_W), F32),
            jax.ShapeDtypeStruct((t_all, GATE_W), F32),
        ],
        compiler_params=_cparams("parallel"),
        name="inproj",
    )(x_all, g_mix.reshape(1, d), mod_p, mod_p, mod_s, mod_s, w_packed)


def _attn_unit(q, k2, v2, slopes, dil, valid_prev):
    lane = lax.broadcasted_iota(jnp.int32, (ATTN_BLOCK, LANES), 1)
    qi = lax.broadcasted_iota(jnp.int32, (ATTN_BLOCK, 2 * ATTN_BLOCK), 0)
    ki = lax.broadcasted_iota(jnp.int32, (ATTN_BLOCK, 2 * ATTN_BLOCK), 1)
    dist = qi + ATTN_BLOCK - ki
    keep = (dist >= 0) & (dist <= ATTN_BLOCK) & ((ki >= ATTN_BLOCK) | valid_prev)
    distf = (dil * dist).astype(F32)
    k2b = k2.astype(BF16)
    v2b = v2.astype(BF16)
    outs = []
    for e in range(2):
        in_head = (lane >= e * HEAD_DIM) & (lane < (e + 1) * HEAD_DIM)
        qh = jnp.where(in_head, q, 0.0).astype(BF16)
        s = _dot_nt(qh, k2b) * (HEAD_DIM ** -0.5)
        s = jnp.where(keep, s - slopes[e] * distf, NEG_BIG)
        m = jnp.max(s, axis=-1, keepdims=True)
        p = jnp.exp(s - m)
        l = jnp.sum(p, axis=-1, keepdims=True)
        o = _dot(p.astype(BF16), v2b) / l
        outs.append((o, m + jnp.log(l)))
    first = lane < HEAD_DIM
    return jnp.where(first, outs[0][0], outs[1][0]), jnp.where(first, outs[0][1], outs[1][1])


def _attn_prompt_kernel(slope_ref, *refs):
    ins, o_ref, o_s, l_s = refs[:15], refs[15], refs[16], refs[17]
    hp = pl.program_id(1)
    tile = pl.program_id(2)
    slopes = (slope_ref[2 * hp], slope_ref[2 * hp + 1])
    for g, (_, dil) in enumerate(ATTN_GROUPS):
        q_ref, kc_ref, vc_ref, kp_ref, vp_ref = ins[5 * g:5 * g + 5]
        sub = ATTN_BLOCK * dil
        n_first = dil
        n_units = ATTN_TILE // ATTN_BLOCK

        def strided(ref, start, size):
            return ref[pl.ds(start, size, stride=dil), :] if dil > 1 else ref[pl.ds(start, size), :]

        def store(start, o, lse):
            if dil > 1:
                o_s[g, pl.ds(start, ATTN_BLOCK, stride=dil), :] = o
                l_s[g, pl.ds(start, ATTN_BLOCK, stride=dil), :] = lse
            else:
                o_s[g, pl.ds(start, ATTN_BLOCK), :] = o
                l_s[g, pl.ds(start, ATTN_BLOCK), :] = lse

        def first_body(r, carry):
            q = strided(q_ref, r, ATTN_BLOCK)
            k2 = jnp.concatenate([strided(kp_ref, r, ATTN_BLOCK), strided(kc_ref, r, ATTN_BLOCK)], axis=0)
            v2 = jnp.concatenate([strided(vp_ref, r, ATTN_BLOCK), strided(vc_ref, r, ATTN_BLOCK)], axis=0)
            o, lse = _attn_unit(q, k2, v2, slopes, dil, tile > 0)
            store(r, o, lse)
            return carry

        lax.fori_loop(0, n_first, first_body, 0)

        def rest_body(u, carry):
            j = u // dil
            r = u - j * dil
            start = j * sub + r
            q = strided(q_ref, start, ATTN_BLOCK)
            k2 = strided(kc_ref, start - sub, 2 * ATTN_BLOCK)
            v2 = strided(vc_ref, start - sub, 2 * ATTN_BLOCK)
            o, lse = _attn_unit(q, k2, v2, slopes, dil, True)
            store(start, o, lse)
            return carry

        if n_units > n_first:
            lax.fori_loop(n_first, n_units, rest_body, 0)

    m = jnp.maximum(jnp.maximum(l_s[0], l_s[1]), l_s[2])
    w0 = jnp.exp(l_s[0] - m)
    w1 = jnp.exp(l_s[1] - m)
    w2 = jnp.exp(l_s[2] - m)
    o_ref[...] = (w0 * o_s[0] + w1 * o_s[1] + w2 * o_s[2]) / (w0 + w1 + w2)


def _alibi_slopes():
    return jnp.asarray(2.0 ** (-8.0 * np.arange(1, HEADS + 1) / HEADS), dtype=F32)


def _attn_prompt(qkv, n_batch, seq, t_all):
    tiles = seq // ATTN_TILE
    slabs = GROUP_W // LANES
    in_specs = [pl.BlockSpec(memory_space=pltpu.SMEM)]
    args = [_alibi_slopes()]
    for g, (_, dil) in enumerate(ATTN_GROUPS):
        sub = ATTN_BLOCK * dil
        per_tile = ATTN_TILE // sub
        for which in range(3):
            col = which * (ATTN_QKV // LANES) + g * slabs
            in_specs.append(pl.BlockSpec((ATTN_TILE, LANES), lambda b, hp, t, col=col: (b * tiles + t, col + hp)))
            args.append(qkv)
        for which in (1, 2):
            col = which * (ATTN_QKV // LANES) + g * slabs
            in_specs.append(pl.BlockSpec(
                (sub, LANES),
                lambda b, hp, t, col=col, per_tile=per_tile: (jnp.maximum((b * tiles + t) * per_tile - 1, 0), col + hp)))
            args.append(qkv)
    return pl.pallas_call(
        _attn_prompt_kernel,
        grid=(n_batch, slabs, tiles),
        in_specs=in_specs,
        out_specs=pl.BlockSpec((ATTN_TILE, LANES), lambda b, hp, t: (b * tiles + t, hp)),
        out_shape=jax.ShapeDtypeStruct((t_all, GROUP_W), F32),
        scratch_shapes=[pltpu.VMEM((N_GROUPS, ATTN_TILE, LANES), F32), pltpu.VMEM((N_GROUPS, ATTN_TILE, LANES), F32)],
        compiler_params=_cparams("parallel", "parallel", "arbitrary"),
        name="attn_prompt",
    )(*args)


def _attn_sample_kernel(slope_ref, qkv_ref, c0_ref, c1_ref, c2_ref, prev_ref, o_ref, *, dec_seq):
    n = pl.program_id(0)
    half = n % 2
    caches = (c0_ref, c1_ref, c2_ref)
    rows = 2 * SUBLANES
    row = lax.broadcasted_iota(jnp.int32, (rows, 1), 0)
    t_row = (row % SUBLANES) - half * dec_seq
    row_ok = (t_row >= 0) & (t_row < dec_seq)
    lane = lax.broadcasted_iota(jnp.int32, (SUBLANES, LANES), 1)
    first = lane < HEAD_DIM
    ucol = lax.broadcasted_iota(jnp.int32, (rows, SUBLANES), 1) - half * dec_seq
    col_ok = (ucol >= 0) & (ucol < dec_seq)
    out_slabs = []
    for j in range(GROUP_W // LANES):
        slope = jnp.where(row < SUBLANES, slope_ref[2 * j], slope_ref[2 * j + 1])
        o_g, l_g = [], []
        for g, (win, dil) in enumerate(ATTN_GROUPS):
            cache = caches[g]
            wb = cache.shape[-1]
            c = g * (GROUP_W // LANES) + j
            q = qkv_ref[:, c * LANES:(c + 1) * LANES]
            kn = qkv_ref[:, ATTN_QKV + c * LANES:ATTN_QKV + (c + 1) * LANES]
            vn = qkv_ref[:, 2 * ATTN_QKV + c * LANES:2 * ATTN_QKV + (c + 1) * LANES]
            q2f = jnp.concatenate([jnp.where(first, q, 0.0), jnp.where(first, 0.0, q)], axis=0)
            q2 = q2f.astype(BF16)
            kt = cache[0, j].astype(BF16)
            vt = cache[1, j].astype(BF16)
            scale = HEAD_DIM ** -0.5
            w = lax.broadcasted_iota(jnp.int32, (rows, wb), 1)
            delta = wb + t_row - w
            ok = row_ok & (delta <= win) & ((delta & (dil - 1)) == 0)
            s_b = jnp.where(ok, _dot(q2, kt) * scale - slope * delta.astype(F32), NEG_BIG)
            dn = t_row - ucol
            okn = row_ok & col_ok & (dn >= 0) & (dn <= win) & ((dn & (dil - 1)) == 0)
            s_n = jnp.where(okn, _dot_nt(q2f, kn) * scale - slope * dn.astype(F32), NEG_BIG)
            m = jnp.maximum(jnp.max(s_b, axis=-1, keepdims=True), jnp.max(s_n, axis=-1, keepdims=True))
            p_b = jnp.exp(s_b - m)
            p_n = jnp.exp(s_n - m)
            l = jnp.sum(p_b, axis=-1, keepdims=True) + jnp.sum(p_n, axis=-1, keepdims=True)
            o = (_dot_nt(p_b.astype(BF16), vt) + _dot(p_n, vn)) / l
            lse = m + jnp.log(l)
            o_g.append(jnp.where(first, o[:SUBLANES], o[SUBLANES:]))
            l_g.append(jnp.where(first, lse[:SUBLANES], lse[SUBLANES:]))
        m = jnp.maximum(jnp.maximum(l_g[0], l_g[1]), l_g[2])
        ws = [jnp.exp(lg - m) for lg in l_g]
        out_slabs.append((ws[0] * o_g[0] + ws[1] * o_g[1] + ws[2] * o_g[2]) / (ws[0] + ws[1] + ws[2]))
    res = jnp.concatenate(out_slabs, axis=1)
    mine = (lax.broadcasted_iota(jnp.int32, (SUBLANES, 1), 0) // dec_seq) == half

    @pl.when(half == 0)
    def _():
        o_ref[...] = res

    @pl.when(half != 0)
    def _():
        o_ref[...] = jnp.where(mine, res, o_ref[...])


def _attn_sample(qkv, caches_t, layer, attn, tp, n_seq, dec_seq):
    assert 2 * dec_seq == SUBLANES
    blk0 = tp // SUBLANES
    in_specs = [
        pl.BlockSpec(memory_space=pltpu.SMEM),
        pl.BlockSpec((SUBLANES, 3 * ATTN_QKV), lambda n: (blk0 + n // 2, 0)),
    ]
    for c in caches_t:
        in_specs.append(pl.BlockSpec((None, None) + c.shape[2:], lambda n, layer=layer: (layer, n, 0, 0, 0, 0)))
    in_specs.append(pl.BlockSpec(memory_space=pl.ANY))
    return pl.pallas_call(
        functools.partial(_attn_sample_kernel, dec_seq=dec_seq),
        grid=(n_seq,),
        in_specs=in_specs,
        out_specs=pl.BlockSpec((SUBLANES, GROUP_W), lambda n: (blk0 + n // 2, 0)),
        out_shape=jax.ShapeDtypeStruct(attn.shape, attn.dtype),
        input_output_aliases={5: 0},
        compiler_params=_cparams("arbitrary"),
        name="attn_sample",
    )(_alibi_slopes(), qkv, *caches_t, attn)


def _log_sigmoid(x):
    return jnp.minimum(x, 0.0) - jnp.log1p(jnp.exp(-jnp.abs(x)))


def _mlstm_chunk(q, k, v, i_col, lf_col, c_st, n_st, m_st):
    ln = q.shape[0]
    ii = lax.broadcasted_iota(jnp.int32, (ln, ln), 0)
    jj = lax.broadcasted_iota(jnp.int32, (ln, ln), 1)
    eye = ii == jj
    causal = jj <= ii
    lf_row = jnp.sum(jnp.where(eye, lf_col, 0.0), axis=0, keepdims=True)
    i_row = jnp.sum(jnp.where(eye, i_col, 0.0), axis=0, keepdims=True)
    b_col = jnp.sum(jnp.where(causal, lf_row, 0.0), axis=1, keepdims=True)
    b_row = jnp.sum(jnp.where(ii <= jj, lf_col, 0.0), axis=0, keepdims=True)
    dm = jnp.where(causal, b_col - b_row + i_row, -jnp.inf)
    m_inter = b_col + m_st
    m_t = jnp.maximum(m_inter, jnp.max(dm, axis=1, keepdims=True))
    ks = k * (MLSTM_DK ** -0.5)
    qb = q.astype(BF16)
    ksb = ks.astype(BF16)
    vb = v.astype(BF16)
    w_intra = jnp.exp(dm - m_t) * _dot_nt(qb, ksb)
    w_inter = jnp.exp(m_inter - m_t)
    num = _dot(w_intra.astype(BF16), vb) + w_inter * _dot(qb, c_st.astype(BF16))
    den = jnp.sum(w_intra, axis=1, keepdims=True) + w_inter * jnp.sum(q * n_st, axis=1, keepdims=True)
    h = num / jnp.maximum(jnp.abs(den), jnp.exp(-m_t))
    b_last = b_col[ln - 1:ln, :]
    w_src = b_last - b_col + i_col
    m_new = jnp.maximum(b_last + m_st, jnp.max(w_src, axis=0, keepdims=True))
    decay = jnp.exp(b_last + m_st - m_new)
    kp = jnp.exp(w_src - m_new) * ks
    c_new = decay * c_st + _dot_tn(kp.astype(BF16), vb)
    n_new = decay * n_st + jnp.sum(kp, axis=0, keepdims=True)
    return h, c_new, n_new, m_new


def _mlstm_kernel(q_ref, k_ref, v_ref, og_ref, gt_ref, bg_ref, gn_ref, c0_ref, n0_ref, m0_ref,
                  hm_ref, c_out, n_out, m_out, c_s, n_s, m_s, *, chunks, seq_rows):
    head = pl.program_id(1)
    step = pl.program_id(2)
    ln = q_ref.shape[0]

    @pl.when((step == 0) | (chunks == 1))
    def _():
        c_s[...] = c0_ref[...]
        n_s[...] = n0_ref[...]
        m_s[...] = m0_ref[...]

    gates = gt_ref[...] + bg_ref[...]
    lane = lax.broadcasted_iota(jnp.int32, gates.shape, 1)
    i_col = jnp.sum(jnp.where(lane == head, gates, 0.0), axis=1, keepdims=True)
    f_col = jnp.sum(jnp.where(lane == head + MLSTM_HEADS, gates, 0.0), axis=1, keepdims=True)
    lf_col = _log_sigmoid(f_col)
    if chunks == 1:
        row = lax.broadcasted_iota(jnp.int32, (ln, 1), 0)
        mine = (row // seq_rows) == step
        i_col = jnp.where(mine, i_col, NEG_BIG)
        lf_col = jnp.where(mine, lf_col, 0.0)

    h, c_new, n_new, m_new = _mlstm_chunk(q_ref[...], k_ref[...], v_ref[...], i_col, lf_col,
                                          c_s[...], n_s[...], m_s[...])
    c_s[...] = c_new
    n_s[...] = n_new
    m_s[...] = m_new
    hn = _rms(h) * gn_ref[...] * jax.nn.sigmoid(og_ref[...])
    if chunks == 1:
        @pl.when(step == 0)
        def _():
            hm_ref[...] = hn

        @pl.when(step != 0)
        def _():
            hm_ref[...] = jnp.where(mine, hn, hm_ref[...])
    else:
        hm_ref[...] = hn

    @pl.when((step == chunks - 1) | (chunks == 1))
    def _():
        c_out[...] = c_new
        n_out[...] = n_new
        m_out[...] = m_new


def _mlstm(ml, b_gates_pad, g_mlstm, c0, n0, m0, layer, hm_prev, *, row0, n_seq, seq_len, t_all):
    if seq_len >= MLSTM_CHUNK:
        ln = MLSTM_CHUNK
        chunks = seq_len // ln
        groups, steps, seq_rows = n_seq, chunks, ln
        rblk = lambda s, c: row0 // ln + s * chunks + c
        sidx = lambda s, c: s
    else:
        ln = SUBLANES
        chunks = 1
        per_blk = ln // seq_len
        groups, steps, seq_rows = n_seq // per_blk, per_blk, seq_len
        rblk = lambda s, c: row0 // ln + s
        sidx = lambda s, c: s * per_blk + c
    qk_blk = lambda off: pl.BlockSpec((ln, MLSTM_DK), lambda s, h, c: (rblk(s, c), off + h))
    v_blk = lambda off: pl.BlockSpec((ln, MLSTM_DV), lambda s, h, c: (rblk(s, c), off + h))
    st = lambda shp: pl.BlockSpec((None, None) + shp, lambda s, h, c: (sidx(s, c), h, 0, 0))
    st_in = lambda shp: pl.BlockSpec((None, None, None) + shp, lambda s, h, c: (layer, sidx(s, c), h, 0, 0))
    in_specs = [
        qk_blk(0), qk_blk(MLSTM_HEADS), v_blk(MLSTM_HEADS), v_blk(2 * MLSTM_HEADS),
        pl.BlockSpec((ln, LANES), lambda s, h, c: (rblk(s, c), ML_W // LANES)),
        pl.BlockSpec((1, LANES), lambda s, h, c: (0, 0)),
        pl.BlockSpec((1, MLSTM_DV), lambda s, h, c: (0, h)),
        st_in((MLSTM_DK, MLSTM_DV)), st_in((1, MLSTM_DK)), st_in((1, 1)),
    ]
    args = [ml, ml, ml, ml, ml, b_gates_pad, g_mlstm.reshape(1, -1), c0, n0, m0]
    aliases = {}
    if hm_prev is not None:
        in_specs.append(pl.BlockSpec(memory_space=pl.ANY))
        args.append(hm_prev)
        aliases = {len(args) - 1: 0}
    kern = functools.partial(_mlstm_kernel, chunks=chunks, seq_rows=seq_rows)
    if hm_prev is not None:
        kern_inner = kern
        kern = lambda *r: kern_inner(*r[:10], *r[11:])
    return pl.pallas_call(
        kern,
        grid=(groups, MLSTM_HEADS, steps),
        in_specs=in_specs,
        out_specs=[
            pl.BlockSpec((ln, MLSTM_DV), lambda s, h, c: (rblk(s, c), h)),
            st((MLSTM_DK, MLSTM_DV)), st((1, MLSTM_DK)), st((1, 1)),
        ],
        out_shape=[
            jax.ShapeDtypeStruct((t_all, MLSTM_HEADS * MLSTM_DV), F32),
            jax.ShapeDtypeStruct(c0.shape[1:], F32),
            jax.ShapeDtypeStruct(n0.shape[1:], F32),
            jax.ShapeDtypeStruct(m0.shape[1:], F32),
        ],
        scratch_shapes=[pltpu.VMEM((MLSTM_DK, MLSTM_DV), F32), pltpu.VMEM((1, MLSTM_DK), F32),
                        pltpu.VMEM((1, 1), F32)],
        input_output_aliases=aliases,
        compiler_params=_cparams("parallel", "parallel", "arbitrary"),
        name="mlstm_prompt" if chunks > 1 else "mlstm_sample",
    )(*args)


def _postmix_kernel(attn_ref, hm_ref, gt_ref, x_ref, gtp_ref, shp_ref, scp_ref, gts_ref, shs_ref, scs_ref,
                    gf_ref, wa_ref, wm_ref, wo_ref, wr_ref, br_ref, x1_ref, h2_ref, lg_ref, *, n_prompt_tiles):
    is_sample = pl.program_id(0) >= n_prompt_tiles
    gate1 = jnp.where(is_sample, gts_ref[...], gtp_ref[...])
    sh = jnp.where(is_sample, shs_ref[...], shp_ref[...])
    sc = jnp.where(is_sample, scs_ref[...], scp_ref[...])
    d = x_ref.shape[1]
    y_attn = _dot(attn_ref[...].astype(BF16), wa_ref[...])
    y_mlstm = _dot(hm_ref[...].astype(BF16), wm_ref[...])
    merged = jax.nn.sigmoid(gt_ref[:, :d]) * y_attn + jax.nn.sigmoid(gt_ref[:, d:]) * y_mlstm
    x1 = x_ref[...] + gate1 * _dot(merged.astype(BF16), wo_ref[...])
    x1_ref[...] = x1
    h2 = (_rms(x1) * gf_ref[...] * (1.0 + sc) + sh).astype(BF16)
    h2_ref[...] = h2
    lg_ref[...] = _dot(h2, wr_ref[...]) + br_ref[...]


def _postmix(attn, hm, gt, x_all, mod_p, mod_s, g_ffn, wa, wm, wo, wr, br, tp, n_batch):
    t_all, d = x_all.shape
    tm = ROW_TILE
    n_prompt_tiles = tp // tm
    tiles_per_batch = n_prompt_tiles // n_batch
    specs = [_mod_specs(c, tm, n_prompt_tiles, tiles_per_batch, n_batch) for c in (2, 3, 4)]
    row = lambda w: pl.BlockSpec((tm, w), lambda i: (i, 0))
    full = lambda a: pl.BlockSpec(a.shape, lambda i: (0,) * a.ndim)
    gf = g_ffn.reshape(1, d)
    return pl.pallas_call(
        functools.partial(_postmix_kernel, n_prompt_tiles=n_prompt_tiles),
        grid=(t_all // tm,),
        in_specs=[row(GROUP_W), row(d), row(GATE_W), row(d),
                  specs[0][0], specs[1][0], specs[2][0], specs[0][1], specs[1][1], specs[2][1],
                  full(gf), full(wa), full(wm), full(wo), full(wr), full(br)],
        out_specs=[row(d), row(d), row(LANES)],
        out_shape=[jax.ShapeDtypeStruct((t_all, d), F32), jax.ShapeDtypeStruct((t_all, d), BF16),
                   jax.ShapeDtypeStruct((t_all, LANES), F32)],
        compiler_params=_cparams("parallel"),
        name="postmix",
    )(attn, hm, gt, x_all, mod_p, mod_p, mod_p, mod_s, mod_s, mod_s, gf, wa, wm, wo, wr, br)


def _expert_kernel(be_ref, nused_ref, rows_ref, wgu_ref, bgu_ref, wd_ref, bd_ref, y_ref, wgu_s, wd_s):
    i = pl.program_id(0)
    e = be_ref[i]
    prev = be_ref[jnp.maximum(i - 1, 0)]

    @pl.when((i == 0) | (e != prev))
    def _():
        wgu_s[...] = wgu_ref[...].astype(BF16)
        wd_s[...] = wd_ref[...].astype(BF16)

    @pl.when(i < nused_ref[0])
    def _():
        de = wd_s.shape[0]
        gu = _dot(rows_ref[...], wgu_s[...]) + bgu_ref[...]
        gate = jnp.minimum(gu[:, :de], SWIGLU_LIMIT)
        up = jnp.clip(gu[:, de:], -SWIGLU_LIMIT, SWIGLU_LIMIT)
        act = (up + 1.0) * gate * jax.nn.sigmoid(SWIGLU_ALPHA * gate)
        y_ref[...] = _dot(act.astype(BF16), wd_s[...]) + bd_ref[...]

    @pl.when(i >= nused_ref[0])
    def _():
        y_ref[...] = jnp.zeros_like(y_ref)


def _experts(block_expert, n_used, rows, w_gu, b_gu, w_down, b_down, layer):
    nrows, d = rows.shape
    n_blocks = nrows // MOE_BLOCK
    de = w_down.shape[2]
    grid_spec = pltpu.PrefetchScalarGridSpec(
        num_scalar_prefetch=2,
        grid=(n_blocks,),
        in_specs=[
            pl.BlockSpec((MOE_BLOCK, d), lambda i, be, nu: (i, 0)),
            pl.BlockSpec((None, None, d, 2 * de), lambda i, be, nu: (layer, be[i], 0, 0)),
            pl.BlockSpec((None, None, 1, 2 * de), lambda i, be, nu: (layer, be[i], 0, 0)),
            pl.BlockSpec((None, None, de, d), lambda i, be, nu: (layer, be[i], 0, 0)),
            pl.BlockSpec((None, None, 1, d), lambda i, be, nu: (layer, be[i], 0, 0)),
        ],
        out_specs=pl.BlockSpec((MOE_BLOCK, d), lambda i, be, nu: (i, 0)),
        scratch_shapes=[pltpu.VMEM((d, 2 * de), BF16), pltpu.VMEM((de, d), BF16)],
    )
    return pl.pallas_call(
        _expert_kernel,
        grid_spec=grid_spec,
        out_shape=jax.ShapeDtypeStruct((nrows, d), F32),
        compiler_params=_cparams("arbitrary"),
        name="experts",
    )(block_expert, n_used, rows, w_gu, b_gu.reshape(b_gu.shape[0], b_gu.shape[1], 1, -1),
      w_down, b_down.reshape(b_down.shape[0], b_down.shape[1], 1, -1))


def _combine_kernel(yg_ref, gw_ref, x1_ref, gtp_ref, gts_ref, gfin_ref, x2_ref, y_ref, *, n_prompt_tiles):
    is_sample = pl.program_id(0) >= n_prompt_tiles
    gate2 = jnp.where(is_sample, gts_ref[...], gtp_ref[...])
    d = x1_ref.shape[1]
    gw = gw_ref[...]
    acc = gw[:, 0:1] * yg_ref[:, 0:d]
    for k in range(1, TOP_K):
        acc = acc + gw[:, k:k + 1] * yg_ref[:, k * d:(k + 1) * d]
    x2 = x1_ref[...] + gate2 * acc
    x2_ref[...] = x2
    y_ref[...] = _rms(x2) * gfin_ref[...]


def _combine(yg, gates, x1, mod_p, mod_s, g_final, tp, n_batch):
    t_all, d = x1.shape
    tm = ROW_TILE
    n_prompt_tiles = tp // tm
    tiles_per_batch = n_prompt_tiles // n_batch
    gtp, gts = _mod_specs(5, tm, n_prompt_tiles, tiles_per_batch, n_batch)
    row = lambda w: pl.BlockSpec((tm, w), lambda i: (i, 0))
    return pl.pallas_call(
        functools.partial(_combine_kernel, n_prompt_tiles=n_prompt_tiles),
        grid=(t_all // tm,),
        in_specs=[row(TOP_K * d), row(TOP_K), row(d), gtp, gts, pl.BlockSpec((1, d), lambda i: (0, 0))],
        out_specs=[row(d), row(d)],
        out_shape=[jax.ShapeDtypeStruct((t_all, d), F32), jax.ShapeDtypeStruct((t_all, d), F32)],
        compiler_params=_cparams("parallel"),
        name="moe_combine",
    )(yg, gates, x1, mod_p, mod_s, g_final.reshape(1, d))


def _route(logits):
    t = logits.shape[0]
    a = t * TOP_K
    n_blocks = a // MOE_BLOCK + N_EXPERTS
    top_val, top_idx = lax.top_k(logits, TOP_K)
    gates = jax.nn.softmax(top_val, axis=-1)
    e_flat = top_idx.reshape(a)
    order = jnp.argsort(e_flat)
    e_sorted = e_flat[order]
    counts = jnp.zeros((N_EXPERTS,), jnp.int32).at[e_flat].add(1)
    padded = (counts + MOE_BLOCK - 1) // MOE_BLOCK * MOE_BLOCK
    start = jnp.cumsum(counts) - counts
    pad_end = jnp.cumsum(padded)
    pad_start = pad_end - padded
    dest = pad_start[e_sorted] + jnp.arange(a, dtype=jnp.int32) - start[e_sorted]
    src_tok = jnp.zeros((n_blocks * MOE_BLOCK,), jnp.int32).at[dest].set((order // TOP_K).astype(jnp.int32))
    pos = jnp.zeros((a,), jnp.int32).at[order].set(dest)
    n_used = (pad_end[-1] // MOE_BLOCK).astype(jnp.int32)
    blk = jnp.arange(n_blocks, dtype=jnp.int32)
    blk = jnp.minimum(blk, n_used - 1)
    block_expert = jnp.minimum(jnp.searchsorted(pad_end, blk * MOE_BLOCK, side='right'), N_EXPERTS - 1)
    return gates, src_tok, pos, block_expert.astype(jnp.int32), n_used.reshape(1)


def _pack_w_in(w_in_l):
    d = w_in_l.shape[0]
    a = 3 * ATTN_QKV
    if_w = 2 * MLSTM_HEADS
    parts = [
        w_in_l[:, :a + ML_W],
        w_in_l[:, a + ML_W:a + ML_W + if_w],
        jnp.zeros((d, LANES - if_w), w_in_l.dtype),
        w_in_l[:, a + ML_W + if_w:],
    ]
    return jnp.concatenate(parts, axis=1).astype(BF16)


def kernel(x_prompt, x_sample, cache_kv_w128, cache_kv_w512, cache_kv_w2048, state_mlstm_C, state_mlstm_n,
           state_mlstm_m, c_prompt, c_sample, w_ada, b_ada, g_mix, g_ffn, w_in, b_gates, g_mlstm, w_br_attn,
           w_br_mlstm, w_out, w_router, b_router, w_gu, b_gu, w_down, b_down, g_final):
    n_batch, seq, d = x_prompt.shape
    n_seq, dec_seq, _ = x_sample.shape
    depth = w_ada.shape[0]
    tp, ts = n_batch * seq, n_seq * dec_seq
    t_all = tp + ts

    x_all = jnp.concatenate([x_prompt.reshape(tp, d), x_sample.reshape(ts, d)], axis=0)
    mod_all = _ada_mod(jnp.concatenate([c_prompt, c_sample], axis=0), w_ada, b_ada)
    caches_t = [jnp.transpose(c, (0, 1, 3, 4, 5, 2)).reshape(c.shape[0], c.shape[1], 2, HEADS // 2, 2 * HEAD_DIM, c.shape[2])
                for c in (cache_kv_w128, cache_kv_w512, cache_kv_w2048)]
    zc = jnp.zeros((1, n_batch, MLSTM_HEADS, MLSTM_DK, MLSTM_DV), F32)
    zn = jnp.zeros((1, n_batch, MLSTM_HEADS, 1, MLSTM_DK), F32)
    zm = jnp.zeros((1, n_batch, MLSTM_HEADS, 1, 1), F32)
    sn = state_mlstm_n.reshape(depth, n_seq, MLSTM_HEADS, 1, MLSTM_DK)
    sm = state_mlstm_m.reshape(depth, n_seq, MLSTM_HEADS, 1, 1)
    wr_pad = jnp.pad(w_router, ((0, 0), (0, 0), (0, LANES - N_EXPERTS))).astype(BF16)
    br_pad = jnp.pad(b_router, ((0, 0), (0, LANES - N_EXPERTS))).reshape(depth, 1, LANES)
    bg_pad = jnp.pad(b_gates, ((0, 0), (0, LANES - b_gates.shape[1]))).reshape(depth, 1, LANES)

    kvp = [[] for _ in range(N_GROUPS)]
    kvs = [[] for _ in range(N_GROUPS)]
    mp = [[], [], []]
    ms = [[], [], []]
    y_all = None
    for l in range(depth):
        mod_p = mod_all[l, :n_batch].reshape(n_batch, 1, 6 * d)
        mod_s = jnp.repeat(mod_all[l, n_batch:], dec_seq, axis=0)
        qkv, ml, gt = _inproj(x_all, g_mix[l], mod_p, mod_s, _pack_w_in(w_in[l]), tp, n_batch)

        attn = _attn_prompt(qkv, n_batch, seq, t_all)
        attn = _attn_sample(qkv, caches_t, l, attn, tp, n_seq, dec_seq)

        hm, c_p, n_p, m_p = _mlstm(ml, bg_pad[l], g_mlstm[l], zc, zn, zm, 0, None,
                                   row0=0, n_seq=n_batch, seq_len=seq, t_all=t_all)
        hm, c_s, n_s, m_s = _mlstm(ml, bg_pad[l], g_mlstm[l], state_mlstm_C, sn, sm, l, hm,
                                   row0=tp, n_seq=n_seq, seq_len=dec_seq, t_all=t_all)

        x1, h2, logits = _postmix(attn, hm, gt, x_all, mod_p, mod_s, g_ffn[l], w_br_attn[l].astype(BF16),
                                  w_br_mlstm[l].astype(BF16), w_out[l].astype(BF16), wr_pad[l], br_pad[l],
                                  tp, n_batch)

        gates, src_tok, pos, block_expert, n_used = _route(logits[:, :N_EXPERTS])
        rows = h2[src_tok]
        y = _experts(block_expert, n_used, rows, w_gu, b_gu, w_down, b_down, l)
        yg = y[pos].reshape(t_all, TOP_K * d)
        x_all, y_all = _combine(yg, gates, x1, mod_p, mod_s, g_final, tp, n_batch)

        for g, (win, _) in enumerate(ATTN_GROUPS):
            kg = qkv[:, ATTN_QKV + g * GROUP_W:ATTN_QKV + (g + 1) * GROUP_W]
            vg = qkv[:, 2 * ATTN_QKV + g * GROUP_W:2 * ATTN_QKV + (g + 1) * GROUP_W]
            kv = jnp.stack([kg, vg], axis=1).reshape(t_all, 2, HEADS, HEAD_DIM)
            keep = min(win, seq)
            kvp[g].append(kv[:tp].reshape(n_batch, seq, 2, HEADS, HEAD_DIM)[:, seq - keep:])
            kvs[g].append(kv[tp:].reshape(n_seq, dec_seq, 2, HEADS, HEAD_DIM))
        for lst, val in zip(mp, (c_p, n_p.reshape(n_batch, MLSTM_HEADS, MLSTM_DK), m_p.reshape(n_batch, MLSTM_HEADS))):
            lst.append(val)
        for lst, val in zip(ms, (c_s, n_s.reshape(n_seq, MLSTM_HEADS, MLSTM_DK), m_s.reshape(n_seq, MLSTM_HEADS))):
            lst.append(val)

    y_prompt = y_all[:tp].reshape(n_batch, seq, d)
    y_sample = y_all[tp:].reshape(n_seq, dec_seq, d)
    return (y_prompt, y_sample,
            jnp.stack(kvp[0]), jnp.stack(kvp[1]), jnp.stack(kvp[2]),
            jnp.stack(mp[0]), jnp.stack(mp[1]), jnp.stack(mp[2]),
            jnp.stack(kvs[0]), jnp.stack(kvs[1]), jnp.stack(kvs[2]),
            jnp.stack(ms[0]), jnp.stack(ms[1]), jnp.stack(ms[2]))
```

```python
import functools

import jax
import jax.numpy as jnp
import numpy as np
from jax import lax
from jax.experimental import pallas as pl
from jax.experimental.pallas import tpu as pltpu
from jax.experimental.pallas import tpu_sc as plsc

F32 = jnp.float32
BF16 = jnp.bfloat16

ATTN_GROUPS = ((128, 1), (512, 4), (2048, 16))
N_GROUPS = len(ATTN_GROUPS)
HEADS = 8
HEAD_DIM = 64
ATTN_BLOCK = 128
GROUP_W = HEADS * HEAD_DIM
ATTN_QKV = N_GROUPS * GROUP_W
MLSTM_HEADS = 4
MLSTM_DK = 128
MLSTM_DV = 256
N_EXPERTS = 32
TOP_K = 4
SWIGLU_LIMIT = 7.0
SWIGLU_ALPHA = 1.702
RMS_EPS = 1e-6
NEG_BIG = -1e30

LANES = 128
SUBLANES = 8
VMEM_LIMIT_BYTES = 56 * 1024 * 1024

ROW_TILE = 256
ATTN_TILE = 2048
ATTN_UNROLL = 4
MLSTM_CHUNK = 256
MOE_BLOCK = 256
SC_SCATTER_WIN = 48
SC_GATHER_WIN = 32

ML_W = 2 * MLSTM_HEADS * MLSTM_DK + 2 * MLSTM_HEADS * MLSTM_DV
ML_PAD_W = ML_W + LANES
GATE_W = 2048
W_QKV0 = 0
W_ML0 = 3 * ATTN_QKV
W_GT0 = W_ML0 + ML_PAD_W
W_TOTAL = W_GT0 + GATE_W


def _cparams(*sem):
    return pltpu.CompilerParams(dimension_semantics=sem, vmem_limit_bytes=VMEM_LIMIT_BYTES)


def _dot(a, b):
    return jnp.dot(a, b, preferred_element_type=F32)


def _dot_nt(a, b):
    return lax.dot_general(a, b, (((1,), (1,)), ((), ())), preferred_element_type=F32)


def _dot_tn(a, b):
    return lax.dot_general(a, b, (((0,), (0,)), ((), ())), preferred_element_type=F32)


def _ada_kernel(c_ref, w_ref, b_ref, o_ref):
    c = c_ref[...]
    s = c * jax.nn.sigmoid(c)
    o_ref[...] = _dot(s.astype(BF16), w_ref[...].astype(BF16)) + b_ref[...]


def _ada_mod(c_all, w_ada, b_ada):
    depth, d, n = w_ada.shape
    bc = c_all.shape[0]
    tn = 1024
    return pl.pallas_call(
        _ada_kernel,
        grid=(depth, n // tn),
        in_specs=[
            pl.BlockSpec((bc, d), lambda l, j: (0, 0)),
            pl.BlockSpec((None, d, tn), lambda l, j: (l, 0, j)),
            pl.BlockSpec((None, 1, tn), lambda l, j: (l, 0, j)),
        ],
        out_specs=pl.BlockSpec((None, bc, tn), lambda l, j: (l, 0, j)),
        out_shape=jax.ShapeDtypeStruct((depth, bc, n), F32),
        compiler_params=_cparams("parallel", "parallel"),
        name="ada_mod",
    )(c_all, w_ada, b_ada.reshape(depth, 1, n))


def _mod_specs(chunk, tm, n_prompt_tiles, tiles_per_batch, n_batch):
    d = 1024
    sp = pl.BlockSpec((None, 1, d), lambda i: (jnp.minimum(i // tiles_per_batch, n_batch - 1), 0, chunk))
    ss = pl.BlockSpec((tm, d), lambda i: (jnp.maximum(i - n_prompt_tiles, 0), chunk))
    return sp, ss


def _rms(x):
    return x * lax.rsqrt(jnp.mean(x * x, axis=-1, keepdims=True) + RMS_EPS)


def _inproj_kernel(x_ref, g_ref, shp_ref, scp_ref, shs_ref, scs_ref, w_ref, qkv_ref, ml_ref, gt_ref,
                   *, n_prompt_tiles):
    is_sample = pl.program_id(0) >= n_prompt_tiles
    sh = jnp.where(is_sample, shs_ref[...], shp_ref[...])
    sc = jnp.where(is_sample, scs_ref[...], scp_ref[...])
    h = (_rms(x_ref[...]) * g_ref[...] * (1.0 + sc) + sh).astype(BF16)
    cw = 512
    for c0 in range(0, 3 * ATTN_QKV, cw):
        qkv_ref[:, c0:c0 + cw] = _dot(h, w_ref[:, W_QKV0 + c0:W_QKV0 + c0 + cw])
    for c0 in range(0, ML_PAD_W, 640):
        ml_ref[:, c0:c0 + 640] = _dot(h, w_ref[:, W_ML0 + c0:W_ML0 + c0 + 640])
    for c0 in range(0, GATE_W, cw):
        gt_ref[:, c0:c0 + cw] = _dot(h, w_ref[:, W_GT0 + c0:W_GT0 + c0 + cw])


def _inproj(x_all, g_mix, mod_p, mod_s, w_packed, tp, n_batch):
    t_all, d = x_all.shape
    tm = ROW_TILE
    n_prompt_tiles = tp // tm
    tiles_per_batch = n_prompt_tiles // n_batch
    shp, shs = _mod_specs(0, tm, n_prompt_tiles, tiles_per_batch, n_batch)
    scp, scs = _mod_specs(1, tm, n_prompt_tiles, tiles_per_batch, n_batch)
    row = lambda w: pl.BlockSpec((tm, w), lambda i: (i, 0))
    return pl.pallas_call(
        functools.partial(_inproj_kernel, n_prompt_tiles=n_prompt_tiles),
        grid=(t_all // tm,),
        in_specs=[
            row(d),
            pl.BlockSpec((1, d), lambda i: (0, 0)),
            shp, scp, shs, scs,
            pl.BlockSpec((d, W_TOTAL), lambda i: (0, 0), pipeline_mode=pl.Buffered(1)),
        ],
        out_specs=[row(3 * ATTN_QKV), row(ML_PAD_W), row(GATE_W)],
        out_shape=[
            jax.ShapeDtypeStruct((t_all, 3 * ATTN_QKV), F32),
            jax.ShapeDtypeStruct((t_all, ML_PAD_W), F32),
            jax.ShapeDtypeStruct((t_all, GATE_W), F32),
        ],
        compiler_params=_cparams("parallel"),
        name="inproj",
    )(x_all, g_mix.reshape(1, d), mod_p, mod_p, mod_s, mod_s, w_packed)


def _attn_bias(slope, dil, valid_prev):
    qi = lax.broadcasted_iota(jnp.int32, (ATTN_BLOCK, 2 * ATTN_BLOCK), 0)
    ki = lax.broadcasted_iota(jnp.int32, (ATTN_BLOCK, 2 * ATTN_BLOCK), 1)
    dist = qi + ATTN_BLOCK - ki
    keep = (dist >= 0) & (dist <= ATTN_BLOCK) & ((ki >= ATTN_BLOCK) | valid_prev)
    return jnp.where(keep, -slope * (dil * dist).astype(F32), NEG_BIG)


def _attn_unit(q, k2, v2, bias_ref):
    lane = lax.broadcasted_iota(jnp.int32, (ATTN_BLOCK, LANES), 1)
    k2b = k2.astype(BF16)
    v2b = v2.astype(BF16)
    qs = q * (HEAD_DIM ** -0.5)
    outs = []
    for e in range(2):
        in_head = (lane >= e * HEAD_DIM) & (lane < (e + 1) * HEAD_DIM)
        qh = jnp.where(in_head, qs, 0.0).astype(BF16)
        s = _dot_nt(qh, k2b) + bias_ref[e]
        m = jnp.max(s, axis=-1, keepdims=True)
        p = jnp.exp(s - m)
        l = jnp.sum(p, axis=-1, keepdims=True)
        o = _dot(p.astype(BF16), v2b) / l
        outs.append((o, m + jnp.log(l)))
    first = lane < HEAD_DIM
    return jnp.where(first, outs[0][0], outs[1][0]), jnp.where(first, outs[0][1], outs[1][1])


def _attn_prompt_kernel(slope_ref, *refs):
    ins, o_ref, o_s, l_s, bias_s, bias_first_s = refs[:15], refs[16], refs[17], refs[18], refs[19], refs[20]
    hp = pl.program_id(1)
    tile = pl.program_id(2)
    for g, (_, dil) in enumerate(ATTN_GROUPS):
        q_ref, kc_ref, vc_ref, kp_ref, vp_ref = ins[5 * g:5 * g + 5]
        sub = ATTN_BLOCK * dil
        n_first = dil
        n_units = ATTN_TILE // ATTN_BLOCK
        for e in range(2):
            bias_s[e] = _attn_bias(slope_ref[2 * hp + e], dil, True)
            bias_first_s[e] = _attn_bias(slope_ref[2 * hp + e], dil, tile > 0)

        def strided(ref, start, size):
            return ref[pl.ds(start, size, stride=dil), :] if dil > 1 else ref[pl.ds(start, size), :]

        def store(start, o, lse):
            if dil > 1:
                o_s[g, pl.ds(start, ATTN_BLOCK, stride=dil), :] = o
                l_s[g, pl.ds(start, ATTN_BLOCK, stride=dil), :] = lse
            else:
                o_s[g, pl.ds(start, ATTN_BLOCK), :] = o
                l_s[g, pl.ds(start, ATTN_BLOCK), :] = lse

        def first_body(r, carry):
            q = strided(q_ref, r, ATTN_BLOCK)
            k2 = jnp.concatenate([strided(kp_ref, r, ATTN_BLOCK), strided(kc_ref, r, ATTN_BLOCK)], axis=0)
            v2 = jnp.concatenate([strided(vp_ref, r, ATTN_BLOCK), strided(vc_ref, r, ATTN_BLOCK)], axis=0)
            o, lse = _attn_unit(q, k2, v2, bias_first_s)
            store(r, o, lse)
            return carry

        lax.fori_loop(0, n_first, first_body, 0, unroll=min(n_first, ATTN_UNROLL))

        def rest_body(u, carry):
            j = u // dil
            r = u - j * dil
            start = j * sub + r
            q = strided(q_ref, start, ATTN_BLOCK)
            k2 = strided(kc_ref, start - sub, 2 * ATTN_BLOCK)
            v2 = strided(vc_ref, start - sub, 2 * ATTN_BLOCK)
            o, lse = _attn_unit(q, k2, v2, bias_s)
            store(start, o, lse)
            return carry

        if n_units > n_first:
            lax.fori_loop(n_first, n_units, rest_body, 0, unroll=ATTN_UNROLL)

    m = jnp.maximum(jnp.maximum(l_s[0], l_s[1]), l_s[2])
    w0 = jnp.exp(l_s[0] - m)
    w1 = jnp.exp(l_s[1] - m)
    w2 = jnp.exp(l_s[2] - m)
    o_ref[...] = (w0 * o_s[0] + w1 * o_s[1] + w2 * o_s[2]) / (w0 + w1 + w2)


def _alibi_slopes():
    return jnp.asarray(2.0 ** (-8.0 * np.arange(1, HEADS + 1) / HEADS), dtype=F32)


def _attn_prompt(qkv, attn, n_batch, seq):
    tiles = seq // ATTN_TILE
    slabs = GROUP_W // LANES
    in_specs = [pl.BlockSpec(memory_space=pltpu.SMEM)]
    args = [_alibi_slopes()]
    for g, (_, dil) in enumerate(ATTN_GROUPS):
        sub = ATTN_BLOCK * dil
        per_tile = ATTN_TILE // sub
        for which in range(3):
            col = which * (ATTN_QKV // LANES) + g * slabs
            in_specs.append(pl.BlockSpec((ATTN_TILE, LANES), lambda b, hp, t, col=col: (b * tiles + t, col + hp)))
            args.append(qkv)
        for which in (1, 2):
            col = which * (ATTN_QKV // LANES) + g * slabs
            in_specs.append(pl.BlockSpec(
                (sub, LANES),
                lambda b, hp, t, col=col, per_tile=per_tile: (jnp.maximum((b * tiles + t) * per_tile - 1, 0), col + hp)))
            args.append(qkv)
    in_specs.append(pl.BlockSpec(memory_space=pl.ANY))
    args.append(attn)
    return pl.pallas_call(
        _attn_prompt_kernel,
        grid=(n_batch, slabs, tiles),
        in_specs=in_specs,
        out_specs=pl.BlockSpec((ATTN_TILE, LANES), lambda b, hp, t: (b * tiles + t, hp)),
        out_shape=jax.ShapeDtypeStruct(attn.shape, attn.dtype),
        input_output_aliases={len(args) - 1: 0},
        scratch_shapes=[pltpu.VMEM((N_GROUPS, ATTN_TILE, LANES), F32), pltpu.VMEM((N_GROUPS, ATTN_TILE, LANES), F32),
                        pltpu.VMEM((2, ATTN_BLOCK, 2 * ATTN_BLOCK), F32), pltpu.VMEM((2, ATTN_BLOCK, 2 * ATTN_BLOCK), F32)],
        compiler_params=_cparams("parallel", "parallel", "arbitrary"),
        name="attn_prompt",
    )(*args)


def _kv_prompt_kernel(k_ref, v_ref, o_ref):
    for c, ref in enumerate((k_ref, v_ref)):
        for s in range(GROUP_W // LANES):
            o_ref[c, s] = ref[:, s * LANES:(s + 1) * LANES].T


def _kv_prompt(qkv, g, n_batch, seq):
    w = min(ATTN_GROUPS[g][0], seq)
    tm = min(512, w)
    slabs = GROUP_W // LANES
    row0 = (seq - w) // tm
    col = lambda which: which * N_GROUPS + g
    return pl.pallas_call(
        _kv_prompt_kernel,
        grid=(n_batch, w // tm),
        in_specs=[pl.BlockSpec((tm, GROUP_W), lambda b, i: (b * (seq // tm) + row0 + i, col(1))),
                  pl.BlockSpec((tm, GROUP_W), lambda b, i: (b * (seq // tm) + row0 + i, col(2)))],
        out_specs=pl.BlockSpec((None, 2, slabs, LANES, tm), lambda b, i: (b, 0, 0, 0, i)),
        out_shape=jax.ShapeDtypeStruct((n_batch, 2, slabs, LANES, w), F32),
        compiler_params=_cparams("parallel", "parallel"),
        name="kv_prompt",
    )(qkv, qkv)


def _kv_sample_kernel(x_ref, o_ref, *, dec_seq, n_seq):
    for t in range(dec_seq):
        o_ref[t] = x_ref[pl.ds(t, n_seq, stride=dec_seq), :].T


def _kv_sample(qkv, tp, n_seq, dec_seq):
    ts = n_seq * dec_seq
    slabs = GROUP_W // LANES
    assert tp % ts == 0
    return pl.pallas_call(
        functools.partial(_kv_sample_kernel, dec_seq=dec_seq, n_seq=n_seq),
        grid=(N_GROUPS, 2, slabs),
        in_specs=[pl.BlockSpec((ts, LANES), lambda g, c, s: (tp // ts, (c + 1) * N_GROUPS * slabs + g * slabs + s))],
        out_specs=pl.BlockSpec((None, dec_seq, None, None, LANES, n_seq), lambda g, c, s: (g, 0, c, s, 0, 0)),
        out_shape=jax.ShapeDtypeStruct((N_GROUPS, dec_seq, 2, slabs, LANES, n_seq), F32),
        compiler_params=_cparams("parallel", "parallel", "parallel"),
        name="kv_sample",
    )(qkv)


def _attn_sample_kernel(slope_ref, qkv_ref, c0_ref, c1_ref, c2_ref, prev_ref, o_ref, *, dec_seq):
    n = pl.program_id(0)
    half = n % 2
    caches = (c0_ref, c1_ref, c2_ref)
    rows = 2 * SUBLANES
    row = lax.broadcasted_iota(jnp.int32, (rows, 1), 0)
    t_row = (row % SUBLANES) - half * dec_seq
    row_ok = (t_row >= 0) & (t_row < dec_seq)
    lane = lax.broadcasted_iota(jnp.int32, (SUBLANES, LANES), 1)
    first = lane < HEAD_DIM
    ucol = lax.broadcasted_iota(jnp.int32, (rows, SUBLANES), 1) - half * dec_seq
    col_ok = (ucol >= 0) & (ucol < dec_seq)
    out_slabs = []
    for j in range(GROUP_W // LANES):
        slope = jnp.where(row < SUBLANES, slope_ref[2 * j], slope_ref[2 * j + 1])
        o_g, l_g = [], []
        for g, (win, dil) in enumerate(ATTN_GROUPS):
            cache = caches[g]
            wb = cache.shape[-1]
            c = g * (GROUP_W // LANES) + j
            q = qkv_ref[:, c * LANES:(c + 1) * LANES]
            kn = qkv_ref[:, ATTN_QKV + c * LANES:ATTN_QKV + (c + 1) * LANES]
            vn = qkv_ref[:, 2 * ATTN_QKV + c * LANES:2 * ATTN_QKV + (c + 1) * LANES]
            q2f = jnp.concatenate([jnp.where(first, q, 0.0), jnp.where(first, 0.0, q)], axis=0)
            q2 = q2f.astype(BF16)
            kt = cache[0, j].astype(BF16)
            vt = cache[1, j].astype(BF16)
            scale = HEAD_DIM ** -0.5
            w = lax.broadcasted_iota(jnp.int32, (rows, wb), 1)
            delta = wb + t_row - w
            ok = row_ok & (delta <= win) & ((delta & (dil - 1)) == 0)
            s_b = jnp.where(ok, _dot(q2, kt) * scale - slope * delta.astype(F32), NEG_BIG)
            dn = t_row - ucol
            okn = row_ok & col_ok & (dn >= 0) & (dn <= win) & ((dn & (dil - 1)) == 0)
            s_n = jnp.where(okn, _dot_nt(q2f, kn) * scale - slope * dn.astype(F32), NEG_BIG)
            m = jnp.maximum(jnp.max(s_b, axis=-1, keepdims=True), jnp.max(s_n, axis=-1, keepdims=True))
            p_b = jnp.exp(s_b - m)
            p_n = jnp.exp(s_n - m)
            l = jnp.sum(p_b, axis=-1, keepdims=True) + jnp.sum(p_n, axis=-1, keepdims=True)
            o = (_dot_nt(p_b.astype(BF16), vt) + _dot(p_n, vn)) / l
            lse = m + jnp.log(l)
            o_g.append(jnp.where(first, o[:SUBLANES], o[SUBLANES:]))
            l_g.append(jnp.where(first, lse[:SUBLANES], lse[SUBLANES:]))
        m = jnp.maximum(jnp.maximum(l_g[0], l_g[1]), l_g[2])
        ws = [jnp.exp(lg - m) for lg in l_g]
        out_slabs.append((ws[0] * o_g[0] + ws[1] * o_g[1] + ws[2] * o_g[2]) / (ws[0] + ws[1] + ws[2]))
    res = jnp.concatenate(out_slabs, axis=1)
    mine = (lax.broadcasted_iota(jnp.int32, (SUBLANES, 1), 0) // dec_seq) == half

    @pl.when(half == 0)
    def _():
        o_ref[...] = res

    @pl.when(half != 0)
    def _():
        o_ref[...] = jnp.where(mine, res, o_ref[...])


def _attn_sample(qkv, caches_t, layer, attn, tp, n_seq, dec_seq):
    assert 2 * dec_seq == SUBLANES
    blk0 = tp // SUBLANES
    in_specs = [
        pl.BlockSpec(memory_space=pltpu.SMEM),
        pl.BlockSpec((SUBLANES, 3 * ATTN_QKV), lambda n: (blk0 + n // 2, 0)),
    ]
    for c in caches_t:
        in_specs.append(pl.BlockSpec((None, None) + c.shape[2:], lambda n, layer=layer: (layer, n, 0, 0, 0, 0)))
    in_specs.append(pl.BlockSpec(memory_space=pl.ANY))
    return pl.pallas_call(
        functools.partial(_attn_sample_kernel, dec_seq=dec_seq),
        grid=(n_seq,),
        in_specs=in_specs,
        out_specs=pl.BlockSpec((SUBLANES, GROUP_W), lambda n: (blk0 + n // 2, 0)),
        out_shape=jax.ShapeDtypeStruct(attn.shape, attn.dtype),
        input_output_aliases={5: 0},
        compiler_params=_cparams("arbitrary"),
        name="attn_sample",
    )(_alibi_slopes(), qkv, *caches_t, attn)


def _log_sigmoid(x):
    return jnp.minimum(x, 0.0) - jnp.log1p(jnp.exp(-jnp.abs(x)))


def _mlstm_chunk(q, k, v, i_col, lf_col, c_st, n_st, m_st):
    ln = q.shape[0]
    ii = lax.broadcasted_iota(jnp.int32, (ln, ln), 0)
    jj = lax.broadcasted_iota(jnp.int32, (ln, ln), 1)
    eye = ii == jj
    causal = jj <= ii
    lf_row = jnp.sum(jnp.where(eye, lf_col, 0.0), axis=0, keepdims=True)
    i_row = jnp.sum(jnp.where(eye, i_col, 0.0), axis=0, keepdims=True)
    b_col = jnp.sum(jnp.where(causal, lf_row, 0.0), axis=1, keepdims=True)
    b_row = jnp.sum(jnp.where(ii <= jj, lf_col, 0.0), axis=0, keepdims=True)
    dm = jnp.where(causal, b_col - b_row + i_row, -jnp.inf)
    m_inter = b_col + m_st
    m_t = jnp.maximum(m_inter, jnp.max(dm, axis=1, keepdims=True))
    ks = k * (MLSTM_DK ** -0.5)
    qb = q.astype(BF16)
    ksb = ks.astype(BF16)
    vb = v.astype(BF16)
    w_intra = jnp.exp(dm - m_t) * _dot_nt(qb, ksb)
    w_inter = jnp.exp(m_inter - m_t)
    num = _dot(w_intra.astype(BF16), vb) + w_inter * _dot(qb, c_st.astype(BF16))
    den = jnp.sum(w_intra, axis=1, keepdims=True) + w_inter * jnp.sum(q * n_st, axis=1, keepdims=True)
    h = num / jnp.maximum(jnp.abs(den), jnp.exp(-m_t))
    b_last = b_col[ln - 1:ln, :]
    w_src = b_last - b_col + i_col
    m_new = jnp.maximum(b_last + m_st, jnp.max(w_src, axis=0, keepdims=True))
    decay = jnp.exp(b_last + m_st - m_new)
    kp = jnp.exp(w_src - m_new) * ks
    c_new = decay * c_st + _dot_tn(kp.astype(BF16), vb)
    n_new = decay * n_st + jnp.sum(kp, axis=0, keepdims=True)
    return h, c_new, n_new, m_new


def _gate_columns(gates, head):
    lane = lax.broadcasted_iota(jnp.int32, gates.shape, 1)
    i_col = jnp.sum(jnp.where(lane == head, gates, 0.0), axis=1, keepdims=True)
    f_col = jnp.sum(jnp.where(lane == head + MLSTM_HEADS, gates, 0.0), axis=1, keepdims=True)
    return i_col, _log_sigmoid(f_col)


def _mlstm_prompt_kernel(q_ref, k_ref, v_ref, og_ref, gt_ref, bg_ref, gn_ref, prev_ref,
                         hm_ref, c_out, n_out, m_out, c_s, n_s, m_s):
    chunk = pl.program_id(2)

    @pl.when(chunk == 0)
    def _():
        c_s[...] = jnp.zeros_like(c_s)
        n_s[...] = jnp.zeros_like(n_s)
        m_s[...] = jnp.zeros_like(m_s)

    i_col, lf_col = _gate_columns(gt_ref[...] + bg_ref[...], pl.program_id(1))
    h, c_new, n_new, m_new = _mlstm_chunk(q_ref[...], k_ref[...], v_ref[...], i_col, lf_col,
                                          c_s[...], n_s[...], m_s[...])
    c_s[...] = c_new
    n_s[...] = n_new
    m_s[...] = m_new
    hm_ref[...] = _rms(h) * gn_ref[...] * jax.nn.sigmoid(og_ref[...])

    @pl.when(chunk == pl.num_programs(2) - 1)
    def _():
        c_out[...] = c_new
        n_out[...] = n_new
        m_out[...] = m_new


def _mlstm_prompt(ml, b_gates_pad, g_mlstm, hm, n_seq, seq_len):
    ln = MLSTM_CHUNK
    chunks = seq_len // ln
    rblk = lambda s, c: s * chunks + c
    qk_blk = lambda off: pl.BlockSpec((ln, MLSTM_DK), lambda s, h, c: (rblk(s, c), off + h))
    v_blk = lambda off: pl.BlockSpec((ln, MLSTM_DV), lambda s, h, c: (rblk(s, c), off + h))
    st = lambda shp: pl.BlockSpec((None, None) + shp, lambda s, h, c: (s, h, 0, 0))
    return pl.pallas_call(
        _mlstm_prompt_kernel,
        grid=(n_seq, MLSTM_HEADS, chunks),
        in_specs=[
            qk_blk(0), qk_blk(MLSTM_HEADS), v_blk(MLSTM_HEADS), v_blk(2 * MLSTM_HEADS),
            pl.BlockSpec((ln, LANES), lambda s, h, c: (rblk(s, c), ML_W // LANES)),
            pl.BlockSpec((1, LANES), lambda s, h, c: (0, 0)),
            pl.BlockSpec((1, MLSTM_DV), lambda s, h, c: (0, h)),
            pl.BlockSpec(memory_space=pl.ANY),
        ],
        out_specs=[
            pl.BlockSpec((ln, MLSTM_DV), lambda s, h, c: (rblk(s, c), h)),
            st((MLSTM_DK, MLSTM_DV)), st((1, MLSTM_DK)), st((1, 1)),
        ],
        out_shape=[
            jax.ShapeDtypeStruct(hm.shape, hm.dtype),
            jax.ShapeDtypeStruct((n_seq, MLSTM_HEADS, MLSTM_DK, MLSTM_DV), F32),
            jax.ShapeDtypeStruct((n_seq, MLSTM_HEADS, 1, MLSTM_DK), F32),
            jax.ShapeDtypeStruct((n_seq, MLSTM_HEADS, 1, 1), F32),
        ],
        scratch_shapes=[pltpu.VMEM((MLSTM_DK, MLSTM_DV), F32), pltpu.VMEM((1, MLSTM_DK), F32),
                        pltpu.VMEM((1, 1), F32)],
        input_output_aliases={7: 0},
        compiler_params=_cparams("parallel", "parallel", "arbitrary"),
        name="mlstm_prompt",
    )(ml, ml, ml, ml, ml, b_gates_pad, g_mlstm.reshape(1, -1), hm)


def _mlstm_sample_kernel(ml_ref, bg_ref, gn_ref, c0_ref, n0_ref, m0_ref, prev_ref,
                         hm_ref, c_out, n_out, m_out, *, seq_rows):
    ln = ml_ref.shape[0]
    row = lax.broadcasted_iota(jnp.int32, (ln, 1), 0)
    gates = ml_ref[:, ML_W:ML_W + LANES] + bg_ref[...]
    k0 = MLSTM_HEADS * MLSTM_DK
    v0 = 2 * MLSTM_HEADS * MLSTM_DK
    o0 = v0 + MLSTM_HEADS * MLSTM_DV
    for head in range(MLSTM_HEADS):
        q = ml_ref[:, head * MLSTM_DK:(head + 1) * MLSTM_DK]
        k = ml_ref[:, k0 + head * MLSTM_DK:k0 + (head + 1) * MLSTM_DK]
        v = ml_ref[:, v0 + head * MLSTM_DV:v0 + (head + 1) * MLSTM_DV]
        og = ml_ref[:, o0 + head * MLSTM_DV:o0 + (head + 1) * MLSTM_DV]
        i_col, lf_col = _gate_columns(gates, head)
        h_all = None
        for j in range(ln // seq_rows):
            mine = (row // seq_rows) == j
            h, c_new, n_new, m_new = _mlstm_chunk(q, k, v, jnp.where(mine, i_col, NEG_BIG),
                                                  jnp.where(mine, lf_col, 0.0),
                                                  c0_ref[j, head], n0_ref[j, head], m0_ref[j, head])
            c_out[j, head] = c_new
            n_out[j, head] = n_new
            m_out[j, head] = m_new
            h_all = h if h_all is None else jnp.where(mine, h, h_all)
        gn = gn_ref[:, head * MLSTM_DV:(head + 1) * MLSTM_DV]
        hm_ref[:, head * MLSTM_DV:(head + 1) * MLSTM_DV] = _rms(h_all) * gn * jax.nn.sigmoid(og)


def _mlstm_sample(ml, b_gates_pad, g_mlstm, c0, n0, m0, layer, hm, row0, n_seq, seq_len):
    per_blk = SUBLANES // seq_len
    assert per_blk * seq_len == SUBLANES and n_seq % per_blk == 0 and row0 % SUBLANES == 0
    blk0 = row0 // SUBLANES
    st_in = lambda shp: pl.BlockSpec((None, per_blk, MLSTM_HEADS) + shp, lambda s: (layer, s, 0, 0, 0))
    st_out = lambda shp: pl.BlockSpec((per_blk, MLSTM_HEADS) + shp, lambda s: (s, 0, 0, 0))
    return pl.pallas_call(
        functools.partial(_mlstm_sample_kernel, seq_rows=seq_len),
        grid=(n_seq // per_blk,),
        in_specs=[
            pl.BlockSpec((SUBLANES, ML_PAD_W), lambda s: (blk0 + s, 0)),
            pl.BlockSpec((1, LANES), lambda s: (0, 0)),
            pl.BlockSpec((1, MLSTM_HEADS * MLSTM_DV), lambda s: (0, 0)),
            st_in((MLSTM_DK, MLSTM_DV)), st_in((1, MLSTM_DK)), st_in((1, 1)),
            pl.BlockSpec(memory_space=pl.ANY),
        ],
        out_specs=[
            pl.BlockSpec((SUBLANES, MLSTM_HEADS * MLSTM_DV), lambda s: (blk0 + s, 0)),
            st_out((MLSTM_DK, MLSTM_DV)), st_out((1, MLSTM_DK)), st_out((1, 1)),
        ],
        out_shape=[
            jax.ShapeDtypeStruct(hm.shape, hm.dtype),
            jax.ShapeDtypeStruct(c0.shape[1:], F32),
            jax.ShapeDtypeStruct(n0.shape[1:], F32),
            jax.ShapeDtypeStruct(m0.shape[1:], F32),
        ],
        input_output_aliases={6: 0},
        compiler_params=_cparams("parallel"),
        name="mlstm_sample",
    )(ml, b_gates_pad, g_mlstm.reshape(1, -1), c0, n0, m0, hm)


def _postmix_kernel(attn_ref, hm_ref, gt_ref, x_ref, gtp_ref, shp_ref, scp_ref, gts_ref, shs_ref, scs_ref,
                    gf_ref, wa_ref, wm_ref, wo_ref, wr_ref, br_ref, x1_ref, h2_ref, gw_ref, ei_ref, rk_ref, cnt_ref,
                    cnt_s, *, n_prompt_tiles):
    is_sample = pl.program_id(0) >= n_prompt_tiles
    gate1 = jnp.where(is_sample, gts_ref[...], gtp_ref[...])
    sh = jnp.where(is_sample, shs_ref[...], shp_ref[...])
    sc = jnp.where(is_sample, scs_ref[...], scp_ref[...])
    d = x_ref.shape[1]
    y_attn = _dot(attn_ref[...].astype(BF16), wa_ref[...])
    y_mlstm = _dot(hm_ref[...].astype(BF16), wm_ref[...])
    merged = jax.nn.sigmoid(gt_ref[:, :d]) * y_attn + jax.nn.sigmoid(gt_ref[:, d:]) * y_mlstm
    x1 = x_ref[...] + gate1 * _dot(merged.astype(BF16), wo_ref[...])
    x1_ref[...] = x1
    h2 = (_rms(x1) * gf_ref[...] * (1.0 + sc) + sh).astype(BF16)
    bits = lax.bitcast_convert_type(h2.astype(F32), jnp.int32)
    h2_ref[...] = lax.shift_right_logical(bits[:, :d // 2], 16) | bits[:, d // 2:]

    tm = x_ref.shape[0]
    lane = lax.broadcasted_iota(jnp.int32, (tm, LANES), 1)
    logits = jnp.where(lane < N_EXPERTS, _dot(h2, wr_ref[...]) + br_ref[...], -jnp.inf)

    @pl.when(pl.program_id(0) == 0)
    def _():
        cnt_s[...] = jnp.zeros_like(cnt_s)

    vals, idxs = [], []
    chosen = jnp.zeros((tm, LANES), F32)
    for _ in range(TOP_K):
        m = jnp.max(logits, axis=-1, keepdims=True)
        idx = jnp.min(jnp.where(logits == m, lane, LANES), axis=-1, keepdims=True)
        hit = lane == idx
        vals.append(m)
        idxs.append(idx)
        chosen = jnp.where(hit, 1.0, chosen)
        logits = jnp.where(hit, -jnp.inf, logits)
    ex = [jnp.exp(v - vals[0]) for v in vals]
    den = ex[0] + ex[1] + ex[2] + ex[3]
    ri = lax.broadcasted_iota(jnp.int32, (tm, tm), 0)
    ci = lax.broadcasted_iota(jnp.int32, (tm, tm), 1)
    before = jnp.where(ci < ri, 1.0, 0.0).astype(BF16)
    prefix = _dot(before, chosen.astype(BF16)) + cnt_s[...]
    for k in range(TOP_K):
        gw_ref[:, k:k + 1] = ex[k] / den
        ei_ref[:, k:k + 1] = idxs[k]
        rank = jnp.sum(jnp.where(lane == idxs[k], prefix, 0.0), axis=-1, keepdims=True)
        rk_ref[:, k:k + 1] = rank.astype(jnp.int32)
    total = cnt_s[...] + jnp.sum(chosen, axis=0, keepdims=True)
    cnt_s[...] = total
    cnt_ref[...] = total.astype(jnp.int32)


def _postmix(attn, hm, gt, x_all, mod_p, mod_s, g_ffn, wa, wm, wo, wr, br, tp, n_batch):
    t_all, d = x_all.shape
    tm = ROW_TILE
    n_prompt_tiles = tp // tm
    tiles_per_batch = n_prompt_tiles // n_batch
    specs = [_mod_specs(c, tm, n_prompt_tiles, tiles_per_batch, n_batch) for c in (2, 3, 4)]
    row = lambda w: pl.BlockSpec((tm, w), lambda i: (i, 0))
    full = lambda a: pl.BlockSpec(a.shape, lambda i: (0,) * a.ndim)
    gf = g_ffn.reshape(1, d)
    return pl.pallas_call(
        functools.partial(_postmix_kernel, n_prompt_tiles=n_prompt_tiles),
        grid=(t_all // tm,),
        in_specs=[row(GROUP_W), row(d), row(GATE_W), row(d),
                  specs[0][0], specs[1][0], specs[2][0], specs[0][1], specs[1][1], specs[2][1],
                  full(gf), full(wa), full(wm), full(wo), full(wr), full(br)],
        out_specs=[row(d), row(d // 2), row(TOP_K), row(TOP_K), row(TOP_K),
                   pl.BlockSpec((1, LANES), lambda i: (0, 0))],
        out_shape=[jax.ShapeDtypeStruct((t_all, d), F32), jax.ShapeDtypeStruct((t_all, d // 2), jnp.int32),
                   jax.ShapeDtypeStruct((t_all, TOP_K), F32), jax.ShapeDtypeStruct((t_all, TOP_K), jnp.int32),
                   jax.ShapeDtypeStruct((t_all, TOP_K), jnp.int32), jax.ShapeDtypeStruct((1, LANES), jnp.int32)],
        scratch_shapes=[pltpu.VMEM((1, LANES), F32)],
        compiler_params=_cparams("arbitrary"),
        name="postmix",
    )(attn, hm, gt, x_all, mod_p, mod_p, mod_p, mod_s, mod_s, mod_s, gf, wa, wm, wo, wr, br)


def _expert_kernel(be_ref, nused_ref, rows_ref, wgu_ref, bgu_ref, wd_ref, bd_ref, y_ref, wgu_s, wd_s):
    i = pl.program_id(0)
    e = be_ref[i]
    prev = be_ref[jnp.maximum(i - 1, 0)]

    @pl.when((i == 0) | (e != prev))
    def _():
        wgu_s[...] = wgu_ref[...].astype(BF16)
        wd_s[...] = wd_ref[...].astype(BF16)

    @pl.when(i < nused_ref[0])
    def _():
        de = wd_s.shape[0]
        words = rows_ref[...]
        lo = lax.bitcast_convert_type(lax.shift_left(words, 16), F32).astype(BF16)
        hi = lax.bitcast_convert_type(words & jnp.int32(-65536), F32).astype(BF16)
        rows = jnp.concatenate([lo, hi], axis=1)
        gu = _dot(rows, wgu_s[...]) + bgu_ref[...]
        gate = jnp.minimum(gu[:, :de], SWIGLU_LIMIT)
        up = jnp.clip(gu[:, de:], -SWIGLU_LIMIT, SWIGLU_LIMIT)
        act = (up + 1.0) * gate * jax.nn.sigmoid(SWIGLU_ALPHA * gate)
        y_ref[...] = _dot(act.astype(BF16), wd_s[...]) + bd_ref[...]

    @pl.when(i >= nused_ref[0])
    def _():
        y_ref[...] = jnp.zeros_like(y_ref)


def _experts(block_expert, n_used, rows, w_gu, b_gu, w_down, b_down, layer):
    nrows = rows.shape[0]
    n_blocks = nrows // MOE_BLOCK
    d, de = w_down.shape[3], w_down.shape[2]
    grid_spec = pltpu.PrefetchScalarGridSpec(
        num_scalar_prefetch=2,
        grid=(n_blocks,),
        in_specs=[
            pl.BlockSpec((MOE_BLOCK, d // 2), lambda i, be, nu: (i, 0)),
            pl.BlockSpec((None, None, d, 2 * de), lambda i, be, nu: (layer, be[i], 0, 0)),
            pl.BlockSpec((None, None, 1, 2 * de), lambda i, be, nu: (layer, be[i], 0, 0)),
            pl.BlockSpec((None, None, de, d), lambda i, be, nu: (layer, be[i], 0, 0)),
            pl.BlockSpec((None, None, 1, d), lambda i, be, nu: (layer, be[i], 0, 0)),
        ],
        out_specs=pl.BlockSpec((MOE_BLOCK, d), lambda i, be, nu: (i, 0)),
        scratch_shapes=[pltpu.VMEM((d, 2 * de), BF16), pltpu.VMEM((de, d), BF16)],
    )
    return pl.pallas_call(
        _expert_kernel,
        grid_spec=grid_spec,
        out_shape=jax.ShapeDtypeStruct((nrows, d), F32),
        compiler_params=_cparams("arbitrary"),
        name="experts",
    )(block_expert, n_used, rows, w_gu, b_gu.reshape(b_gu.shape[0], b_gu.shape[1], 1, -1),
      w_down, b_down.reshape(b_down.shape[0], b_down.shape[1], 1, -1))


def _sc_mesh():
    return plsc.VectorSubcoreMesh(core_axis_name="core", subcore_axis_name="subcore")


def _sc_scatter_rows(x, pos, n_rows):
    t, c = x.shape
    nk = pos.shape[0]
    nwin = t // SC_SCATTER_WIN
    assert nwin * SC_SCATTER_WIN == t
    idx = pos.reshape(nk * nwin, SC_SCATTER_WIN)

    @functools.partial(pl.kernel, out_type=jax.ShapeDtypeStruct((n_rows, c), x.dtype), mesh=_sc_mesh(),
                       scratch_types=[], name="sc_scatter_rows")
    def kern(x_hbm, i_hbm, o_hbm):
        def body(x_vmem, i_vmem):
            pltpu.sync_copy(x_vmem, o_hbm.at[i_vmem.at[0]])

        pltpu.emit_pipeline(
            body,
            grid=(nk * nwin,),
            in_specs=[pl.BlockSpec((SC_SCATTER_WIN, c), lambda i: (i % nwin, 0)),
                      pl.BlockSpec((1, SC_SCATTER_WIN), lambda i: (i, 0))],
            out_specs=[],
            core_axis_name=("core", "subcore"),
            dimension_semantics=(pltpu.PARALLEL,),
        )(x_hbm, i_hbm)

    return kern(x, idx)


def _sc_gather_rows(y, pos):
    nk, t = pos.shape
    c = y.shape[1]
    n = nk * t
    assert n % SC_GATHER_WIN == 0
    idx = pos.reshape(n // SC_GATHER_WIN, SC_GATHER_WIN)

    @functools.partial(pl.kernel, out_type=jax.ShapeDtypeStruct((n, c), y.dtype), mesh=_sc_mesh(),
                       scratch_types=[], name="sc_gather_rows")
    def kern(y_hbm, i_hbm, o_hbm):
        def body(i_vmem, o_vmem):
            pltpu.sync_copy(y_hbm.at[i_vmem.at[0]], o_vmem)

        pltpu.emit_pipeline(
            body,
            grid=(n // SC_GATHER_WIN,),
            in_specs=[pl.BlockSpec((1, SC_GATHER_WIN), lambda i: (i, 0))],
            out_specs=[pl.BlockSpec((SC_GATHER_WIN, c), lambda i: (i, 0))],
            core_axis_name=("core", "subcore"),
            dimension_semantics=(pltpu.PARALLEL,),
        )(i_hbm, o_hbm)

    return kern(y, idx).reshape(nk, t, c)


def _combine_kernel(yg_ref, gw_ref, x1_ref, gtp_ref, gts_ref, gfin_ref, *out_refs, n_prompt_tiles, final):
    is_sample = pl.program_id(0) >= n_prompt_tiles
    gate2 = jnp.where(is_sample, gts_ref[...], gtp_ref[...])
    gw = gw_ref[...]
    acc = gw[:, 0:1] * yg_ref[0]
    for k in range(1, TOP_K):
        acc = acc + gw[:, k:k + 1] * yg_ref[k]
    x2 = x1_ref[...] + gate2 * acc
    out_refs[0][...] = _rms(x2) * gfin_ref[...] if final else x2


def _combine(yg, gates, x1, mod_p, mod_s, g_final, tp, n_batch, final):
    t_all, d = x1.shape
    tm = ROW_TILE
    n_prompt_tiles = tp // tm
    tiles_per_batch = n_prompt_tiles // n_batch
    gtp, gts = _mod_specs(5, tm, n_prompt_tiles, tiles_per_batch, n_batch)
    row = lambda w: pl.BlockSpec((tm, w), lambda i: (i, 0))
    return pl.pallas_call(
        functools.partial(_combine_kernel, n_prompt_tiles=n_prompt_tiles, final=final),
        grid=(t_all // tm,),
        in_specs=[pl.BlockSpec((TOP_K, tm, d), lambda i: (0, i, 0)), row(TOP_K), row(d), gtp, gts,
                  pl.BlockSpec((1, d), lambda i: (0, 0))],
        out_specs=row(d),
        out_shape=jax.ShapeDtypeStruct((t_all, d), F32),
        compiler_params=_cparams("parallel"),
        name="moe_combine",
    )(yg, gates, x1, mod_p, mod_s, g_final.reshape(1, d))


def _positions(eidx, rank, counts):
    t = eidx.shape[0]
    n_blocks = t * TOP_K // MOE_BLOCK + N_EXPERTS
    padded = (counts + MOE_BLOCK - 1) // MOE_BLOCK * MOE_BLOCK
    pad_end = jnp.cumsum(padded)
    pad_start = pad_end - padded
    experts = jnp.arange(N_EXPERTS, dtype=jnp.int32)
    start = jnp.sum(jnp.where(eidx[:, :, None] == experts, pad_start, 0), axis=-1)
    pos = (start + rank).T.astype(jnp.int32)
    n_used = (pad_end[-1] // MOE_BLOCK).astype(jnp.int32)
    blk = jnp.minimum(jnp.arange(n_blocks, dtype=jnp.int32), n_used - 1)
    block_expert = jnp.sum((pad_end[None, :] <= (blk * MOE_BLOCK)[:, None]).astype(jnp.int32), axis=1)
    return pos, jnp.minimum(block_expert, N_EXPERTS - 1).astype(jnp.int32), n_used.reshape(1), n_blocks


def _pack_w_in(w_in_l):
    d = w_in_l.shape[0]
    a = 3 * ATTN_QKV
    if_w = 2 * MLSTM_HEADS
    parts = [
        w_in_l[:, :a + ML_W],
        w_in_l[:, a + ML_W:a + ML_W + if_w],
        jnp.zeros((d, LANES - if_w), w_in_l.dtype),
        w_in_l[:, a + ML_W + if_w:],
    ]
    return jnp.concatenate(parts, axis=1).astype(BF16)


def kernel(x_prompt, x_sample, cache_kv_w128, cache_kv_w512, cache_kv_w2048, state_mlstm_C, state_mlstm_n,
           state_mlstm_m, c_prompt, c_sample, w_ada, b_ada, g_mix, g_ffn, w_in, b_gates, g_mlstm, w_br_attn,
           w_br_mlstm, w_out, w_router, b_router, w_gu, b_gu, w_down, b_down, g_final):
    n_batch, seq, d = x_prompt.shape
    n_seq, dec_seq, _ = x_sample.shape
    depth = w_ada.shape[0]
    tp, ts = n_batch * seq, n_seq * dec_seq
    t_all = tp + ts

    x_all = jnp.concatenate([x_prompt.reshape(tp, d), x_sample.reshape(ts, d)], axis=0)
    mod_all = _ada_mod(jnp.concatenate([c_prompt, c_sample], axis=0), w_ada, b_ada)
    caches_t = [jnp.transpose(c, (0, 1, 3, 4, 5, 2)).reshape(c.shape[0], c.shape[1], 2, HEADS // 2, 2 * HEAD_DIM, c.shape[2])
                for c in (cache_kv_w128, cache_kv_w512, cache_kv_w2048)]
    sn = state_mlstm_n.reshape(depth, n_seq, MLSTM_HEADS, 1, MLSTM_DK)
    sm = state_mlstm_m.reshape(depth, n_seq, MLSTM_HEADS, 1, 1)
    wr_pad = jnp.pad(w_router, ((0, 0), (0, 0), (0, LANES - N_EXPERTS))).astype(BF16)
    br_pad = jnp.pad(b_router, ((0, 0), (0, LANES - N_EXPERTS))).reshape(depth, 1, LANES)
    bg_pad = jnp.pad(b_gates, ((0, 0), (0, LANES - b_gates.shape[1]))).reshape(depth, 1, LANES)

    kvp = [[] for _ in range(N_GROUPS)]
    kvs = [[] for _ in range(N_GROUPS)]
    mp = [[], [], []]
    ms = [[], [], []]
    attn = jnp.zeros((t_all, GROUP_W), F32)
    hm = jnp.zeros((t_all, MLSTM_HEADS * MLSTM_DV), F32)
    for l in range(depth):
        mod_p = mod_all[l, :n_batch].reshape(n_batch, 1, 6 * d)
        mod_s = jnp.repeat(mod_all[l, n_batch:], dec_seq, axis=0)
        qkv, ml, gt = _inproj(x_all, g_mix[l], mod_p, mod_s, _pack_w_in(w_in[l]), tp, n_batch)

        attn = _attn_prompt(qkv, attn, n_batch, seq)
        attn = _attn_sample(qkv, caches_t, l, attn, tp, n_seq, dec_seq)

        hm, c_p, n_p, m_p = _mlstm_prompt(ml, bg_pad[l], g_mlstm[l], hm, n_batch, seq)
        hm, c_s, n_s, m_s = _mlstm_sample(ml, bg_pad[l], g_mlstm[l], state_mlstm_C, sn, sm, l, hm, tp, n_seq, dec_seq)

        x1, h2w, gates, eidx, rank, counts = _postmix(
            attn, hm, gt, x_all, mod_p, mod_s, g_ffn[l], w_br_attn[l].astype(BF16), w_br_mlstm[l].astype(BF16),
            w_out[l].astype(BF16), wr_pad[l], br_pad[l], tp, n_batch)

        pos, block_expert, n_used, n_blocks = _positions(eidx, rank, counts[0, :N_EXPERTS])
        rows = _sc_scatter_rows(h2w, pos, n_blocks * MOE_BLOCK)
        y = _experts(block_expert, n_used, rows, w_gu, b_gu, w_down, b_down, l)
        yg = _sc_gather_rows(y, pos)
        x_all = _combine(yg, gates, x1, mod_p, mod_s, g_final, tp, n_batch, final=(l == depth - 1))

        kv_s = _kv_sample(qkv, tp, n_seq, dec_seq)
        for g in range(N_GROUPS):
            kvp[g].append(_kv_prompt(qkv, g, n_batch, seq))
            kvs[g].append(kv_s[g])
        for lst, val in zip(mp, (c_p, n_p.reshape(n_batch, MLSTM_HEADS, MLSTM_DK), m_p.reshape(n_batch, MLSTM_HEADS))):
            lst.append(val)
        for lst, val in zip(ms, (c_s, n_s.reshape(n_seq, MLSTM_HEADS, MLSTM_DK), m_s.reshape(n_seq, MLSTM_HEADS))):
            lst.append(val)

    y_prompt = x_all[:tp].reshape(n_batch, seq, d)
    y_sample = x_all[tp:].reshape(n_seq, dec_seq, d)

    def kv_prompt_out(parts):
        a = jnp.stack(parts)
        a = a.reshape(depth, n_batch, 2, HEADS, HEAD_DIM, a.shape[-1])
        return jnp.transpose(a, (0, 1, 5, 2, 3, 4))

    def kv_sample_out(parts):
        a = jnp.stack(parts).reshape(depth, dec_seq, 2, HEADS, HEAD_DIM, n_seq)
        return jnp.transpose(a, (0, 5, 1, 2, 3, 4))

    return (y_prompt, y_sample,
            kv_prompt_out(kvp[0]), kv_prompt_out(kvp[1]), kv_prompt_out(kvp[2]),
            jnp.stack(mp[0]), jnp.stack(mp[1]), jnp.stack(mp[2]),
            kv_sample_out(kvs[0]), kv_sample_out(kvs[1]), kv_sample_out(kvs[2]),
            jnp.stack(ms[0]), jnp.stack(ms[1]), jnp.stack(ms[2]))
```

```python
import functools

import jax
import jax.numpy as jnp
import numpy as np
from jax import lax
from jax.experimental import pallas as pl
from jax.experimental.pallas import tpu as pltpu
from jax.experimental.pallas import tpu_sc as plsc

F32 = jnp.float32
BF16 = jnp.bfloat16

ATTN_GROUPS = ((128, 1), (512, 4), (2048, 16))
N_GROUPS = len(ATTN_GROUPS)
HEADS = 8
HEAD_DIM = 64
ATTN_BLOCK = 128
GROUP_W = HEADS * HEAD_DIM
ATTN_QKV = N_GROUPS * GROUP_W
MLSTM_HEADS = 4
MLSTM_DK = 128
MLSTM_DV = 256
N_EXPERTS = 32
TOP_K = 4
SWIGLU_LIMIT = 7.0
SWIGLU_ALPHA = 1.702
RMS_EPS = 1e-6
NEG_BIG = -1e30

LANES = 128
SUBLANES = 8
VMEM_LIMIT_BYTES = 56 * 1024 * 1024

ROW_TILE = 256
ATTN_TILE = 2048
ATTN_UNROLL = 8
MLSTM_CHUNK = 256
MOE_BLOCK = 256
SC_SCATTER_WIN = 48
SC_GATHER_WIN = 32

ML_W = 2 * MLSTM_HEADS * MLSTM_DK + 2 * MLSTM_HEADS * MLSTM_DV
ML_PAD_W = ML_W + LANES
GATE_W = 2048


def _cparams(*sem):
    return pltpu.CompilerParams(dimension_semantics=sem, vmem_limit_bytes=VMEM_LIMIT_BYTES)


def _dot(a, b):
    return jnp.dot(a, b, preferred_element_type=F32)


def _dot_nt(a, b):
    return lax.dot_general(a, b, (((1,), (1,)), ((), ())), preferred_element_type=F32)


def _dot_tn(a, b):
    return lax.dot_general(a, b, (((0,), (0,)), ((), ())), preferred_element_type=F32)


def _ada_kernel(c_ref, w_ref, b_ref, o_ref):
    c = c_ref[...]
    s = c * jax.nn.sigmoid(c)
    o_ref[...] = _dot(s.astype(BF16), w_ref[...].astype(BF16)) + b_ref[...]


def _ada_mod(c_all, w_ada, b_ada):
    depth, d, n = w_ada.shape
    bc = c_all.shape[0]
    tn = 1024
    return pl.pallas_call(
        _ada_kernel,
        grid=(depth, n // tn),
        in_specs=[
            pl.BlockSpec((bc, d), lambda l, j: (0, 0)),
            pl.BlockSpec((None, d, tn), lambda l, j: (l, 0, j)),
            pl.BlockSpec((None, 1, tn), lambda l, j: (l, 0, j)),
        ],
        out_specs=pl.BlockSpec((None, bc, tn), lambda l, j: (l, 0, j)),
        out_shape=jax.ShapeDtypeStruct((depth, bc, n), F32),
        compiler_params=_cparams("parallel", "parallel"),
        name="ada_mod",
    )(c_all, w_ada, b_ada.reshape(depth, 1, n))


def _mod_specs(chunk, tm, n_prompt_tiles, tiles_per_batch, n_batch):
    d = 1024
    sp = pl.BlockSpec((None, 1, d), lambda i: (jnp.minimum(i // tiles_per_batch, n_batch - 1), 0, chunk))
    ss = pl.BlockSpec((tm, d), lambda i: (jnp.maximum(i - n_prompt_tiles, 0), chunk))
    return sp, ss


def _rms(x):
    return x * lax.rsqrt(jnp.mean(x * x, axis=-1, keepdims=True) + RMS_EPS)


def _inproj_kernel(x_ref, g_ref, shp_ref, scp_ref, shs_ref, scs_ref, wa_ref, wif_ref, wg_ref,
                   qkv_ref, ml_ref, gt_ref, *, n_prompt_tiles):
    is_sample = pl.program_id(0) >= n_prompt_tiles
    sh = jnp.where(is_sample, shs_ref[...], shp_ref[...])
    sc = jnp.where(is_sample, scs_ref[...], scp_ref[...])
    h = (_rms(x_ref[...]) * g_ref[...] * (1.0 + sc) + sh).astype(BF16)
    cw = 512
    a = 3 * ATTN_QKV
    for c0 in range(0, a, cw):
        qkv_ref[:, c0:c0 + cw] = _dot(h, wa_ref[:, c0:c0 + cw])
    for c0 in range(0, ML_W, cw):
        ml_ref[:, c0:c0 + cw] = _dot(h, wa_ref[:, a + c0:a + c0 + cw])
    ml_ref[:, ML_W:] = _dot(h, wif_ref[...])
    for c0 in range(0, GATE_W, cw):
        gt_ref[:, c0:c0 + cw] = _dot(h, wg_ref[:, c0:c0 + cw])


def _split_w_in(w_in_l):
    a = 3 * ATTN_QKV + ML_W
    if_w = 2 * MLSTM_HEADS
    w_if = jnp.pad(w_in_l[:, a:a + if_w], ((0, 0), (0, LANES - if_w)))
    return w_in_l[:, :a].astype(BF16), w_if.astype(BF16), w_in_l[:, a + if_w:].astype(BF16)


def _inproj(x_all, g_mix, mod_p, mod_s, w_parts, tp, n_batch):
    t_all, d = x_all.shape
    tm = ROW_TILE
    n_prompt_tiles = tp // tm
    tiles_per_batch = n_prompt_tiles // n_batch
    shp, shs = _mod_specs(0, tm, n_prompt_tiles, tiles_per_batch, n_batch)
    scp, scs = _mod_specs(1, tm, n_prompt_tiles, tiles_per_batch, n_batch)
    row = lambda w: pl.BlockSpec((tm, w), lambda i: (i, 0))
    return pl.pallas_call(
        functools.partial(_inproj_kernel, n_prompt_tiles=n_prompt_tiles),
        grid=(t_all // tm,),
        in_specs=[
            row(d),
            pl.BlockSpec((1, d), lambda i: (0, 0)),
            shp, scp, shs, scs,
        ] + [pl.BlockSpec(w.shape, lambda i: (0, 0), pipeline_mode=pl.Buffered(1)) for w in w_parts],
        out_specs=[row(3 * ATTN_QKV), row(ML_PAD_W), row(GATE_W)],
        out_shape=[
            jax.ShapeDtypeStruct((t_all, 3 * ATTN_QKV), F32),
            jax.ShapeDtypeStruct((t_all, ML_PAD_W), F32),
            jax.ShapeDtypeStruct((t_all, GATE_W), F32),
        ],
        compiler_params=_cparams("parallel"),
        name="inproj",
    )(x_all, g_mix.reshape(1, d), mod_p, mod_p, mod_s, mod_s, *w_parts)


def _attn_bias(slope, dil, valid_prev):
    qi = lax.broadcasted_iota(jnp.int32, (ATTN_BLOCK, 2 * ATTN_BLOCK), 0)
    ki = lax.broadcasted_iota(jnp.int32, (ATTN_BLOCK, 2 * ATTN_BLOCK), 1)
    dist = qi + ATTN_BLOCK - ki
    keep = (dist >= 0) & (dist <= ATTN_BLOCK) & ((ki >= ATTN_BLOCK) | valid_prev)
    return jnp.where(keep, -slope * (dil * dist).astype(F32), NEG_BIG)


def _attn_unit(q, k2, v2, bias_ref):
    lane = lax.broadcasted_iota(jnp.int32, (ATTN_BLOCK, LANES), 1)
    k2b = k2.astype(BF16)
    v2b = v2.astype(BF16)
    qs = q * (HEAD_DIM ** -0.5)
    outs = []
    for e in range(2):
        in_head = (lane >= e * HEAD_DIM) & (lane < (e + 1) * HEAD_DIM)
        qh = jnp.where(in_head, qs, 0.0).astype(BF16)
        s = _dot_nt(qh, k2b) + bias_ref[e]
        m = jnp.max(s, axis=-1, keepdims=True)
        p = jnp.exp(s - m)
        l = jnp.sum(p, axis=-1, keepdims=True)
        o = _dot(p.astype(BF16), v2b) / l
        outs.append((o, m + jnp.log(l)))
    first = lane < HEAD_DIM
    return jnp.where(first, outs[0][0], outs[1][0]), jnp.where(first, outs[0][1], outs[1][1])


def _attn_prompt_kernel(slope_ref, *refs):
    ins, o_ref, o_s, l_s, bias_s, bias_first_s = refs[:15], refs[16], refs[17], refs[18], refs[19], refs[20]
    hp = pl.program_id(1)
    tile = pl.program_id(2)
    for g, (_, dil) in enumerate(ATTN_GROUPS):
        q_ref, kc_ref, vc_ref, kp_ref, vp_ref = ins[5 * g:5 * g + 5]
        sub = ATTN_BLOCK * dil
        n_first = dil
        n_units = ATTN_TILE // ATTN_BLOCK
        for e in range(2):
            bias_s[e] = _attn_bias(slope_ref[2 * hp + e], dil, True)
            bias_first_s[e] = _attn_bias(slope_ref[2 * hp + e], dil, tile > 0)

        def strided(ref, start, size):
            return ref[pl.ds(start, size, stride=dil), :] if dil > 1 else ref[pl.ds(start, size), :]

        def store(start, o, lse):
            if dil > 1:
                o_s[g, pl.ds(start, ATTN_BLOCK, stride=dil), :] = o
                l_s[g, pl.ds(start, ATTN_BLOCK, stride=dil), :] = lse
            else:
                o_s[g, pl.ds(start, ATTN_BLOCK), :] = o
                l_s[g, pl.ds(start, ATTN_BLOCK), :] = lse

        def first_body(r, carry):
            q = strided(q_ref, r, ATTN_BLOCK)
            k2 = jnp.concatenate([strided(kp_ref, r, ATTN_BLOCK), strided(kc_ref, r, ATTN_BLOCK)], axis=0)
            v2 = jnp.concatenate([strided(vp_ref, r, ATTN_BLOCK), strided(vc_ref, r, ATTN_BLOCK)], axis=0)
            o, lse = _attn_unit(q, k2, v2, bias_first_s)
            store(r, o, lse)
            return carry

        lax.fori_loop(0, n_first, first_body, 0, unroll=min(n_first, ATTN_UNROLL))

        def rest_body(u, carry):
            j = u // dil
            r = u - j * dil
            start = j * sub + r
            q = strided(q_ref, start, ATTN_BLOCK)
            k2 = strided(kc_ref, start - sub, 2 * ATTN_BLOCK)
            v2 = strided(vc_ref, start - sub, 2 * ATTN_BLOCK)
            o, lse = _attn_unit(q, k2, v2, bias_s)
            store(start, o, lse)
            return carry

        if n_units > n_first:
            lax.fori_loop(n_first, n_units, rest_body, 0, unroll=ATTN_UNROLL)

    m = jnp.maximum(jnp.maximum(l_s[0], l_s[1]), l_s[2])
    w0 = jnp.exp(l_s[0] - m)
    w1 = jnp.exp(l_s[1] - m)
    w2 = jnp.exp(l_s[2] - m)
    o_ref[...] = (w0 * o_s[0] + w1 * o_s[1] + w2 * o_s[2]) / (w0 + w1 + w2)


def _alibi_slopes():
    return jnp.asarray(2.0 ** (-8.0 * np.arange(1, HEADS + 1) / HEADS), dtype=F32)


def _attn_prompt(qkv, attn, n_batch, seq):
    tiles = seq // ATTN_TILE
    slabs = GROUP_W // LANES
    in_specs = [pl.BlockSpec(memory_space=pltpu.SMEM)]
    args = [_alibi_slopes()]
    for g, (_, dil) in enumerate(ATTN_GROUPS):
        sub = ATTN_BLOCK * dil
        per_tile = ATTN_TILE // sub
        for which in range(3):
            col = which * (ATTN_QKV // LANES) + g * slabs
            in_specs.append(pl.BlockSpec((ATTN_TILE, LANES), lambda b, hp, t, col=col: (b * tiles + t, col + hp)))
            args.append(qkv)
        for which in (1, 2):
            col = which * (ATTN_QKV // LANES) + g * slabs
            in_specs.append(pl.BlockSpec(
                (sub, LANES),
                lambda b, hp, t, col=col, per_tile=per_tile: (jnp.maximum((b * tiles + t) * per_tile - 1, 0), col + hp)))
            args.append(qkv)
    in_specs.append(pl.BlockSpec(memory_space=pl.ANY))
    args.append(attn)
    return pl.pallas_call(
        _attn_prompt_kernel,
        grid=(n_batch, slabs, tiles),
        in_specs=in_specs,
        out_specs=pl.BlockSpec((ATTN_TILE, LANES), lambda b, hp, t: (b * tiles + t, hp)),
        out_shape=jax.ShapeDtypeStruct(attn.shape, attn.dtype),
        input_output_aliases={len(args) - 1: 0},
        scratch_shapes=[pltpu.VMEM((N_GROUPS, ATTN_TILE, LANES), F32), pltpu.VMEM((N_GROUPS, ATTN_TILE, LANES), F32),
                        pltpu.VMEM((2, ATTN_BLOCK, 2 * ATTN_BLOCK), F32), pltpu.VMEM((2, ATTN_BLOCK, 2 * ATTN_BLOCK), F32)],
        compiler_params=_cparams("parallel", "parallel", "arbitrary"),
        name="attn_prompt",
    )(*args)


def _kv_prompt_kernel(k_ref, v_ref, o_ref):
    for c, ref in enumerate((k_ref, v_ref)):
        for s in range(GROUP_W // LANES):
            o_ref[c, s] = ref[:, s * LANES:(s + 1) * LANES].T


def _kv_prompt(qkv, g, n_batch, seq):
    w = min(ATTN_GROUPS[g][0], seq)
    tm = min(512, w)
    slabs = GROUP_W // LANES
    row0 = (seq - w) // tm
    col = lambda which: which * N_GROUPS + g
    return pl.pallas_call(
        _kv_prompt_kernel,
        grid=(n_batch, w // tm),
        in_specs=[pl.BlockSpec((tm, GROUP_W), lambda b, i: (b * (seq // tm) + row0 + i, col(1))),
                  pl.BlockSpec((tm, GROUP_W), lambda b, i: (b * (seq // tm) + row0 + i, col(2)))],
        out_specs=pl.BlockSpec((None, 2, slabs, LANES, tm), lambda b, i: (b, 0, 0, 0, i)),
        out_shape=jax.ShapeDtypeStruct((n_batch, 2, slabs, LANES, w), F32),
        compiler_params=_cparams("parallel", "parallel"),
        name="kv_prompt",
    )(qkv, qkv)


def _kv_sample_kernel(x_ref, o_ref, *, dec_seq, n_seq):
    for t in range(dec_seq):
        o_ref[t] = x_ref[pl.ds(t, n_seq, stride=dec_seq), :].T


def _kv_sample(qkv, tp, n_seq, dec_seq):
    ts = n_seq * dec_seq
    slabs = GROUP_W // LANES
    assert tp % ts == 0
    return pl.pallas_call(
        functools.partial(_kv_sample_kernel, dec_seq=dec_seq, n_seq=n_seq),
        grid=(N_GROUPS, 2, slabs),
        in_specs=[pl.BlockSpec((ts, LANES), lambda g, c, s: (tp // ts, (c + 1) * N_GROUPS * slabs + g * slabs + s))],
        out_specs=pl.BlockSpec((None, dec_seq, None, None, LANES, n_seq), lambda g, c, s: (g, 0, c, s, 0, 0)),
        out_shape=jax.ShapeDtypeStruct((N_GROUPS, dec_seq, 2, slabs, LANES, n_seq), F32),
        compiler_params=_cparams("parallel", "parallel", "parallel"),
        name="kv_sample",
    )(qkv)


def _attn_sample_kernel(slope_ref, qkv_ref, c0_ref, c1_ref, c2_ref, prev_ref, o_ref, *, dec_seq):
    n = pl.program_id(0)
    half = n % 2
    caches = (c0_ref, c1_ref, c2_ref)
    rows = 2 * SUBLANES
    row = lax.broadcasted_iota(jnp.int32, (rows, 1), 0)
    t_row = (row % SUBLANES) - half * dec_seq
    row_ok = (t_row >= 0) & (t_row < dec_seq)
    lane = lax.broadcasted_iota(jnp.int32, (SUBLANES, LANES), 1)
    first = lane < HEAD_DIM
    ucol = lax.broadcasted_iota(jnp.int32, (rows, SUBLANES), 1) - half * dec_seq
    col_ok = (ucol >= 0) & (ucol < dec_seq)
    out_slabs = []
    for j in range(GROUP_W // LANES):
        slope = jnp.where(row < SUBLANES, slope_ref[2 * j], slope_ref[2 * j + 1])
        o_g, l_g = [], []
        for g, (win, dil) in enumerate(ATTN_GROUPS):
            cache = caches[g]
            wb = cache.shape[-1]
            c = g * (GROUP_W // LANES) + j
            q = qkv_ref[:, c * LANES:(c + 1) * LANES]
            kn = qkv_ref[:, ATTN_QKV + c * LANES:ATTN_QKV + (c + 1) * LANES]
            vn = qkv_ref[:, 2 * ATTN_QKV + c * LANES:2 * ATTN_QKV + (c + 1) * LANES]
            q2f = jnp.concatenate([jnp.where(first, q, 0.0), jnp.where(first, 0.0, q)], axis=0)
            q2 = q2f.astype(BF16)
            kt = cache[0, j].astype(BF16)
            vt = cache[1, j].astype(BF16)
            scale = HEAD_DIM ** -0.5
            w = lax.broadcasted_iota(jnp.int32, (rows, wb), 1)
            delta = wb + t_row - w
            ok = row_ok & (delta <= win) & ((delta & (dil - 1)) == 0)
            s_b = jnp.where(ok, _dot(q2, kt) * scale - slope * delta.astype(F32), NEG_BIG)
            dn = t_row - ucol
            okn = row_ok & col_ok & (dn >= 0) & (dn <= win) & ((dn & (dil - 1)) == 0)
            s_n = jnp.where(okn, _dot_nt(q2f, kn) * scale - slope * dn.astype(F32), NEG_BIG)
            m = jnp.maximum(jnp.max(s_b, axis=-1, keepdims=True), jnp.max(s_n, axis=-1, keepdims=True))
            p_b = jnp.exp(s_b - m)
            p_n = jnp.exp(s_n - m)
            l = jnp.sum(p_b, axis=-1, keepdims=True) + jnp.sum(p_n, axis=-1, keepdims=True)
            o = (_dot_nt(p_b.astype(BF16), vt) + _dot(p_n, vn)) / l
            lse = m + jnp.log(l)
            o_g.append(jnp.where(first, o[:SUBLANES], o[SUBLANES:]))
            l_g.append(jnp.where(first, lse[:SUBLANES], lse[SUBLANES:]))
        m = jnp.maximum(jnp.maximum(l_g[0], l_g[1]), l_g[2])
        ws = [jnp.exp(lg - m) for lg in l_g]
        out_slabs.append((ws[0] * o_g[0] + ws[1] * o_g[1] + ws[2] * o_g[2]) / (ws[0] + ws[1] + ws[2]))
    res = jnp.concatenate(out_slabs, axis=1)
    mine = (lax.broadcasted_iota(jnp.int32, (SUBLANES, 1), 0) // dec_seq) == half

    @pl.when(half == 0)
    def _():
        o_ref[...] = res

    @pl.when(half != 0)
    def _():
        o_ref[...] = jnp.where(mine, res, o_ref[...])


def _attn_sample(qkv, caches_t, layer, attn, tp, n_seq, dec_seq):
    assert 2 * dec_seq == SUBLANES
    blk0 = tp // SUBLANES
    in_specs = [
        pl.BlockSpec(memory_space=pltpu.SMEM),
        pl.BlockSpec((SUBLANES, 3 * ATTN_QKV), lambda n: (blk0 + n // 2, 0)),
    ]
    for c in caches_t:
        in_specs.append(pl.BlockSpec((None, None) + c.shape[2:], lambda n, layer=layer: (layer, n, 0, 0, 0, 0)))
    in_specs.append(pl.BlockSpec(memory_space=pl.ANY))
    return pl.pallas_call(
        functools.partial(_attn_sample_kernel, dec_seq=dec_seq),
        grid=(n_seq,),
        in_specs=in_specs,
        out_specs=pl.BlockSpec((SUBLANES, GROUP_W), lambda n: (blk0 + n // 2, 0)),
        out_shape=jax.ShapeDtypeStruct(attn.shape, attn.dtype),
        input_output_aliases={5: 0},
        compiler_params=_cparams("arbitrary"),
        name="attn_sample",
    )(_alibi_slopes(), qkv, *caches_t, attn)


def _log_sigmoid(x):
    return jnp.minimum(x, 0.0) - jnp.log1p(jnp.exp(-jnp.abs(x)))


def _mlstm_chunk(q, k, v, i_col, lf_col, c_st, n_st, m_st):
    ln = q.shape[0]
    ii = lax.broadcasted_iota(jnp.int32, (ln, ln), 0)
    jj = lax.broadcasted_iota(jnp.int32, (ln, ln), 1)
    eye = ii == jj
    causal = jj <= ii
    lf_row = jnp.sum(jnp.where(eye, lf_col, 0.0), axis=0, keepdims=True)
    i_row = jnp.sum(jnp.where(eye, i_col, 0.0), axis=0, keepdims=True)
    b_col = jnp.sum(jnp.where(causal, lf_row, 0.0), axis=1, keepdims=True)
    b_row = jnp.sum(jnp.where(ii <= jj, lf_col, 0.0), axis=0, keepdims=True)
    dm = jnp.where(causal, b_col - b_row + i_row, -jnp.inf)
    m_inter = b_col + m_st
    m_t = jnp.maximum(m_inter, jnp.max(dm, axis=1, keepdims=True))
    ks = k * (MLSTM_DK ** -0.5)
    qb = q.astype(BF16)
    ksb = ks.astype(BF16)
    vb = v.astype(BF16)
    w_intra = jnp.exp(dm - m_t) * _dot_nt(qb, ksb)
    w_inter = jnp.exp(m_inter - m_t)
    num = _dot(w_intra.astype(BF16), vb) + w_inter * _dot(qb, c_st.astype(BF16))
    den = jnp.sum(w_intra, axis=1, keepdims=True) + w_inter * jnp.sum(q * n_st, axis=1, keepdims=True)
    h = num / jnp.maximum(jnp.abs(den), jnp.exp(-m_t))
    b_last = b_col[ln - 1:ln, :]
    w_src = b_last - b_col + i_col
    m_new = jnp.maximum(b_last + m_st, jnp.max(w_src, axis=0, keepdims=True))
    decay = jnp.exp(b_last + m_st - m_new)
    kp = jnp.exp(w_src - m_new) * ks
    c_new = decay * c_st + _dot_tn(kp.astype(BF16), vb)
    n_new = decay * n_st + jnp.sum(kp, axis=0, keepdims=True)
    return h, c_new, n_new, m_new


def _gate_columns(gates, head):
    lane = lax.broadcasted_iota(jnp.int32, gates.shape, 1)
    i_col = jnp.sum(jnp.where(lane == head, gates, 0.0), axis=1, keepdims=True)
    f_col = jnp.sum(jnp.where(lane == head + MLSTM_HEADS, gates, 0.0), axis=1, keepdims=True)
    return i_col, _log_sigmoid(f_col)


def _mlstm_prompt_kernel(ml_ref, bg_ref, gn_ref, prev_ref, hm_ref, c_out, n_out, m_out, c_s, n_s, m_s):
    chunk = pl.program_id(1)

    @pl.when(chunk == 0)
    def _():
        c_s[...] = jnp.zeros_like(c_s)
        n_s[...] = jnp.zeros_like(n_s)
        m_s[...] = jnp.zeros_like(m_s)

    gates = ml_ref[:, ML_W:ML_W + LANES] + bg_ref[...]
    k0 = MLSTM_HEADS * MLSTM_DK
    v0 = 2 * MLSTM_HEADS * MLSTM_DK
    o0 = v0 + MLSTM_HEADS * MLSTM_DV
    for head in range(MLSTM_HEADS):
        i_col, lf_col = _gate_columns(gates, head)
        h, c_new, n_new, m_new = _mlstm_chunk(
            ml_ref[:, head * MLSTM_DK:(head + 1) * MLSTM_DK],
            ml_ref[:, k0 + head * MLSTM_DK:k0 + (head + 1) * MLSTM_DK],
            ml_ref[:, v0 + head * MLSTM_DV:v0 + (head + 1) * MLSTM_DV],
            i_col, lf_col, c_s[head], n_s[head], m_s[head])
        c_s[head] = c_new
        n_s[head] = n_new
        m_s[head] = m_new
        cols = slice(head * MLSTM_DV, (head + 1) * MLSTM_DV)
        og = ml_ref[:, o0 + head * MLSTM_DV:o0 + (head + 1) * MLSTM_DV]
        hm_ref[:, cols] = _rms(h) * gn_ref[:, cols] * jax.nn.sigmoid(og)

    @pl.when(chunk == pl.num_programs(1) - 1)
    def _():
        c_out[...] = c_s[...]
        n_out[...] = n_s[...]
        m_out[...] = m_s[...]


def _mlstm_prompt(ml, b_gates_pad, g_mlstm, hm, n_seq, seq_len):
    ln = MLSTM_CHUNK
    chunks = seq_len // ln
    st = lambda shp: pl.BlockSpec((None, MLSTM_HEADS) + shp, lambda s, c: (s, 0, 0, 0))
    return pl.pallas_call(
        _mlstm_prompt_kernel,
        grid=(n_seq, chunks),
        in_specs=[
            pl.BlockSpec((ln, ML_PAD_W), lambda s, c: (s * chunks + c, 0)),
            pl.BlockSpec((1, LANES), lambda s, c: (0, 0)),
            pl.BlockSpec((1, MLSTM_HEADS * MLSTM_DV), lambda s, c: (0, 0)),
            pl.BlockSpec(memory_space=pl.ANY),
        ],
        out_specs=[
            pl.BlockSpec((ln, MLSTM_HEADS * MLSTM_DV), lambda s, c: (s * chunks + c, 0)),
            st((MLSTM_DK, MLSTM_DV)), st((1, MLSTM_DK)), st((1, 1)),
        ],
        out_shape=[
            jax.ShapeDtypeStruct(hm.shape, hm.dtype),
            jax.ShapeDtypeStruct((n_seq, MLSTM_HEADS, MLSTM_DK, MLSTM_DV), F32),
            jax.ShapeDtypeStruct((n_seq, MLSTM_HEADS, 1, MLSTM_DK), F32),
            jax.ShapeDtypeStruct((n_seq, MLSTM_HEADS, 1, 1), F32),
        ],
        scratch_shapes=[pltpu.VMEM((MLSTM_HEADS, MLSTM_DK, MLSTM_DV), F32),
                        pltpu.VMEM((MLSTM_HEADS, 1, MLSTM_DK), F32),
                        pltpu.VMEM((MLSTM_HEADS, 1, 1), F32)],
        input_output_aliases={3: 0},
        compiler_params=_cparams("parallel", "arbitrary"),
        name="mlstm_prompt",
    )(ml, b_gates_pad, g_mlstm.reshape(1, -1), hm)


def _mlstm_sample_kernel(ml_ref, bg_ref, gn_ref, c0_ref, n0_ref, m0_ref, prev_ref,
                         hm_ref, c_out, n_out, m_out, *, seq_rows):
    ln = ml_ref.shape[0]
    row = lax.broadcasted_iota(jnp.int32, (ln, 1), 0)
    gates = ml_ref[:, ML_W:ML_W + LANES] + bg_ref[...]
    k0 = MLSTM_HEADS * MLSTM_DK
    v0 = 2 * MLSTM_HEADS * MLSTM_DK
    o0 = v0 + MLSTM_HEADS * MLSTM_DV
    for head in range(MLSTM_HEADS):
        q = ml_ref[:, head * MLSTM_DK:(head + 1) * MLSTM_DK]
        k = ml_ref[:, k0 + head * MLSTM_DK:k0 + (head + 1) * MLSTM_DK]
        v = ml_ref[:, v0 + head * MLSTM_DV:v0 + (head + 1) * MLSTM_DV]
        og = ml_ref[:, o0 + head * MLSTM_DV:o0 + (head + 1) * MLSTM_DV]
        i_col, lf_col = _gate_columns(gates, head)
        h_all = None
        for j in range(ln // seq_rows):
            mine = (row // seq_rows) == j
            h, c_new, n_new, m_new = _mlstm_chunk(q, k, v, jnp.where(mine, i_col, NEG_BIG),
                                                  jnp.where(mine, lf_col, 0.0),
                                                  c0_ref[j, head], n0_ref[j, head], m0_ref[j, head])
            c_out[j, head] = c_new
            n_out[j, head] = n_new
            m_out[j, head] = m_new
            h_all = h if h_all is None else jnp.where(mine, h, h_all)
        gn = gn_ref[:, head * MLSTM_DV:(head + 1) * MLSTM_DV]
        hm_ref[:, head * MLSTM_DV:(head + 1) * MLSTM_DV] = _rms(h_all) * gn * jax.nn.sigmoid(og)


def _mlstm_sample(ml, b_gates_pad, g_mlstm, c0, n0, m0, layer, hm, row0, n_seq, seq_len):
    per_blk = SUBLANES // seq_len
    assert per_blk * seq_len == SUBLANES and n_seq % per_blk == 0 and row0 % SUBLANES == 0
    blk0 = row0 // SUBLANES
    st_in = lambda shp: pl.BlockSpec((None, per_blk, MLSTM_HEADS) + shp, lambda s: (layer, s, 0, 0, 0))
    st_out = lambda shp: pl.BlockSpec((per_blk, MLSTM_HEADS) + shp, lambda s: (s, 0, 0, 0))
    return pl.pallas_call(
        functools.partial(_mlstm_sample_kernel, seq_rows=seq_len),
        grid=(n_seq // per_blk,),
        in_specs=[
            pl.BlockSpec((SUBLANES, ML_PAD_W), lambda s: (blk0 + s, 0)),
            pl.BlockSpec((1, LANES), lambda s: (0, 0)),
            pl.BlockSpec((1, MLSTM_HEADS * MLSTM_DV), lambda s: (0, 0)),
            st_in((MLSTM_DK, MLSTM_DV)), st_in((1, MLSTM_DK)), st_in((1, 1)),
            pl.BlockSpec(memory_space=pl.ANY),
        ],
        out_specs=[
            pl.BlockSpec((SUBLANES, MLSTM_HEADS * MLSTM_DV), lambda s: (blk0 + s, 0)),
            st_out((MLSTM_DK, MLSTM_DV)), st_out((1, MLSTM_DK)), st_out((1, 1)),
        ],
        out_shape=[
            jax.ShapeDtypeStruct(hm.shape, hm.dtype),
            jax.ShapeDtypeStruct(c0.shape[1:], F32),
            jax.ShapeDtypeStruct(n0.shape[1:], F32),
            jax.ShapeDtypeStruct(m0.shape[1:], F32),
        ],
        input_output_aliases={6: 0},
        compiler_params=_cparams("parallel"),
        name="mlstm_sample",
    )(ml, b_gates_pad, g_mlstm.reshape(1, -1), c0, n0, m0, hm)


def _postmix_kernel(attn_ref, hm_ref, gt_ref, x_ref, gtp_ref, shp_ref, scp_ref, gts_ref, shs_ref, scs_ref,
                    gf_ref, wa_ref, wm_ref, wo_ref, wr_ref, br_ref, x1_ref, h2_ref, gw_ref, ei_ref, rk_ref, cnt_ref,
                    cnt_s, *, n_prompt_tiles):
    is_sample = pl.program_id(0) >= n_prompt_tiles
    gate1 = jnp.where(is_sample, gts_ref[...], gtp_ref[...])
    sh = jnp.where(is_sample, shs_ref[...], shp_ref[...])
    sc = jnp.where(is_sample, scs_ref[...], scp_ref[...])
    d = x_ref.shape[1]
    y_attn = _dot(attn_ref[...].astype(BF16), wa_ref[...])
    y_mlstm = _dot(hm_ref[...].astype(BF16), wm_ref[...])
    merged = jax.nn.sigmoid(gt_ref[:, :d]) * y_attn + jax.nn.sigmoid(gt_ref[:, d:]) * y_mlstm
    x1 = x_ref[...] + gate1 * _dot(merged.astype(BF16), wo_ref[...])
    x1_ref[...] = x1
    h2 = (_rms(x1) * gf_ref[...] * (1.0 + sc) + sh).astype(BF16)
    bits = lax.bitcast_convert_type(h2.astype(F32), jnp.int32)
    h2_ref[...] = lax.shift_right_logical(bits[:, :d // 2], 16) | bits[:, d // 2:]

    tm = x_ref.shape[0]
    lane = lax.broadcasted_iota(jnp.int32, (tm, LANES), 1)
    logits = jnp.where(lane < N_EXPERTS, _dot(h2, wr_ref[...]) + br_ref[...], -jnp.inf)

    @pl.when(pl.program_id(0) == 0)
    def _():
        cnt_s[...] = jnp.zeros_like(cnt_s)

    vals, idxs = [], []
    chosen = jnp.zeros((tm, LANES), F32)
    for _ in range(TOP_K):
        m = jnp.max(logits, axis=-1, keepdims=True)
        idx = jnp.min(jnp.where(logits == m, lane, LANES), axis=-1, keepdims=True)
        hit = lane == idx
        vals.append(m)
        idxs.append(idx)
        chosen = jnp.where(hit, 1.0, chosen)
        logits = jnp.where(hit, -jnp.inf, logits)
    ex = [jnp.exp(v - vals[0]) for v in vals]
    den = ex[0] + ex[1] + ex[2] + ex[3]
    ri = lax.broadcasted_iota(jnp.int32, (tm, tm), 0)
    ci = lax.broadcasted_iota(jnp.int32, (tm, tm), 1)
    before = jnp.where(ci < ri, 1.0, 0.0).astype(BF16)
    prefix = _dot(before, chosen.astype(BF16)) + cnt_s[...]
    for k in range(TOP_K):
        gw_ref[:, k:k + 1] = ex[k] / den
        ei_ref[:, k:k + 1] = idxs[k]
        rank = jnp.sum(jnp.where(lane == idxs[k], prefix, 0.0), axis=-1, keepdims=True)
        rk_ref[:, k:k + 1] = rank.astype(jnp.int32)
    total = cnt_s[...] + jnp.sum(chosen, axis=0, keepdims=True)
    cnt_s[...] = total
    cnt_ref[...] = total.astype(jnp.int32)


def _postmix(attn, hm, gt, x_all, mod_p, mod_s, g_ffn, wa, wm, wo, wr, br, tp, n_batch):
    t_all, d = x_all.shape
    tm = ROW_TILE
    n_prompt_tiles = tp // tm
    tiles_per_batch = n_prompt_tiles // n_batch
    specs = [_mod_specs(c, tm, n_prompt_tiles, tiles_per_batch, n_batch) for c in (2, 3, 4)]
    row = lambda w: pl.BlockSpec((tm, w), lambda i: (i, 0))
    full = lambda a: pl.BlockSpec(a.shape, lambda i: (0,) * a.ndim)
    gf = g_ffn.reshape(1, d)
    return pl.pallas_call(
        functools.partial(_postmix_kernel, n_prompt_tiles=n_prompt_tiles),
        grid=(t_all // tm,),
        in_specs=[row(GROUP_W), row(d), row(GATE_W), row(d),
                  specs[0][0], specs[1][0], specs[2][0], specs[0][1], specs[1][1], specs[2][1],
                  full(gf), full(wa), full(wm), full(wo), full(wr), full(br)],
        out_specs=[row(d), row(d // 2), row(TOP_K), row(TOP_K), row(TOP_K),
                   pl.BlockSpec((1, LANES), lambda i: (0, 0))],
        out_shape=[jax.ShapeDtypeStruct((t_all, d), F32), jax.ShapeDtypeStruct((t_all, d // 2), jnp.int32),
                   jax.ShapeDtypeStruct((t_all, TOP_K), F32), jax.ShapeDtypeStruct((t_all, TOP_K), jnp.int32),
                   jax.ShapeDtypeStruct((t_all, TOP_K), jnp.int32), jax.ShapeDtypeStruct((1, LANES), jnp.int32)],
        scratch_shapes=[pltpu.VMEM((1, LANES), F32)],
        compiler_params=_cparams("arbitrary"),
        name="postmix",
    )(attn, hm, gt, x_all, mod_p, mod_p, mod_p, mod_s, mod_s, mod_s, gf, wa, wm, wo, wr, br)


def _expert_kernel(be_ref, ne_ref, nused_ref, rows_ref, wgu_hbm, bgu_ref, wd_hbm, bd_ref, y_ref,
                   wgu_f, wd_f, wgu_s, wd_s, sem, *, layer):
    i = pl.program_id(0)
    e = be_ref[i]
    prev = be_ref[jnp.maximum(i - 1, 0)]

    def weight_copies(expert):
        return (pltpu.make_async_copy(wgu_hbm.at[layer, expert], wgu_f, sem.at[0]),
                pltpu.make_async_copy(wd_hbm.at[layer, expert], wd_f, sem.at[1]))

    @pl.when(i == 0)
    def _():
        for cp in weight_copies(e):
            cp.start()

    @pl.when((i == 0) | (e != prev))
    def _():
        for cp in weight_copies(e):
            cp.wait()
        wgu_s[...] = wgu_f[...].astype(BF16)
        wd_s[...] = wd_f[...].astype(BF16)
        nxt = ne_ref[i]

        @pl.when(nxt >= 0)
        def _():
            for cp in weight_copies(nxt):
                cp.start()

    @pl.when(i < nused_ref[0])
    def _():
        de = wd_s.shape[0]
        words = rows_ref[...]
        lo = lax.bitcast_convert_type(lax.shift_left(words, 16), F32).astype(BF16)
        hi = lax.bitcast_convert_type(words & jnp.int32(-65536), F32).astype(BF16)
        rows = jnp.concatenate([lo, hi], axis=1)
        gu = _dot(rows, wgu_s[...]) + bgu_ref[...]
        gate = jnp.minimum(gu[:, :de], SWIGLU_LIMIT)
        up = jnp.clip(gu[:, de:], -SWIGLU_LIMIT, SWIGLU_LIMIT)
        act = (up + 1.0) * gate * jax.nn.sigmoid(SWIGLU_ALPHA * gate)
        y_ref[...] = _dot(act.astype(BF16), wd_s[...]) + bd_ref[...]

    @pl.when(i >= nused_ref[0])
    def _():
        y_ref[...] = jnp.zeros_like(y_ref)


def _experts(block_expert, next_expert, n_used, rows, w_gu, b_gu, w_down, b_down, layer):
    nrows = rows.shape[0]
    n_blocks = nrows // MOE_BLOCK
    d, de = w_down.shape[3], w_down.shape[2]
    grid_spec = pltpu.PrefetchScalarGridSpec(
        num_scalar_prefetch=3,
        grid=(n_blocks,),
        in_specs=[
            pl.BlockSpec((MOE_BLOCK, d // 2), lambda i, be, ne, nu: (i, 0)),
            pl.BlockSpec(memory_space=pl.ANY),
            pl.BlockSpec((None, None, 1, 2 * de), lambda i, be, ne, nu: (layer, be[i], 0, 0)),
            pl.BlockSpec(memory_space=pl.ANY),
            pl.BlockSpec((None, None, 1, d), lambda i, be, ne, nu: (layer, be[i], 0, 0)),
        ],
        out_specs=pl.BlockSpec((MOE_BLOCK, d), lambda i, be, ne, nu: (i, 0)),
        scratch_shapes=[pltpu.VMEM((d, 2 * de), F32), pltpu.VMEM((de, d), F32),
                        pltpu.VMEM((d, 2 * de), BF16), pltpu.VMEM((de, d), BF16),
                        pltpu.SemaphoreType.DMA((2,))],
    )
    return pl.pallas_call(
        functools.partial(_expert_kernel, layer=layer),
        grid_spec=grid_spec,
        out_shape=jax.ShapeDtypeStruct((nrows, d), F32),
        compiler_params=_cparams("arbitrary"),
        name="experts",
    )(block_expert, next_expert, n_used, rows, w_gu, b_gu.reshape(b_gu.shape[0], b_gu.shape[1], 1, -1),
      w_down, b_down.reshape(b_down.shape[0], b_down.shape[1], 1, -1))


def _sc_mesh():
    return plsc.VectorSubcoreMesh(core_axis_name="core", subcore_axis_name="subcore")


def _sc_scatter_rows(x, pos, n_rows):
    t, c = x.shape
    nk = pos.shape[0]
    nwin = t // SC_SCATTER_WIN
    assert nwin * SC_SCATTER_WIN == t
    idx = pos.reshape(nk * nwin, SC_SCATTER_WIN)

    @functools.partial(pl.kernel, out_type=jax.ShapeDtypeStruct((n_rows, c), x.dtype), mesh=_sc_mesh(),
                       scratch_types=[], name="sc_scatter_rows")
    def kern(x_hbm, i_hbm, o_hbm):
        def body(x_vmem, i_vmem):
            pltpu.sync_copy(x_vmem, o_hbm.at[i_vmem.at[0]])

        pltpu.emit_pipeline(
            body,
            grid=(nk * nwin,),
            in_specs=[pl.BlockSpec((SC_SCATTER_WIN, c), lambda i: (i % nwin, 0)),
                      pl.BlockSpec((1, SC_SCATTER_WIN), lambda i: (i, 0))],
            out_specs=[],
            core_axis_name=("core", "subcore"),
            dimension_semantics=(pltpu.PARALLEL,),
        )(x_hbm, i_hbm)

    return kern(x, idx)


def _sc_gather_rows(y, pos):
    nk, t = pos.shape
    c = y.shape[1]
    n = nk * t
    assert n % SC_GATHER_WIN == 0
    idx = pos.reshape(n // SC_GATHER_WIN, SC_GATHER_WIN)

    @functools.partial(pl.kernel, out_type=jax.ShapeDtypeStruct((n, c), y.dtype), mesh=_sc_mesh(),
                       scratch_types=[], name="sc_gather_rows")
    def kern(y_hbm, i_hbm, o_hbm):
        def body(i_vmem, o_vmem):
            pltpu.sync_copy(y_hbm.at[i_vmem.at[0]], o_vmem)

        pltpu.emit_pipeline(
            body,
            grid=(n // SC_GATHER_WIN,),
            in_specs=[pl.BlockSpec((1, SC_GATHER_WIN), lambda i: (i, 0))],
            out_specs=[pl.BlockSpec((SC_GATHER_WIN, c), lambda i: (i, 0))],
            core_axis_name=("core", "subcore"),
            dimension_semantics=(pltpu.PARALLEL,),
        )(i_hbm, o_hbm)

    return kern(y, idx).reshape(nk, t, c)


def _combine_kernel(yg_ref, gw_ref, x1_ref, gtp_ref, gts_ref, gfin_ref, *out_refs, n_prompt_tiles, final):
    is_sample = pl.program_id(0) >= n_prompt_tiles
    gate2 = jnp.where(is_sample, gts_ref[...], gtp_ref[...])
    gw = gw_ref[...]
    acc = gw[:, 0:1] * yg_ref[0]
    for k in range(1, TOP_K):
        acc = acc + gw[:, k:k + 1] * yg_ref[k]
    x2 = x1_ref[...] + gate2 * acc
    out_refs[0][...] = _rms(x2) * gfin_ref[...] if final else x2


def _combine(yg, gates, x1, mod_p, mod_s, g_final, tp, n_batch, final):
    t_all, d = x1.shape
    tm = ROW_TILE
    n_prompt_tiles = tp // tm
    tiles_per_batch = n_prompt_tiles // n_batch
    gtp, gts = _mod_specs(5, tm, n_prompt_tiles, tiles_per_batch, n_batch)
    row = lambda w: pl.BlockSpec((tm, w), lambda i: (i, 0))
    return pl.pallas_call(
        functools.partial(_combine_kernel, n_prompt_tiles=n_prompt_tiles, final=final),
        grid=(t_all // tm,),
        in_specs=[pl.BlockSpec((TOP_K, tm, d), lambda i: (0, i, 0)), row(TOP_K), row(d), gtp, gts,
                  pl.BlockSpec((1, d), lambda i: (0, 0))],
        out_specs=row(d),
        out_shape=jax.ShapeDtypeStruct((t_all, d), F32),
        compiler_params=_cparams("parallel"),
        name="moe_combine",
    )(yg, gates, x1, mod_p, mod_s, g_final.reshape(1, d))


def _positions(eidx, rank, counts):
    t = eidx.shape[0]
    n_blocks = t * TOP_K // MOE_BLOCK + N_EXPERTS
    padded = (counts + MOE_BLOCK - 1) // MOE_BLOCK * MOE_BLOCK
    pad_end = jnp.cumsum(padded)
    pad_start = pad_end - padded
    experts = jnp.arange(N_EXPERTS, dtype=jnp.int32)
    start = jnp.sum(jnp.where(eidx[:, :, None] == experts, pad_start, 0), axis=-1)
    pos = (start + rank).T.astype(jnp.int32)
    n_used = (pad_end[-1] // MOE_BLOCK).astype(jnp.int32)
    blk = jnp.minimum(jnp.arange(n_blocks, dtype=jnp.int32), n_used - 1)
    block_expert = jnp.sum((pad_end[None, :] <= (blk * MOE_BLOCK)[:, None]).astype(jnp.int32), axis=1)
    block_expert = jnp.minimum(block_expert, N_EXPERTS - 1).astype(jnp.int32)
    later_used = (counts[None, :] > 0) & (experts[None, :] > experts[:, None])
    nxt = jnp.min(jnp.where(later_used, experts[None, :], N_EXPERTS), axis=1)
    nxt = jnp.where(nxt < N_EXPERTS, nxt, -1)
    next_expert = jnp.sum(jnp.where(block_expert[:, None] == experts[None, :], nxt[None, :], 0), axis=1)
    return pos, block_expert, next_expert.astype(jnp.int32), n_used.reshape(1), n_blocks


def kernel(x_prompt, x_sample, cache_kv_w128, cache_kv_w512, cache_kv_w2048, state_mlstm_C, state_mlstm_n,
           state_mlstm_m, c_prompt, c_sample, w_ada, b_ada, g_mix, g_ffn, w_in, b_gates, g_mlstm, w_br_attn,
           w_br_mlstm, w_out, w_router, b_router, w_gu, b_gu, w_down, b_down, g_final):
    n_batch, seq, d = x_prompt.shape
    n_seq, dec_seq, _ = x_sample.shape
    depth = w_ada.shape[0]
    tp, ts = n_batch * seq, n_seq * dec_seq
    t_all = tp + ts

    x_all = jnp.concatenate([x_prompt.reshape(tp, d), x_sample.reshape(ts, d)], axis=0)
    mod_all = _ada_mod(jnp.concatenate([c_prompt, c_sample], axis=0), w_ada, b_ada)
    caches_t = [jnp.transpose(c, (0, 1, 3, 4, 5, 2)).reshape(c.shape[0], c.shape[1], 2, HEADS // 2, 2 * HEAD_DIM, c.shape[2])
                for c in (cache_kv_w128, cache_kv_w512, cache_kv_w2048)]
    sn = state_mlstm_n.reshape(depth, n_seq, MLSTM_HEADS, 1, MLSTM_DK)
    sm = state_mlstm_m.reshape(depth, n_seq, MLSTM_HEADS, 1, 1)
    wr_pad = jnp.pad(w_router, ((0, 0), (0, 0), (0, LANES - N_EXPERTS))).astype(BF16)
    br_pad = jnp.pad(b_router, ((0, 0), (0, LANES - N_EXPERTS))).reshape(depth, 1, LANES)
    bg_pad = jnp.pad(b_gates, ((0, 0), (0, LANES - b_gates.shape[1]))).reshape(depth, 1, LANES)

    kvp = [[] for _ in range(N_GROUPS)]
    kvs = [[] for _ in range(N_GROUPS)]
    mp = [[], [], []]
    ms = [[], [], []]
    attn = jnp.zeros((t_all, GROUP_W), F32)
    hm = jnp.zeros((t_all, MLSTM_HEADS * MLSTM_DV), F32)
    for l in range(depth):
        mod_p = mod_all[l, :n_batch].reshape(n_batch, 1, 6 * d)
        mod_s = jnp.repeat(mod_all[l, n_batch:], dec_seq, axis=0)
        qkv, ml, gt = _inproj(x_all, g_mix[l], mod_p, mod_s, _split_w_in(w_in[l]), tp, n_batch)

        attn = _attn_prompt(qkv, attn, n_batch, seq)
        attn = _attn_sample(qkv, caches_t, l, attn, tp, n_seq, dec_seq)

        hm, c_p, n_p, m_p = _mlstm_prompt(ml, bg_pad[l], g_mlstm[l], hm, n_batch, seq)
        hm, c_s, n_s, m_s = _mlstm_sample(ml, bg_pad[l], g_mlstm[l], state_mlstm_C, sn, sm, l, hm, tp, n_seq, dec_seq)

        x1, h2w, gates, eidx, rank, counts = _postmix(
            attn, hm, gt, x_all, mod_p, mod_s, g_ffn[l], w_br_attn[l].astype(BF16), w_br_mlstm[l].astype(BF16),
            w_out[l].astype(BF16), wr_pad[l], br_pad[l], tp, n_batch)

        pos, block_expert, next_expert, n_used, n_blocks = _positions(eidx, rank, counts[0, :N_EXPERTS])
        rows = _sc_scatter_rows(h2w, pos, n_blocks * MOE_BLOCK)
        y = _experts(block_expert, next_expert, n_used, rows, w_gu, b_gu, w_down, b_down, l)
        yg = _sc_gather_rows(y, pos)
        x_all = _combine(yg, gates, x1, mod_p, mod_s, g_final, tp, n_batch, final=(l == depth - 1))

        kv_s = _kv_sample(qkv, tp, n_seq, dec_seq)
        for g in range(N_GROUPS):
            kvp[g].append(_kv_prompt(qkv, g, n_batch, seq))
            kvs[g].append(kv_s[g])
        for lst, val in zip(mp, (c_p, n_p.reshape(n_batch, MLSTM_HEADS, MLSTM_DK), m_p.reshape(n_batch, MLSTM_HEADS))):
            lst.append(val)
        for lst, val in zip(ms, (c_s, n_s.reshape(n_seq, MLSTM_HEADS, MLSTM_DK), m_s.reshape(n_seq, MLSTM_HEADS))):
            lst.append(val)

    y_prompt = x_all[:tp].reshape(n_batch, seq, d)
    y_sample = x_all[tp:].reshape(n_seq, dec_seq, d)

    def kv_prompt_out(parts):
        a = jnp.stack(parts)
        a = a.reshape(depth, n_batch, 2, HEADS, HEAD_DIM, a.shape[-1])
        return jnp.transpose(a, (0, 1, 5, 2, 3, 4))

    def kv_sample_out(parts):
        a = jnp.stack(parts).reshape(depth, dec_seq, 2, HEADS, HEAD_DIM, n_seq)
        return jnp.transpose(a, (0, 5, 1, 2, 3, 4))

    return (y_prompt, y_sample,
            kv_prompt_out(kvp[0]), kv_prompt_out(kvp[1]), kv_prompt_out(kvp[2]),
            jnp.stack(mp[0]), jnp.stack(mp[1]), jnp.stack(mp[2]),
            kv_sample_out(kvs[0]), kv_sample_out(kvs[1]), kv_sample_out(kvs[2]),
            jnp.stack(ms[0]), jnp.stack(ms[1]), jnp.stack(ms[2]))
```

```python
import functools

import jax
import jax.numpy as jnp
import numpy as np
from jax import lax
from jax.experimental import pallas as pl
from jax.experimental.pallas import tpu as pltpu
from jax.experimental.pallas import tpu_sc as plsc

F32 = jnp.float32
BF16 = jnp.bfloat16

ATTN_GROUPS = ((128, 1), (512, 4), (2048, 16))
N_GROUPS = len(ATTN_GROUPS)
HEADS = 8
HEAD_DIM = 64
ATTN_BLOCK = 128
GROUP_W = HEADS * HEAD_DIM
ATTN_QKV = N_GROUPS * GROUP_W
MLSTM_HEADS = 4
MLSTM_DK = 128
MLSTM_DV = 256
N_EXPERTS = 32
TOP_K = 4
SWIGLU_LIMIT = 7.0
SWIGLU_ALPHA = 1.702
RMS_EPS = 1e-6
NEG_BIG = -1e30

LANES = 128
SUBLANES = 8
VMEM_LIMIT_BYTES = 56 * 1024 * 1024

ROW_TILE = 256
ATTN_TILE = 2048
ATTN_UNROLL = 4
MLSTM_CHUNK = 256
MOE_BLOCK = 256
SC_SCATTER_WIN = 48
SC_GATHER_WIN = 32

ML_W = 2 * MLSTM_HEADS * MLSTM_DK + 2 * MLSTM_HEADS * MLSTM_DV
ML_PAD_W = ML_W + LANES
GATE_W = 2048


def _cparams(*sem):
    return pltpu.CompilerParams(dimension_semantics=sem, vmem_limit_bytes=VMEM_LIMIT_BYTES)


def _dot(a, b):
    return jnp.dot(a, b, preferred_element_type=F32)


def _dot_nt(a, b):
    return lax.dot_general(a, b, (((1,), (1,)), ((), ())), preferred_element_type=F32)


def _dot_tn(a, b):
    return lax.dot_general(a, b, (((0,), (0,)), ((), ())), preferred_element_type=F32)


def _ada_kernel(c_ref, w_ref, b_ref, o_ref):
    c = c_ref[...]
    s = c * jax.nn.sigmoid(c)
    o_ref[...] = _dot(s.astype(BF16), w_ref[...].astype(BF16)) + b_ref[...]


def _ada_mod(c_all, w_ada, b_ada):
    depth, d, n = w_ada.shape
    bc = c_all.shape[0]
    tn = 1024
    return pl.pallas_call(
        _ada_kernel,
        grid=(depth, n // tn),
        in_specs=[
            pl.BlockSpec((bc, d), lambda l, j: (0, 0)),
            pl.BlockSpec((None, d, tn), lambda l, j: (l, 0, j)),
            pl.BlockSpec((None, 1, tn), lambda l, j: (l, 0, j)),
        ],
        out_specs=pl.BlockSpec((None, bc, tn), lambda l, j: (l, 0, j)),
        out_shape=jax.ShapeDtypeStruct((depth, bc, n), F32),
        compiler_params=_cparams("parallel", "parallel"),
        name="ada_mod",
    )(c_all, w_ada, b_ada.reshape(depth, 1, n))


def _mod_specs(chunk, tm, n_prompt_tiles, tiles_per_batch, n_batch):
    d = 1024
    sp = pl.BlockSpec((None, 1, d), lambda i: (jnp.minimum(i // tiles_per_batch, n_batch - 1), 0, chunk))
    ss = pl.BlockSpec((tm, d), lambda i: (jnp.maximum(i - n_prompt_tiles, 0), chunk))
    return sp, ss


def _sigmoid(x):
    return 0.5 * jnp.tanh(0.5 * x) + 0.5


def _rms(x):
    return x * lax.rsqrt(jnp.mean(x * x, axis=-1, keepdims=True) + RMS_EPS)


def _inproj_kernel(x_ref, g_ref, shp_ref, scp_ref, shs_ref, scs_ref, wa_ref, wif_ref, wg_ref,
                   qkv_ref, ml_ref, gt_ref, *, n_prompt_tiles):
    is_sample = pl.program_id(0) >= n_prompt_tiles
    sh = jnp.where(is_sample, shs_ref[...], shp_ref[...])
    sc = jnp.where(is_sample, scs_ref[...], scp_ref[...])
    h = (_rms(x_ref[...]) * g_ref[...] * (1.0 + sc) + sh).astype(BF16)
    cw = 512
    a = 3 * ATTN_QKV
    for c0 in range(0, a, cw):
        qkv_ref[:, c0:c0 + cw] = _dot(h, wa_ref[:, c0:c0 + cw])
    for c0 in range(0, ML_W, cw):
        ml_ref[:, c0:c0 + cw] = _dot(h, wa_ref[:, a + c0:a + c0 + cw])
    ml_ref[:, ML_W:] = _dot(h, wif_ref[...])
    for c0 in range(0, GATE_W, cw):
        gt_ref[:, c0:c0 + cw] = _dot(h, wg_ref[:, c0:c0 + cw])


def _split_w_in(w_in_l):
    a = 3 * ATTN_QKV + ML_W
    if_w = 2 * MLSTM_HEADS
    w_if = jnp.pad(w_in_l[:, a:a + if_w], ((0, 0), (0, LANES - if_w)))
    return w_in_l[:, :a].astype(BF16), w_if.astype(BF16), w_in_l[:, a + if_w:].astype(BF16)


def _inproj(x_all, g_mix, mod_p, mod_s, w_parts, tp, n_batch):
    t_all, d = x_all.shape
    tm = ROW_TILE
    n_prompt_tiles = tp // tm
    tiles_per_batch = n_prompt_tiles // n_batch
    shp, shs = _mod_specs(0, tm, n_prompt_tiles, tiles_per_batch, n_batch)
    scp, scs = _mod_specs(1, tm, n_prompt_tiles, tiles_per_batch, n_batch)
    row = lambda w: pl.BlockSpec((tm, w), lambda i: (i, 0))
    return pl.pallas_call(
        functools.partial(_inproj_kernel, n_prompt_tiles=n_prompt_tiles),
        grid=(t_all // tm,),
        in_specs=[
            row(d),
            pl.BlockSpec((1, d), lambda i: (0, 0)),
            shp, scp, shs, scs,
        ] + [pl.BlockSpec(w.shape, lambda i: (0, 0), pipeline_mode=pl.Buffered(1)) for w in w_parts],
        out_specs=[row(3 * ATTN_QKV), row(ML_PAD_W), row(GATE_W)],
        out_shape=[
            jax.ShapeDtypeStruct((t_all, 3 * ATTN_QKV), F32),
            jax.ShapeDtypeStruct((t_all, ML_PAD_W), F32),
            jax.ShapeDtypeStruct((t_all, GATE_W), F32),
        ],
        compiler_params=_cparams("parallel"),
        name="inproj",
    )(x_all, g_mix.reshape(1, d), mod_p, mod_p, mod_s, mod_s, *w_parts)


def _attn_bias(slope, dil, valid_prev):
    qi = lax.broadcasted_iota(jnp.int32, (ATTN_BLOCK, 2 * ATTN_BLOCK), 0)
    ki = lax.broadcasted_iota(jnp.int32, (ATTN_BLOCK, 2 * ATTN_BLOCK), 1)
    dist = qi + ATTN_BLOCK - ki
    keep = (dist >= 0) & (dist <= ATTN_BLOCK) & ((ki >= ATTN_BLOCK) | valid_prev)
    return jnp.where(keep, -slope * (dil * dist).astype(F32), NEG_BIG)


def _attn_units(blocks):
    lane = lax.broadcasted_iota(jnp.int32, (ATTN_BLOCK, LANES), 1)
    first = lane < HEAD_DIM
    scores, values = [], []
    for q, k2, v2, bias_ref in blocks:
        k2b = k2.astype(BF16)
        qs = q * (HEAD_DIM ** -0.5)
        for e in range(2):
            qh = jnp.where(first if e == 0 else ~first, qs, 0.0).astype(BF16)
            scores.append(_dot_nt(qh, k2b) + bias_ref[e])
        values.append(v2.astype(BF16))
    probs = []
    for s in scores:
        m = jnp.max(s, axis=-1, keepdims=True)
        p = jnp.exp(s - m)
        l = jnp.sum(p, axis=-1, keepdims=True)
        probs.append((p.astype(BF16), l, m + jnp.log(l)))
    outs = []
    for u, v2b in enumerate(values):
        (p0, l0, lse0), (p1, l1, lse1) = probs[2 * u], probs[2 * u + 1]
        o0 = _dot(p0, v2b) / l0
        o1 = _dot(p1, v2b) / l1
        outs.append((jnp.where(first, o0, o1), jnp.where(first, lse0, lse1)))
    return outs


def _attn_prompt_kernel(slope_ref, *refs):
    ins, o_ref, o_s, l_s, bias_s, bias_first_s = refs[:15], refs[16], refs[17], refs[18], refs[19], refs[20]
    hp = pl.program_id(1)
    tile = pl.program_id(2)
    for g, (_, dil) in enumerate(ATTN_GROUPS):
        q_ref, kc_ref, vc_ref, kp_ref, vp_ref = ins[5 * g:5 * g + 5]
        sub = ATTN_BLOCK * dil
        n_first = dil
        n_units = ATTN_TILE // ATTN_BLOCK
        for e in range(2):
            bias_s[e] = _attn_bias(slope_ref[2 * hp + e], dil, True)
            bias_first_s[e] = _attn_bias(slope_ref[2 * hp + e], dil, tile > 0)

        def strided(ref, start, size):
            return ref[pl.ds(start, size, stride=dil), :] if dil > 1 else ref[pl.ds(start, size), :]

        def store(start, o, lse):
            if dil > 1:
                o_s[g, pl.ds(start, ATTN_BLOCK, stride=dil), :] = o
                l_s[g, pl.ds(start, ATTN_BLOCK, stride=dil), :] = lse
            else:
                o_s[g, pl.ds(start, ATTN_BLOCK), :] = o
                l_s[g, pl.ds(start, ATTN_BLOCK), :] = lse

        def first_block(r):
            k2 = jnp.concatenate([strided(kp_ref, r, ATTN_BLOCK), strided(kc_ref, r, ATTN_BLOCK)], axis=0)
            v2 = jnp.concatenate([strided(vp_ref, r, ATTN_BLOCK), strided(vc_ref, r, ATTN_BLOCK)], axis=0)
            return r, (strided(q_ref, r, ATTN_BLOCK), k2, v2, bias_first_s)

        def rest_block(u):
            j = u // dil
            start = j * sub + (u - j * dil)
            return start, (strided(q_ref, start, ATTN_BLOCK), strided(kc_ref, start - sub, 2 * ATTN_BLOCK),
                           strided(vc_ref, start - sub, 2 * ATTN_BLOCK), bias_s)

        def run_group(base, firsts):
            placed = [first_block(base + i) if f else rest_block(base + i) for i, f in enumerate(firsts)]
            for (start, _), (o, lse) in zip(placed, _attn_units([b for _, b in placed])):
                store(start, o, lse)

        patterns = [tuple(gi * ATTN_UNROLL + i < n_first for i in range(ATTN_UNROLL))
                    for gi in range(n_units // ATTN_UNROLL)]
        gi = 0
        while gi < len(patterns):
            end = gi
            while end < len(patterns) and patterns[end] == patterns[gi]:
                end += 1
            if end - gi == 1:
                run_group(gi * ATTN_UNROLL, patterns[gi])
            else:
                def body(it, carry, pattern=patterns[gi]):
                    run_group(it * ATTN_UNROLL, pattern)
                    return carry
                lax.fori_loop(gi, end, body, 0)
            gi = end

    m = jnp.maximum(jnp.maximum(l_s[0], l_s[1]), l_s[2])
    w0 = jnp.exp(l_s[0] - m)
    w1 = jnp.exp(l_s[1] - m)
    w2 = jnp.exp(l_s[2] - m)
    o_ref[...] = (w0 * o_s[0] + w1 * o_s[1] + w2 * o_s[2]) / (w0 + w1 + w2)


def _alibi_slopes():
    return jnp.asarray(2.0 ** (-8.0 * np.arange(1, HEADS + 1) / HEADS), dtype=F32)


def _attn_prompt(qkv, attn, n_batch, seq):
    tiles = seq // ATTN_TILE
    slabs = GROUP_W // LANES
    in_specs = [pl.BlockSpec(memory_space=pltpu.SMEM)]
    args = [_alibi_slopes()]
    for g, (_, dil) in enumerate(ATTN_GROUPS):
        sub = ATTN_BLOCK * dil
        per_tile = ATTN_TILE // sub
        for which in range(3):
            col = which * (ATTN_QKV // LANES) + g * slabs
            in_specs.append(pl.BlockSpec((ATTN_TILE, LANES), lambda b, hp, t, col=col: (b * tiles + t, col + hp)))
            args.append(qkv)
        for which in (1, 2):
            col = which * (ATTN_QKV // LANES) + g * slabs
            in_specs.append(pl.BlockSpec(
                (sub, LANES),
                lambda b, hp, t, col=col, per_tile=per_tile: (jnp.maximum((b * tiles + t) * per_tile - 1, 0), col + hp)))
            args.append(qkv)
    in_specs.append(pl.BlockSpec(memory_space=pl.ANY))
    args.append(attn)
    return pl.pallas_call(
        _attn_prompt_kernel,
        grid=(n_batch, slabs, tiles),
        in_specs=in_specs,
        out_specs=pl.BlockSpec((ATTN_TILE, LANES), lambda b, hp, t: (b * tiles + t, hp)),
        out_shape=jax.ShapeDtypeStruct(attn.shape, attn.dtype),
        input_output_aliases={len(args) - 1: 0},
        scratch_shapes=[pltpu.VMEM((N_GROUPS, ATTN_TILE, LANES), F32), pltpu.VMEM((N_GROUPS, ATTN_TILE, LANES), F32),
                        pltpu.VMEM((2, ATTN_BLOCK, 2 * ATTN_BLOCK), F32), pltpu.VMEM((2, ATTN_BLOCK, 2 * ATTN_BLOCK), F32)],
        compiler_params=_cparams("parallel", "parallel", "arbitrary"),
        name="attn_prompt",
    )(*args)


def _kv_prompt_kernel(k_ref, v_ref, o_ref):
    for c, ref in enumerate((k_ref, v_ref)):
        for s in range(GROUP_W // LANES):
            o_ref[c, s] = ref[:, s * LANES:(s + 1) * LANES].T


def _kv_prompt(qkv, g, n_batch, seq):
    w = min(ATTN_GROUPS[g][0], seq)
    tm = min(512, w)
    slabs = GROUP_W // LANES
    row0 = (seq - w) // tm
    col = lambda which: which * N_GROUPS + g
    return pl.pallas_call(
        _kv_prompt_kernel,
        grid=(n_batch, w // tm),
        in_specs=[pl.BlockSpec((tm, GROUP_W), lambda b, i: (b * (seq // tm) + row0 + i, col(1))),
                  pl.BlockSpec((tm, GROUP_W), lambda b, i: (b * (seq // tm) + row0 + i, col(2)))],
        out_specs=pl.BlockSpec((None, 2, slabs, LANES, tm), lambda b, i: (b, 0, 0, 0, i)),
        out_shape=jax.ShapeDtypeStruct((n_batch, 2, slabs, LANES, w), F32),
        compiler_params=_cparams("parallel", "parallel"),
        name="kv_prompt",
    )(qkv, qkv)


def _kv_sample_kernel(x_ref, o_ref, *, dec_seq, n_seq):
    for t in range(dec_seq):
        o_ref[t] = x_ref[pl.ds(t, n_seq, stride=dec_seq), :].T


def _kv_sample(qkv, tp, n_seq, dec_seq):
    ts = n_seq * dec_seq
    slabs = GROUP_W // LANES
    assert tp % ts == 0
    return pl.pallas_call(
        functools.partial(_kv_sample_kernel, dec_seq=dec_seq, n_seq=n_seq),
        grid=(N_GROUPS, 2, slabs),
        in_specs=[pl.BlockSpec((ts, LANES), lambda g, c, s: (tp // ts, (c + 1) * N_GROUPS * slabs + g * slabs + s))],
        out_specs=pl.BlockSpec((None, dec_seq, None, None, LANES, n_seq), lambda g, c, s: (g, 0, c, s, 0, 0)),
        out_shape=jax.ShapeDtypeStruct((N_GROUPS, dec_seq, 2, slabs, LANES, n_seq), F32),
        compiler_params=_cparams("parallel", "parallel", "parallel"),
        name="kv_sample",
    )(qkv)


def _attn_sample_kernel(slope_ref, qkv_ref, c0_ref, c1_ref, c2_ref, prev_ref, o_ref, *, dec_seq):
    n = pl.program_id(0)
    half = n % 2
    caches = (c0_ref, c1_ref, c2_ref)
    rows = 2 * SUBLANES
    row = lax.broadcasted_iota(jnp.int32, (rows, 1), 0)
    t_row = (row % SUBLANES) - half * dec_seq
    row_ok = (t_row >= 0) & (t_row < dec_seq)
    lane = lax.broadcasted_iota(jnp.int32, (SUBLANES, LANES), 1)
    first = lane < HEAD_DIM
    ucol = lax.broadcasted_iota(jnp.int32, (rows, SUBLANES), 1) - half * dec_seq
    col_ok = (ucol >= 0) & (ucol < dec_seq)
    slabs = GROUP_W // LANES
    units = [(j, g) for j in range(slabs) for g in range(N_GROUPS)]
    scale = HEAD_DIM ** -0.5
    scores = []
    for j, g in units:
        win, dil = ATTN_GROUPS[g]
        wb = caches[g].shape[-1]
        c = g * slabs + j
        slope = jnp.where(row < SUBLANES, slope_ref[2 * j], slope_ref[2 * j + 1])
        q = qkv_ref[:, c * LANES:(c + 1) * LANES]
        kn = qkv_ref[:, ATTN_QKV + c * LANES:ATTN_QKV + (c + 1) * LANES]
        q2f = jnp.concatenate([jnp.where(first, q, 0.0), jnp.where(first, 0.0, q)], axis=0)
        kt = caches[g][0, j].astype(BF16)
        w = lax.broadcasted_iota(jnp.int32, (rows, wb), 1)
        delta = wb + t_row - w
        ok = row_ok & (delta <= win) & ((delta & (dil - 1)) == 0)
        s_b = jnp.where(ok, _dot(q2f.astype(BF16), kt) * scale - slope * delta.astype(F32), NEG_BIG)
        dn = t_row - ucol
        okn = row_ok & col_ok & (dn >= 0) & (dn <= win) & ((dn & (dil - 1)) == 0)
        s_n = jnp.where(okn, _dot_nt(q2f, kn) * scale - slope * dn.astype(F32), NEG_BIG)
        scores.append((s_b, s_n))
    probs = []
    for s_b, s_n in scores:
        m = jnp.maximum(jnp.max(s_b, axis=-1, keepdims=True), jnp.max(s_n, axis=-1, keepdims=True))
        p_b = jnp.exp(s_b - m)
        p_n = jnp.exp(s_n - m)
        l = jnp.sum(p_b, axis=-1, keepdims=True) + jnp.sum(p_n, axis=-1, keepdims=True)
        probs.append((p_b.astype(BF16), p_n, l, m + jnp.log(l)))
    o_u, l_u = {}, {}
    for (j, g), (p_b, p_n, l, lse) in zip(units, probs):
        c = g * slabs + j
        vn = qkv_ref[:, 2 * ATTN_QKV + c * LANES:2 * ATTN_QKV + (c + 1) * LANES]
        vt = caches[g][1, j].astype(BF16)
        o = (_dot_nt(p_b, vt) + _dot(p_n, vn)) / l
        o_u[j, g] = jnp.where(first, o[:SUBLANES], o[SUBLANES:])
        l_u[j, g] = jnp.where(first, lse[:SUBLANES], lse[SUBLANES:])
    out_slabs = []
    for j in range(slabs):
        m = jnp.maximum(jnp.maximum(l_u[j, 0], l_u[j, 1]), l_u[j, 2])
        ws = [jnp.exp(l_u[j, g] - m) for g in range(N_GROUPS)]
        out_slabs.append((ws[0] * o_u[j, 0] + ws[1] * o_u[j, 1] + ws[2] * o_u[j, 2]) / (ws[0] + ws[1] + ws[2]))
    res = jnp.concatenate(out_slabs, axis=1)
    mine = (lax.broadcasted_iota(jnp.int32, (SUBLANES, 1), 0) // dec_seq) == half

    @pl.when(half == 0)
    def _():
        o_ref[...] = res

    @pl.when(half != 0)
    def _():
        o_ref[...] = jnp.where(mine, res, o_ref[...])


def _attn_sample(qkv, caches_t, layer, attn, tp, n_seq, dec_seq):
    assert 2 * dec_seq == SUBLANES
    blk0 = tp // SUBLANES
    in_specs = [
        pl.BlockSpec(memory_space=pltpu.SMEM),
        pl.BlockSpec((SUBLANES, 3 * ATTN_QKV), lambda n: (blk0 + n // 2, 0)),
    ]
    for c in caches_t:
        in_specs.append(pl.BlockSpec((None, None) + c.shape[2:], lambda n, layer=layer: (layer, n, 0, 0, 0, 0)))
    in_specs.append(pl.BlockSpec(memory_space=pl.ANY))
    return pl.pallas_call(
        functools.partial(_attn_sample_kernel, dec_seq=dec_seq),
        grid=(n_seq,),
        in_specs=in_specs,
        out_specs=pl.BlockSpec((SUBLANES, GROUP_W), lambda n: (blk0 + n // 2, 0)),
        out_shape=jax.ShapeDtypeStruct(attn.shape, attn.dtype),
        input_output_aliases={5: 0},
        compiler_params=_cparams("arbitrary"),
        name="attn_sample",
    )(_alibi_slopes(), qkv, *caches_t, attn)


def _log_sigmoid(x):
    return jnp.minimum(x, 0.0) - jnp.log1p(jnp.exp(-jnp.abs(x)))


def _mlstm_chunks(items):
    ln = items[0][0].shape[0]
    ii = lax.broadcasted_iota(jnp.int32, (ln, ln), 0)
    jj = lax.broadcasted_iota(jnp.int32, (ln, ln), 1)
    eye = ii == jj
    causal = jj <= ii
    stage1 = []
    for q, k, v, i_col, lf_col, c_st, n_st, m_st in items:
        lf_row = jnp.sum(jnp.where(eye, lf_col, 0.0), axis=0, keepdims=True)
        i_row = jnp.sum(jnp.where(eye, i_col, 0.0), axis=0, keepdims=True)
        b_col = jnp.sum(jnp.where(causal, lf_row, 0.0), axis=1, keepdims=True)
        b_row = jnp.sum(jnp.where(ii <= jj, lf_col, 0.0), axis=0, keepdims=True)
        dm = jnp.where(causal, b_col - b_row + i_row, -jnp.inf)
        m_inter = b_col + m_st
        m_t = jnp.maximum(m_inter, jnp.max(dm, axis=1, keepdims=True))
        ks = k * (MLSTM_DK ** -0.5)
        qb = q.astype(BF16)
        vb = v.astype(BF16)
        qk = _dot_nt(qb, ks.astype(BF16))
        qc = _dot(qb, c_st.astype(BF16))
        stage1.append((dm, m_inter, m_t, b_col, ks, vb, qk, qc))
    stage2 = []
    for (dm, m_inter, m_t, _, _, vb, qk, _) in stage1:
        w_intra = jnp.exp(dm - m_t) * qk
        stage2.append((w_intra, jnp.exp(m_inter - m_t), _dot(w_intra.astype(BF16), vb)))
    stage3 = []
    for item, s1, (w_intra, w_inter, wv) in zip(items, stage1, stage2):
        q, _, _, i_col, _, c_st, n_st, m_st = item
        _, _, m_t, b_col, ks, vb, _, qc = s1
        num = wv + w_inter * qc
        den = jnp.sum(w_intra, axis=1, keepdims=True) + w_inter * jnp.sum(q * n_st, axis=1, keepdims=True)
        h = num / jnp.maximum(jnp.abs(den), jnp.exp(-m_t))
        b_last = b_col[ln - 1:ln, :]
        w_src = b_last - b_col + i_col
        m_new = jnp.maximum(b_last + m_st, jnp.max(w_src, axis=0, keepdims=True))
        decay = jnp.exp(b_last + m_st - m_new)
        kp = jnp.exp(w_src - m_new) * ks
        stage3.append((h, decay, kp, m_new, _dot_tn(kp.astype(BF16), vb)))
    outs = []
    for item, (h, decay, kp, m_new, kv) in zip(items, stage3):
        c_st, n_st = item[5], item[6]
        outs.append((h, decay * c_st + kv, decay * n_st + jnp.sum(kp, axis=0, keepdims=True), m_new))
    return outs


def _gate_columns(gates, head):
    lane = lax.broadcasted_iota(jnp.int32, gates.shape, 1)
    i_col = jnp.sum(jnp.where(lane == head, gates, 0.0), axis=1, keepdims=True)
    f_col = jnp.sum(jnp.where(lane == head + MLSTM_HEADS, gates, 0.0), axis=1, keepdims=True)
    return i_col, _log_sigmoid(f_col)


def _mlstm_prompt_kernel(ml_ref, bg_ref, gn_ref, prev_ref, hm_ref, c_out, n_out, m_out, c_s, n_s, m_s):
    chunk = pl.program_id(1)

    @pl.when(chunk == 0)
    def _():
        c_s[...] = jnp.zeros_like(c_s)
        n_s[...] = jnp.zeros_like(n_s)
        m_s[...] = jnp.zeros_like(m_s)

    gates = ml_ref[:, ML_W:ML_W + LANES] + bg_ref[...]
    k0 = MLSTM_HEADS * MLSTM_DK
    v0 = 2 * MLSTM_HEADS * MLSTM_DK
    o0 = v0 + MLSTM_HEADS * MLSTM_DV
    items = []
    for head in range(MLSTM_HEADS):
        i_col, lf_col = _gate_columns(gates, head)
        items.append((ml_ref[:, head * MLSTM_DK:(head + 1) * MLSTM_DK],
                      ml_ref[:, k0 + head * MLSTM_DK:k0 + (head + 1) * MLSTM_DK],
                      ml_ref[:, v0 + head * MLSTM_DV:v0 + (head + 1) * MLSTM_DV],
                      i_col, lf_col, c_s[head], n_s[head], m_s[head]))
    for head, (h, c_new, n_new, m_new) in enumerate(_mlstm_chunks(items)):
        c_s[head] = c_new
        n_s[head] = n_new
        m_s[head] = m_new
        cols = slice(head * MLSTM_DV, (head + 1) * MLSTM_DV)
        og = ml_ref[:, o0 + head * MLSTM_DV:o0 + (head + 1) * MLSTM_DV]
        hm_ref[:, cols] = _rms(h) * gn_ref[:, cols] * _sigmoid(og)

    @pl.when(chunk == pl.num_programs(1) - 1)
    def _():
        c_out[...] = c_s[...]
        n_out[...] = n_s[...]
        m_out[...] = m_s[...]


def _mlstm_prompt(ml, b_gates_pad, g_mlstm, hm, n_seq, seq_len):
    ln = MLSTM_CHUNK
    chunks = seq_len // ln
    st = lambda shp: pl.BlockSpec((None, MLSTM_HEADS) + shp, lambda s, c: (s, 0, 0, 0))
    return pl.pallas_call(
        _mlstm_prompt_kernel,
        grid=(n_seq, chunks),
        in_specs=[
            pl.BlockSpec((ln, ML_PAD_W), lambda s, c: (s * chunks + c, 0)),
            pl.BlockSpec((1, LANES), lambda s, c: (0, 0)),
            pl.BlockSpec((1, MLSTM_HEADS * MLSTM_DV), lambda s, c: (0, 0)),
            pl.BlockSpec(memory_space=pl.ANY),
        ],
        out_specs=[
            pl.BlockSpec((ln, MLSTM_HEADS * MLSTM_DV), lambda s, c: (s * chunks + c, 0)),
            st((MLSTM_DK, MLSTM_DV)), st((1, MLSTM_DK)), st((1, 1)),
        ],
        out_shape=[
            jax.ShapeDtypeStruct(hm.shape, hm.dtype),
            jax.ShapeDtypeStruct((n_seq, MLSTM_HEADS, MLSTM_DK, MLSTM_DV), F32),
            jax.ShapeDtypeStruct((n_seq, MLSTM_HEADS, 1, MLSTM_DK), F32),
            jax.ShapeDtypeStruct((n_seq, MLSTM_HEADS, 1, 1), F32),
        ],
        scratch_shapes=[pltpu.VMEM((MLSTM_HEADS, MLSTM_DK, MLSTM_DV), F32),
                        pltpu.VMEM((MLSTM_HEADS, 1, MLSTM_DK), F32),
                        pltpu.VMEM((MLSTM_HEADS, 1, 1), F32)],
        input_output_aliases={3: 0},
        compiler_params=_cparams("parallel", "arbitrary"),
        name="mlstm_prompt",
    )(ml, b_gates_pad, g_mlstm.reshape(1, -1), hm)


def _mlstm_sample_kernel(ml_ref, bg_ref, gn_ref, c0_ref, n0_ref, m0_ref, prev_ref,
                         hm_ref, c_out, n_out, m_out, *, seq_rows):
    ln = ml_ref.shape[0]
    row = lax.broadcasted_iota(jnp.int32, (ln, 1), 0)
    gates = ml_ref[:, ML_W:ML_W + LANES] + bg_ref[...]
    k0 = MLSTM_HEADS * MLSTM_DK
    v0 = 2 * MLSTM_HEADS * MLSTM_DK
    o0 = v0 + MLSTM_HEADS * MLSTM_DV
    per_blk = ln // seq_rows
    items = []
    for head in range(MLSTM_HEADS):
        q = ml_ref[:, head * MLSTM_DK:(head + 1) * MLSTM_DK]
        k = ml_ref[:, k0 + head * MLSTM_DK:k0 + (head + 1) * MLSTM_DK]
        v = ml_ref[:, v0 + head * MLSTM_DV:v0 + (head + 1) * MLSTM_DV]
        i_col, lf_col = _gate_columns(gates, head)
        for j in range(per_blk):
            mine = (row // seq_rows) == j
            items.append((q, k, v, jnp.where(mine, i_col, NEG_BIG), jnp.where(mine, lf_col, 0.0),
                          c0_ref[j, head], n0_ref[j, head], m0_ref[j, head]))
    outs = _mlstm_chunks(items)
    for head in range(MLSTM_HEADS):
        h_all = None
        for j in range(per_blk):
            h, c_new, n_new, m_new = outs[head * per_blk + j]
            c_out[j, head] = c_new
            n_out[j, head] = n_new
            m_out[j, head] = m_new
            h_all = h if h_all is None else jnp.where((row // seq_rows) == j, h, h_all)
        cols = slice(head * MLSTM_DV, (head + 1) * MLSTM_DV)
        og = ml_ref[:, o0 + head * MLSTM_DV:o0 + (head + 1) * MLSTM_DV]
        hm_ref[:, cols] = _rms(h_all) * gn_ref[:, cols] * _sigmoid(og)


def _mlstm_sample(ml, b_gates_pad, g_mlstm, c0, n0, m0, layer, hm, row0, n_seq, seq_len):
    per_blk = SUBLANES // seq_len
    assert per_blk * seq_len == SUBLANES and n_seq % per_blk == 0 and row0 % SUBLANES == 0
    blk0 = row0 // SUBLANES
    st_in = lambda shp: pl.BlockSpec((None, per_blk, MLSTM_HEADS) + shp, lambda s: (layer, s, 0, 0, 0))
    st_out = lambda shp: pl.BlockSpec((per_blk, MLSTM_HEADS) + shp, lambda s: (s, 0, 0, 0))
    return pl.pallas_call(
        functools.partial(_mlstm_sample_kernel, seq_rows=seq_len),
        grid=(n_seq // per_blk,),
        in_specs=[
            pl.BlockSpec((SUBLANES, ML_PAD_W), lambda s: (blk0 + s, 0)),
            pl.BlockSpec((1, LANES), lambda s: (0, 0)),
            pl.BlockSpec((1, MLSTM_HEADS * MLSTM_DV), lambda s: (0, 0)),
            st_in((MLSTM_DK, MLSTM_DV)), st_in((1, MLSTM_DK)), st_in((1, 1)),
            pl.BlockSpec(memory_space=pl.ANY),
        ],
        out_specs=[
            pl.BlockSpec((SUBLANES, MLSTM_HEADS * MLSTM_DV), lambda s: (blk0 + s, 0)),
            st_out((MLSTM_DK, MLSTM_DV)), st_out((1, MLSTM_DK)), st_out((1, 1)),
        ],
        out_shape=[
            jax.ShapeDtypeStruct(hm.shape, hm.dtype),
            jax.ShapeDtypeStruct(c0.shape[1:], F32),
            jax.ShapeDtypeStruct(n0.shape[1:], F32),
            jax.ShapeDtypeStruct(m0.shape[1:], F32),
        ],
        input_output_aliases={6: 0},
        compiler_params=_cparams("parallel"),
        name="mlstm_sample",
    )(ml, b_gates_pad, g_mlstm.reshape(1, -1), c0, n0, m0, hm)


def _postmix_kernel(attn_ref, hm_ref, gt_ref, x_ref, gtp_ref, shp_ref, scp_ref, gts_ref, shs_ref, scs_ref,
                    gf_ref, wa_ref, wm_ref, wo_ref, wr_ref, br_ref, x1_ref, h2_ref, gw_ref, ei_ref, rk_ref, cnt_ref,
                    cnt_s, *, n_prompt_tiles):
    is_sample = pl.program_id(0) >= n_prompt_tiles
    gate1 = jnp.where(is_sample, gts_ref[...], gtp_ref[...])
    sh = jnp.where(is_sample, shs_ref[...], shp_ref[...])
    sc = jnp.where(is_sample, scs_ref[...], scp_ref[...])
    d = x_ref.shape[1]
    y_attn = _dot(attn_ref[...].astype(BF16), wa_ref[...])
    y_mlstm = _dot(hm_ref[...].astype(BF16), wm_ref[...])
    merged = _sigmoid(gt_ref[:, :d]) * y_attn + _sigmoid(gt_ref[:, d:]) * y_mlstm
    x1 = x_ref[...] + gate1 * _dot(merged.astype(BF16), wo_ref[...])
    x1_ref[...] = x1
    h2 = (_rms(x1) * gf_ref[...] * (1.0 + sc) + sh).astype(BF16)
    bits = lax.bitcast_convert_type(h2.astype(F32), jnp.int32)
    h2_ref[...] = lax.shift_right_logical(bits[:, :d // 2], 16) | bits[:, d // 2:]

    tm = x_ref.shape[0]
    lane = lax.broadcasted_iota(jnp.int32, (tm, LANES), 1)
    logits = jnp.where(lane < N_EXPERTS, _dot(h2, wr_ref[...]) + br_ref[...], -jnp.inf)

    @pl.when(pl.program_id(0) == 0)
    def _():
        cnt_s[...] = jnp.zeros_like(cnt_s)

    vals, idxs = [], []
    chosen = jnp.zeros((tm, LANES), F32)
    for _ in range(TOP_K):
        m = jnp.max(logits, axis=-1, keepdims=True)
        idx = jnp.min(jnp.where(logits == m, lane, LANES), axis=-1, keepdims=True)
        hit = lane == idx
        vals.append(m)
        idxs.append(idx)
        chosen = jnp.where(hit, 1.0, chosen)
        logits = jnp.where(hit, -jnp.inf, logits)
    ex = [jnp.exp(v - vals[0]) for v in vals]
    den = ex[0] + ex[1] + ex[2] + ex[3]
    ri = lax.broadcasted_iota(jnp.int32, (tm, tm), 0)
    ci = lax.broadcasted_iota(jnp.int32, (tm, tm), 1)
    before = jnp.where(ci < ri, 1.0, 0.0).astype(BF16)
    prefix = _dot(before, chosen.astype(BF16)) + cnt_s[...]
    for k in range(TOP_K):
        gw_ref[:, k:k + 1] = ex[k] / den
        ei_ref[:, k:k + 1] = idxs[k]
        rank = jnp.sum(jnp.where(lane == idxs[k], prefix, 0.0), axis=-1, keepdims=True)
        rk_ref[:, k:k + 1] = rank.astype(jnp.int32)
    total = cnt_s[...] + jnp.sum(chosen, axis=0, keepdims=True)
    cnt_s[...] = total
    cnt_ref[...] = total.astype(jnp.int32)


def _postmix(attn, hm, gt, x_all, mod_p, mod_s, g_ffn, wa, wm, wo, wr, br, tp, n_batch):
    t_all, d = x_all.shape
    tm = ROW_TILE
    n_prompt_tiles = tp // tm
    tiles_per_batch = n_prompt_tiles // n_batch
    specs = [_mod_specs(c, tm, n_prompt_tiles, tiles_per_batch, n_batch) for c in (2, 3, 4)]
    row = lambda w: pl.BlockSpec((tm, w), lambda i: (i, 0))
    full = lambda a: pl.BlockSpec(a.shape, lambda i: (0,) * a.ndim)
    gf = g_ffn.reshape(1, d)
    return pl.pallas_call(
        functools.partial(_postmix_kernel, n_prompt_tiles=n_prompt_tiles),
        grid=(t_all // tm,),
        in_specs=[row(GROUP_W), row(d), row(GATE_W), row(d),
                  specs[0][0], specs[1][0], specs[2][0], specs[0][1], specs[1][1], specs[2][1],
                  full(gf), full(wa), full(wm), full(wo), full(wr), full(br)],
        out_specs=[row(d), row(d // 2), row(TOP_K), row(TOP_K), row(TOP_K),
                   pl.BlockSpec((1, LANES), lambda i: (0, 0))],
        out_shape=[jax.ShapeDtypeStruct((t_all, d), F32), jax.ShapeDtypeStruct((t_all, d // 2), jnp.int32),
                   jax.ShapeDtypeStruct((t_all, TOP_K), F32), jax.ShapeDtypeStruct((t_all, TOP_K), jnp.int32),
                   jax.ShapeDtypeStruct((t_all, TOP_K), jnp.int32), jax.ShapeDtypeStruct((1, LANES), jnp.int32)],
        scratch_shapes=[pltpu.VMEM((1, LANES), F32)],
        compiler_params=_cparams("arbitrary"),
        name="postmix",
    )(attn, hm, gt, x_all, mod_p, mod_p, mod_p, mod_s, mod_s, mod_s, gf, wa, wm, wo, wr, br)


def _expert_kernel(be_ref, ne_ref, nused_ref, rows_ref, wgu_hbm, bgu_ref, wd_hbm, bd_ref, y_ref,
                   wgu_f, wd_f, wgu_s, wd_s, sem, *, layer):
    i = pl.program_id(0)
    e = be_ref[i]
    prev = be_ref[jnp.maximum(i - 1, 0)]

    def weight_copies(expert):
        return (pltpu.make_async_copy(wgu_hbm.at[layer, expert], wgu_f, sem.at[0]),
                pltpu.make_async_copy(wd_hbm.at[layer, expert], wd_f, sem.at[1]))

    @pl.when(i == 0)
    def _():
        for cp in weight_copies(e):
            cp.start()

    @pl.when((i == 0) | (e != prev))
    def _():
        for cp in weight_copies(e):
            cp.wait()
        wgu_s[...] = wgu_f[...].astype(BF16)
        wd_s[...] = wd_f[...].astype(BF16)
        nxt = ne_ref[i]

        @pl.when(nxt >= 0)
        def _():
            for cp in weight_copies(nxt):
                cp.start()

    @pl.when(i < nused_ref[0])
    def _():
        de = wd_s.shape[0]
        words = rows_ref[...]
        lo = lax.bitcast_convert_type(lax.shift_left(words, 16), F32).astype(BF16)
        hi = lax.bitcast_convert_type(words & jnp.int32(-65536), F32).astype(BF16)
        rows = jnp.concatenate([lo, hi], axis=1)
        gu = _dot(rows, wgu_s[...]) + bgu_ref[...]
        gate = jnp.minimum(gu[:, :de], SWIGLU_LIMIT)
        up = jnp.clip(gu[:, de:], -SWIGLU_LIMIT, SWIGLU_LIMIT)
        act = (up + 1.0) * gate * _sigmoid(SWIGLU_ALPHA * gate)
        y_ref[...] = _dot(act.astype(BF16), wd_s[...]) + bd_ref[...]

    @pl.when(i >= nused_ref[0])
    def _():
        y_ref[...] = jnp.zeros_like(y_ref)


def _experts(block_expert, next_expert, n_used, rows, w_gu, b_gu, w_down, b_down, layer):
    nrows = rows.shape[0]
    n_blocks = nrows // MOE_BLOCK
    d, de = w_down.shape[3], w_down.shape[2]
    grid_spec = pltpu.PrefetchScalarGridSpec(
        num_scalar_prefetch=3,
        grid=(n_blocks,),
        in_specs=[
            pl.BlockSpec((MOE_BLOCK, d // 2), lambda i, be, ne, nu: (i, 0)),
            pl.BlockSpec(memory_space=pl.ANY),
            pl.BlockSpec((None, None, 1, 2 * de), lambda i, be, ne, nu: (layer, be[i], 0, 0)),
            pl.BlockSpec(memory_space=pl.ANY),
            pl.BlockSpec((None, None, 1, d), lambda i, be, ne, nu: (layer, be[i], 0, 0)),
        ],
        out_specs=pl.BlockSpec((MOE_BLOCK, d), lambda i, be, ne, nu: (i, 0)),
        scratch_shapes=[pltpu.VMEM((d, 2 * de), F32), pltpu.VMEM((de, d), F32),
                        pltpu.VMEM((d, 2 * de), BF16), pltpu.VMEM((de, d), BF16),
                        pltpu.SemaphoreType.DMA((2,))],
    )
    return pl.pallas_call(
        functools.partial(_expert_kernel, layer=layer),
        grid_spec=grid_spec,
        out_shape=jax.ShapeDtypeStruct((nrows, d), F32),
        compiler_params=_cparams("arbitrary"),
        name="experts",
    )(block_expert, next_expert, n_used, rows, w_gu, b_gu.reshape(b_gu.shape[0], b_gu.shape[1], 1, -1),
      w_down, b_down.reshape(b_down.shape[0], b_down.shape[1], 1, -1))


def _sc_mesh():
    return plsc.VectorSubcoreMesh(core_axis_name="core", subcore_axis_name="subcore")


def _sc_scatter_rows(x, pos, n_rows):
    t, c = x.shape
    nk = pos.shape[0]
    nwin = t // SC_SCATTER_WIN
    assert nwin * SC_SCATTER_WIN == t
    idx = pos.reshape(nk * nwin, SC_SCATTER_WIN)

    @functools.partial(pl.kernel, out_type=jax.ShapeDtypeStruct((n_rows, c), x.dtype), mesh=_sc_mesh(),
                       scratch_types=[], name="sc_scatter_rows")
    def kern(x_hbm, i_hbm, o_hbm):
        def body(x_vmem, i_vmem):
            pltpu.sync_copy(x_vmem, o_hbm.at[i_vmem.at[0]])

        pltpu.emit_pipeline(
            body,
            grid=(nk * nwin,),
            in_specs=[pl.BlockSpec((SC_SCATTER_WIN, c), lambda i: (i % nwin, 0)),
                      pl.BlockSpec((1, SC_SCATTER_WIN), lambda i: (i, 0))],
            out_specs=[],
            core_axis_name=("core", "subcore"),
            dimension_semantics=(pltpu.PARALLEL,),
        )(x_hbm, i_hbm)

    return kern(x, idx)


def _sc_gather_rows(y, pos):
    nk, t = pos.shape
    c = y.shape[1]
    n = nk * t
    assert n % SC_GATHER_WIN == 0
    idx = pos.reshape(n // SC_GATHER_WIN, SC_GATHER_WIN)

    @functools.partial(pl.kernel, out_type=jax.ShapeDtypeStruct((n, c), y.dtype), mesh=_sc_mesh(),
                       scratch_types=[], name="sc_gather_rows")
    def kern(y_hbm, i_hbm, o_hbm):
        def body(i_vmem, o_vmem):
            pltpu.sync_copy(y_hbm.at[i_vmem.at[0]], o_vmem)

        pltpu.emit_pipeline(
            body,
            grid=(n // SC_GATHER_WIN,),
            in_specs=[pl.BlockSpec((1, SC_GATHER_WIN), lambda i: (i, 0))],
            out_specs=[pl.BlockSpec((SC_GATHER_WIN, c), lambda i: (i, 0))],
            core_axis_name=("core", "subcore"),
            dimension_semantics=(pltpu.PARALLEL,),
        )(i_hbm, o_hbm)

    return kern(y, idx).reshape(nk, t, c)


def _combine_kernel(yg_ref, gw_ref, x1_ref, gtp_ref, gts_ref, gfin_ref, *out_refs, n_prompt_tiles, final):
    is_sample = pl.program_id(0) >= n_prompt_tiles
    gate2 = jnp.where(is_sample, gts_ref[...], gtp_ref[...])
    gw = gw_ref[...]
    acc = gw[:, 0:1] * yg_ref[0]
    for k in range(1, TOP_K):
        acc = acc + gw[:, k:k + 1] * yg_ref[k]
    x2 = x1_ref[...] + gate2 * acc
    out_refs[0][...] = _rms(x2) * gfin_ref[...] if final else x2


def _combine(yg, gates, x1, mod_p, mod_s, g_final, tp, n_batch, final):
    t_all, d = x1.shape
    tm = ROW_TILE
    n_prompt_tiles = tp // tm
    tiles_per_batch = n_prompt_tiles // n_batch
    gtp, gts = _mod_specs(5, tm, n_prompt_tiles, tiles_per_batch, n_batch)
    row = lambda w: pl.BlockSpec((tm, w), lambda i: (i, 0))
    return pl.pallas_call(
        functools.partial(_combine_kernel, n_prompt_tiles=n_prompt_tiles, final=final),
        grid=(t_all // tm,),
        in_specs=[pl.BlockSpec((TOP_K, tm, d), lambda i: (0, i, 0)), row(TOP_K), row(d), gtp, gts,
                  pl.BlockSpec((1, d), lambda i: (0, 0))],
        out_specs=row(d),
        out_shape=jax.ShapeDtypeStruct((t_all, d), F32),
        compiler_params=_cparams("parallel"),
        name="moe_combine",
    )(yg, gates, x1, mod_p, mod_s, g_final.reshape(1, d))


def _positions(eidx, rank, counts):
    t = eidx.shape[0]
    n_blocks = t * TOP_K // MOE_BLOCK + N_EXPERTS
    padded = (counts + MOE_BLOCK - 1) // MOE_BLOCK * MOE_BLOCK
    pad_end = jnp.cumsum(padded)
    pad_start = pad_end - padded
    experts = jnp.arange(N_EXPERTS, dtype=jnp.int32)
    start = jnp.sum(jnp.where(eidx[:, :, None] == experts, pad_start, 0), axis=-1)
    pos = (start + rank).T.astype(jnp.int32)
    n_used = (pad_end[-1] // MOE_BLOCK).astype(jnp.int32)
    blk = jnp.minimum(jnp.arange(n_blocks, dtype=jnp.int32), n_used - 1)
    block_expert = jnp.sum((pad_end[None, :] <= (blk * MOE_BLOCK)[:, None]).astype(jnp.int32), axis=1)
    block_expert = jnp.minimum(block_expert, N_EXPERTS - 1).astype(jnp.int32)
    later_used = (counts[None, :] > 0) & (experts[None, :] > experts[:, None])
    nxt = jnp.min(jnp.where(later_used, experts[None, :], N_EXPERTS), axis=1)
    nxt = jnp.where(nxt < N_EXPERTS, nxt, -1)
    next_expert = jnp.sum(jnp.where(block_expert[:, None] == experts[None, :], nxt[None, :], 0), axis=1)
    return pos, block_expert, next_expert.astype(jnp.int32), n_used.reshape(1), n_blocks


def kernel(x_prompt, x_sample, cache_kv_w128, cache_kv_w512, cache_kv_w2048, state_mlstm_C, state_mlstm_n,
           state_mlstm_m, c_prompt, c_sample, w_ada, b_ada, g_mix, g_ffn, w_in, b_gates, g_mlstm, w_br_attn,
           w_br_mlstm, w_out, w_router, b_router, w_gu, b_gu, w_down, b_down, g_final):
    n_batch, seq, d = x_prompt.shape
    n_seq, dec_seq, _ = x_sample.shape
    depth = w_ada.shape[0]
    tp, ts = n_batch * seq, n_seq * dec_seq
    t_all = tp + ts

    x_all = jnp.concatenate([x_prompt.reshape(tp, d), x_sample.reshape(ts, d)], axis=0)
    mod_all = _ada_mod(jnp.concatenate([c_prompt, c_sample], axis=0), w_ada, b_ada)
    caches_t = [jnp.transpose(c, (0, 1, 3, 4, 5, 2)).reshape(c.shape[0], c.shape[1], 2, HEADS // 2, 2 * HEAD_DIM, c.shape[2])
                for c in (cache_kv_w128, cache_kv_w512, cache_kv_w2048)]
    sn = state_mlstm_n.reshape(depth, n_seq, MLSTM_HEADS, 1, MLSTM_DK)
    sm = state_mlstm_m.reshape(depth, n_seq, MLSTM_HEADS, 1, 1)
    wr_pad = jnp.pad(w_router, ((0, 0), (0, 0), (0, LANES - N_EXPERTS))).astype(BF16)
    br_pad = jnp.pad(b_router, ((0, 0), (0, LANES - N_EXPERTS))).reshape(depth, 1, LANES)
    bg_pad = jnp.pad(b_gates, ((0, 0), (0, LANES - b_gates.shape[1]))).reshape(depth, 1, LANES)

    kvp = [[] for _ in range(N_GROUPS)]
    kvs = [[] for _ in range(N_GROUPS)]
    mp = [[], [], []]
    ms = [[], [], []]
    attn = jnp.zeros((t_all, GROUP_W), F32)
    hm = jnp.zeros((t_all, MLSTM_HEADS * MLSTM_DV), F32)
    for l in range(depth):
        mod_p = mod_all[l, :n_batch].reshape(n_batch, 1, 6 * d)
        mod_s = jnp.repeat(mod_all[l, n_batch:], dec_seq, axis=0)
        qkv, ml, gt = _inproj(x_all, g_mix[l], mod_p, mod_s, _split_w_in(w_in[l]), tp, n_batch)

        attn = _attn_prompt(qkv, attn, n_batch, seq)
        attn = _attn_sample(qkv, caches_t, l, attn, tp, n_seq, dec_seq)

        hm, c_p, n_p, m_p = _mlstm_prompt(ml, bg_pad[l], g_mlstm[l], hm, n_batch, seq)
        hm, c_s, n_s, m_s = _mlstm_sample(ml, bg_pad[l], g_mlstm[l], state_mlstm_C, sn, sm, l, hm, tp, n_seq, dec_seq)

        x1, h2w, gates, eidx, rank, counts = _postmix(
            attn, hm, gt, x_all, mod_p, mod_s, g_ffn[l], w_br_attn[l].astype(BF16), w_br_mlstm[l].astype(BF16),
            w_out[l].astype(BF16), wr_pad[l], br_pad[l], tp, n_batch)

        pos, block_expert, next_expert, n_used, n_blocks = _positions(eidx, rank, counts[0, :N_EXPERTS])
        rows = _sc_scatter_rows(h2w, pos, n_blocks * MOE_BLOCK)
        y = _experts(block_expert, next_expert, n_used, rows, w_gu, b_gu, w_down, b_down, l)
        yg = _sc_gather_rows(y, pos)
        x_all = _combine(yg, gates, x1, mod_p, mod_s, g_final, tp, n_batch, final=(l == depth - 1))

        kv_s = _kv_sample(qkv, tp, n_seq, dec_seq)
        for g in range(N_GROUPS):
            kvp[g].append(_kv_prompt(qkv, g, n_batch, seq))
            kvs[g].append(kv_s[g])
        for lst, val in zip(mp, (c_p, n_p.reshape(n_batch, MLSTM_HEADS, MLSTM_DK), m_p.reshape(n_batch, MLSTM_HEADS))):
            lst.append(val)
        for lst, val in zip(ms, (c_s, n_s.reshape(n_seq, MLSTM_HEADS, MLSTM_DK), m_s.reshape(n_seq, MLSTM_HEADS))):
            lst.append(val)

    y_prompt = x_all[:tp].reshape(n_batch, seq, d)
    y_sample = x_all[tp:].reshape(n_seq, dec_seq, d)

    def kv_prompt_out(parts):
        a = jnp.stack(parts)
        a = a.reshape(depth, n_batch, 2, HEADS, HEAD_DIM, a.shape[-1])
        return jnp.transpose(a, (0, 1, 5, 2, 3, 4))

    def kv_sample_out(parts):
        a = jnp.stack(parts).reshape(depth, dec_seq, 2, HEADS, HEAD_DIM, n_seq)
        return jnp.transpose(a, (0, 5, 1, 2, 3, 4))

    return (y_prompt, y_sample,
            kv_prompt_out(kvp[0]), kv_prompt_out(kvp[1]), kv_prompt_out(kvp[2]),
            jnp.stack(mp[0]), jnp.stack(mp[1]), jnp.stack(mp[2]),
            kv_sample_out(kvs[0]), kv_sample_out(kvs[1]), kv_sample_out(kvs[2]),
            jnp.stack(ms[0]), jnp.stack(ms[1]), jnp.stack(ms[2]))
```

```python
import functools

import jax
import jax.numpy as jnp
import numpy as np
from jax import lax
from jax.experimental import pallas as pl
from jax.experimental.pallas import tpu as pltpu
from jax.experimental.pallas import tpu_sc as plsc

F32 = jnp.float32
BF16 = jnp.bfloat16

ATTN_GROUPS = ((128, 1), (512, 4), (2048, 16))
N_GROUPS = len(ATTN_GROUPS)
HEADS = 8
HEAD_DIM = 64
ATTN_BLOCK = 128
GROUP_W = HEADS * HEAD_DIM
ATTN_QKV = N_GROUPS * GROUP_W
MLSTM_HEADS = 4
MLSTM_DK = 128
MLSTM_DV = 256
N_EXPERTS = 32
TOP_K = 4
SWIGLU_LIMIT = 7.0
SWIGLU_ALPHA = 1.702
RMS_EPS = 1e-6
NEG_BIG = -1e30

LANES = 128
SUBLANES = 8
VMEM_LIMIT_BYTES = 56 * 1024 * 1024

ROW_TILE = 256
ATTN_TILE = 2048
ATTN_UNROLL = 4
MLSTM_CHUNK = 256
MOE_BLOCK = 512
POSTMIX_PARTS = 2
SC_SCATTER_WIN = 48
SC_GATHER_WIN = 32

ML_W = 2 * MLSTM_HEADS * MLSTM_DK + 2 * MLSTM_HEADS * MLSTM_DV
ML_PAD_W = ML_W + LANES
GATE_W = 2048


def _cparams(*sem):
    return pltpu.CompilerParams(dimension_semantics=sem, vmem_limit_bytes=VMEM_LIMIT_BYTES)


def _dot(a, b):
    return jnp.dot(a, b, preferred_element_type=F32)


def _dot_nt(a, b):
    return lax.dot_general(a, b, (((1,), (1,)), ((), ())), preferred_element_type=F32)


def _dot_tn(a, b):
    return lax.dot_general(a, b, (((0,), (0,)), ((), ())), preferred_element_type=F32)


def _ada_kernel(c_ref, w_ref, b_ref, o_ref):
    c = c_ref[...]
    s = c * jax.nn.sigmoid(c)
    o_ref[...] = _dot(s.astype(BF16), w_ref[...].astype(BF16)) + b_ref[...]


def _ada_mod(c_all, w_ada, b_ada):
    depth, d, n = w_ada.shape
    bc = c_all.shape[0]
    tn = 1024
    return pl.pallas_call(
        _ada_kernel,
        grid=(depth, n // tn),
        in_specs=[
            pl.BlockSpec((bc, d), lambda l, j: (0, 0)),
            pl.BlockSpec((None, d, tn), lambda l, j: (l, 0, j)),
            pl.BlockSpec((None, 1, tn), lambda l, j: (l, 0, j)),
        ],
        out_specs=pl.BlockSpec((None, bc, tn), lambda l, j: (l, 0, j)),
        out_shape=jax.ShapeDtypeStruct((depth, bc, n), F32),
        compiler_params=_cparams("parallel", "parallel"),
        name="ada_mod",
    )(c_all, w_ada, b_ada.reshape(depth, 1, n))


def _mod_specs(chunk, tm, n_prompt_tiles, tiles_per_batch, n_batch, layer):
    d = 1024
    sp = pl.BlockSpec((None, None, 1, d),
                      lambda i: (layer, jnp.minimum(i // tiles_per_batch, n_batch - 1), 0, chunk))
    ss = pl.BlockSpec((None, tm, d), lambda i: (layer, jnp.maximum(i - n_prompt_tiles, 0), chunk))
    return sp, ss


def _sigmoid(x):
    return 0.5 * jnp.tanh(0.5 * x) + 0.5


def _rms(x):
    return x * lax.rsqrt(jnp.mean(x * x, axis=-1, keepdims=True) + RMS_EPS)


def _inproj_kernel(x_ref, g_ref, shp_ref, scp_ref, shs_ref, scs_ref, wa_ref, wif_ref, wg_ref,
                   qkv_ref, ml_ref, gt_ref, *, n_prompt_tiles):
    is_sample = pl.program_id(0) >= n_prompt_tiles
    sh = jnp.where(is_sample, shs_ref[...], shp_ref[...])
    sc = jnp.where(is_sample, scs_ref[...], scp_ref[...])
    h = (_rms(x_ref[...]) * g_ref[...] * (1.0 + sc) + sh).astype(BF16)
    cw = 512
    a = 3 * ATTN_QKV
    for c0 in range(0, a, cw):
        qkv_ref[:, c0:c0 + cw] = _dot(h, wa_ref[:, c0:c0 + cw])
    for c0 in range(0, ML_W, cw):
        ml_ref[:, c0:c0 + cw] = _dot(h, wa_ref[:, a + c0:a + c0 + cw])
    ml_ref[:, ML_W:] = _dot(h, wif_ref[...])
    for c0 in range(0, GATE_W, cw):
        gt_ref[:, c0:c0 + cw] = _dot(h, wg_ref[:, c0:c0 + cw])


def _split_w_in(w_in_l):
    a = 3 * ATTN_QKV + ML_W
    if_w = 2 * MLSTM_HEADS
    w_if = jnp.pad(w_in_l[:, a:a + if_w], ((0, 0), (0, LANES - if_w)))
    return w_in_l[:, :a].astype(BF16), w_if.astype(BF16), w_in_l[:, a + if_w:].astype(BF16)


def _inproj(x_all, g_mix, mod_p, mod_s, layer, w_parts, tp, n_batch):
    t_all, d = x_all.shape
    tm = ROW_TILE
    n_prompt_tiles = tp // tm
    tiles_per_batch = n_prompt_tiles // n_batch
    shp, shs = _mod_specs(0, tm, n_prompt_tiles, tiles_per_batch, n_batch, layer)
    scp, scs = _mod_specs(1, tm, n_prompt_tiles, tiles_per_batch, n_batch, layer)
    row = lambda w: pl.BlockSpec((tm, w), lambda i: (i, 0))
    return pl.pallas_call(
        functools.partial(_inproj_kernel, n_prompt_tiles=n_prompt_tiles),
        grid=(t_all // tm,),
        in_specs=[
            row(d),
            pl.BlockSpec((1, d), lambda i: (0, 0)),
            shp, scp, shs, scs,
        ] + [pl.BlockSpec(w.shape, lambda i: (0, 0), pipeline_mode=pl.Buffered(1)) for w in w_parts],
        out_specs=[row(3 * ATTN_QKV), row(ML_PAD_W), row(GATE_W)],
        out_shape=[
            jax.ShapeDtypeStruct((t_all, 3 * ATTN_QKV), F32),
            jax.ShapeDtypeStruct((t_all, ML_PAD_W), F32),
            jax.ShapeDtypeStruct((t_all, GATE_W), F32),
        ],
        compiler_params=_cparams("parallel"),
        name="inproj",
    )(x_all, g_mix.reshape(1, d), mod_p, mod_p, mod_s, mod_s, *w_parts)


def _attn_bias(slope, dil, valid_prev):
    qi = lax.broadcasted_iota(jnp.int32, (ATTN_BLOCK, 2 * ATTN_BLOCK), 0)
    ki = lax.broadcasted_iota(jnp.int32, (ATTN_BLOCK, 2 * ATTN_BLOCK), 1)
    dist = qi + ATTN_BLOCK - ki
    keep = (dist >= 0) & (dist <= ATTN_BLOCK) & ((ki >= ATTN_BLOCK) | valid_prev)
    return jnp.where(keep, -slope * (dil * dist).astype(F32), NEG_BIG)


def _attn_units(blocks):
    lane = lax.broadcasted_iota(jnp.int32, (ATTN_BLOCK, LANES), 1)
    first = lane < HEAD_DIM
    scores, values = [], []
    for q, k2, v2, bias_ref in blocks:
        k2b = k2.astype(BF16)
        qs = q * (HEAD_DIM ** -0.5)
        for e in range(2):
            qh = jnp.where(first if e == 0 else ~first, qs, 0.0).astype(BF16)
            scores.append(_dot_nt(qh, k2b) + bias_ref[e])
        values.append(v2.astype(BF16))
    probs = []
    for s in scores:
        m = jnp.max(s, axis=-1, keepdims=True)
        p = jnp.exp(s - m)
        l = jnp.sum(p, axis=-1, keepdims=True)
        probs.append((p.astype(BF16), l, m + jnp.log(l)))
    outs = []
    for u, v2b in enumerate(values):
        (p0, l0, lse0), (p1, l1, lse1) = probs[2 * u], probs[2 * u + 1]
        o0 = _dot(p0, v2b) / l0
        o1 = _dot(p1, v2b) / l1
        outs.append((jnp.where(first, o0, o1), jnp.where(first, lse0, lse1)))
    return outs


def _attn_prompt_kernel(slope_ref, *refs):
    ins, o_ref, o_s, l_s, bias_s, bias_first_s = refs[:15], refs[16], refs[17], refs[18], refs[19], refs[20]
    hp = pl.program_id(1)
    tile = pl.program_id(2)
    for g, (_, dil) in enumerate(ATTN_GROUPS):
        q_ref, kc_ref, vc_ref, kp_ref, vp_ref = ins[5 * g:5 * g + 5]
        sub = ATTN_BLOCK * dil
        n_first = dil
        n_units = ATTN_TILE // ATTN_BLOCK
        for e in range(2):
            bias_s[e] = _attn_bias(slope_ref[2 * hp + e], dil, True)
            bias_first_s[e] = _attn_bias(slope_ref[2 * hp + e], dil, tile > 0)

        def strided(ref, start, size):
            return ref[pl.ds(start, size, stride=dil), :] if dil > 1 else ref[pl.ds(start, size), :]

        def store(start, o, lse):
            if dil > 1:
                o_s[g, pl.ds(start, ATTN_BLOCK, stride=dil), :] = o
                l_s[g, pl.ds(start, ATTN_BLOCK, stride=dil), :] = lse
            else:
                o_s[g, pl.ds(start, ATTN_BLOCK), :] = o
                l_s[g, pl.ds(start, ATTN_BLOCK), :] = lse

        def first_block(r):
            k2 = jnp.concatenate([strided(kp_ref, r, ATTN_BLOCK), strided(kc_ref, r, ATTN_BLOCK)], axis=0)
            v2 = jnp.concatenate([strided(vp_ref, r, ATTN_BLOCK), strided(vc_ref, r, ATTN_BLOCK)], axis=0)
            return r, (strided(q_ref, r, ATTN_BLOCK), k2, v2, bias_first_s)

        def rest_block(u):
            j = u // dil
            start = j * sub + (u - j * dil)
            return start, (strided(q_ref, start, ATTN_BLOCK), strided(kc_ref, start - sub, 2 * ATTN_BLOCK),
                           strided(vc_ref, start - sub, 2 * ATTN_BLOCK), bias_s)

        def run_group(base, firsts):
            placed = [first_block(base + i) if f else rest_block(base + i) for i, f in enumerate(firsts)]
            for (start, _), (o, lse) in zip(placed, _attn_units([b for _, b in placed])):
                store(start, o, lse)

        patterns = [tuple(gi * ATTN_UNROLL + i < n_first for i in range(ATTN_UNROLL))
                    for gi in range(n_units // ATTN_UNROLL)]
        gi = 0
        while gi < len(patterns):
            end = gi
            while end < len(patterns) and patterns[end] == patterns[gi]:
                end += 1
            if end - gi == 1:
                run_group(gi * ATTN_UNROLL, patterns[gi])
            else:
                def body(it, carry, pattern=patterns[gi]):
                    run_group(it * ATTN_UNROLL, pattern)
                    return carry
                lax.fori_loop(gi, end, body, 0)
            gi = end

    m = jnp.maximum(jnp.maximum(l_s[0], l_s[1]), l_s[2])
    w0 = jnp.exp(l_s[0] - m)
    w1 = jnp.exp(l_s[1] - m)
    w2 = jnp.exp(l_s[2] - m)
    o_ref[...] = (w0 * o_s[0] + w1 * o_s[1] + w2 * o_s[2]) / (w0 + w1 + w2)


def _alibi_slopes():
    return jnp.asarray(2.0 ** (-8.0 * np.arange(1, HEADS + 1) / HEADS), dtype=F32)


def _attn_prompt(qkv, attn, n_batch, seq):
    tiles = seq // ATTN_TILE
    slabs = GROUP_W // LANES
    in_specs = [pl.BlockSpec(memory_space=pltpu.SMEM)]
    args = [_alibi_slopes()]
    for g, (_, dil) in enumerate(ATTN_GROUPS):
        sub = ATTN_BLOCK * dil
        per_tile = ATTN_TILE // sub
        for which in range(3):
            col = which * (ATTN_QKV // LANES) + g * slabs
            in_specs.append(pl.BlockSpec((ATTN_TILE, LANES), lambda b, hp, t, col=col: (b * tiles + t, col + hp)))
            args.append(qkv)
        for which in (1, 2):
            col = which * (ATTN_QKV // LANES) + g * slabs
            in_specs.append(pl.BlockSpec(
                (sub, LANES),
                lambda b, hp, t, col=col, per_tile=per_tile: (jnp.maximum((b * tiles + t) * per_tile - 1, 0), col + hp)))
            args.append(qkv)
    in_specs.append(pl.BlockSpec(memory_space=pl.ANY))
    args.append(attn)
    return pl.pallas_call(
        _attn_prompt_kernel,
        grid=(n_batch, slabs, tiles),
        in_specs=in_specs,
        out_specs=pl.BlockSpec((ATTN_TILE, LANES), lambda b, hp, t: (b * tiles + t, hp)),
        out_shape=jax.ShapeDtypeStruct(attn.shape, attn.dtype),
        input_output_aliases={len(args) - 1: 0},
        scratch_shapes=[pltpu.VMEM((N_GROUPS, ATTN_TILE, LANES), F32), pltpu.VMEM((N_GROUPS, ATTN_TILE, LANES), F32),
                        pltpu.VMEM((2, ATTN_BLOCK, 2 * ATTN_BLOCK), F32), pltpu.VMEM((2, ATTN_BLOCK, 2 * ATTN_BLOCK), F32)],
        compiler_params=_cparams("parallel", "parallel", "arbitrary"),
        name="attn_prompt",
    )(*args)


def _kv_prompt_kernel(k_ref, v_ref, prev_ref, o_ref):
    for c, ref in enumerate((k_ref, v_ref)):
        for s in range(GROUP_W // LANES):
            o_ref[c, s] = ref[:, s * LANES:(s + 1) * LANES].T


def _kv_prompt(qkv, g, seq, buf, layer):
    n_batch, w = buf.shape[1], buf.shape[-1]
    tm = min(512, w)
    slabs = GROUP_W // LANES
    row0 = (seq - w) // tm
    col = lambda which: which * N_GROUPS + g
    return pl.pallas_call(
        _kv_prompt_kernel,
        grid=(n_batch, w // tm),
        in_specs=[pl.BlockSpec((tm, GROUP_W), lambda b, i: (b * (seq // tm) + row0 + i, col(1))),
                  pl.BlockSpec((tm, GROUP_W), lambda b, i: (b * (seq // tm) + row0 + i, col(2))),
                  pl.BlockSpec(memory_space=pl.ANY)],
        out_specs=pl.BlockSpec((None, None, 2, slabs, LANES, tm), lambda b, i: (layer, b, 0, 0, 0, i)),
        out_shape=jax.ShapeDtypeStruct(buf.shape, buf.dtype),
        input_output_aliases={2: 0},
        compiler_params=_cparams("parallel", "parallel"),
        name="kv_prompt",
    )(qkv, qkv, buf)


def _kv_sample_kernel(x_ref, o_ref, *, dec_seq, n_seq):
    for t in range(dec_seq):
        o_ref[t] = x_ref[pl.ds(t, n_seq, stride=dec_seq), :].T


def _kv_sample(qkv, tp, n_seq, dec_seq):
    ts = n_seq * dec_seq
    slabs = GROUP_W // LANES
    assert tp % ts == 0
    return pl.pallas_call(
        functools.partial(_kv_sample_kernel, dec_seq=dec_seq, n_seq=n_seq),
        grid=(N_GROUPS, 2, slabs),
        in_specs=[pl.BlockSpec((ts, LANES), lambda g, c, s: (tp // ts, (c + 1) * N_GROUPS * slabs + g * slabs + s))],
        out_specs=pl.BlockSpec((None, dec_seq, None, None, LANES, n_seq), lambda g, c, s: (g, 0, c, s, 0, 0)),
        out_shape=jax.ShapeDtypeStruct((N_GROUPS, dec_seq, 2, slabs, LANES, n_seq), F32),
        compiler_params=_cparams("parallel", "parallel", "parallel"),
        name="kv_sample",
    )(qkv)


def _attn_sample_kernel(slope_ref, qkv_ref, c0_ref, c1_ref, c2_ref, prev_ref, o_ref, *, dec_seq):
    n = pl.program_id(0)
    half = n % 2
    caches = (c0_ref, c1_ref, c2_ref)
    rows = 2 * SUBLANES
    row = lax.broadcasted_iota(jnp.int32, (rows, 1), 0)
    t_row = (row % SUBLANES) - half * dec_seq
    row_ok = (t_row >= 0) & (t_row < dec_seq)
    lane = lax.broadcasted_iota(jnp.int32, (SUBLANES, LANES), 1)
    first = lane < HEAD_DIM
    ucol = lax.broadcasted_iota(jnp.int32, (rows, SUBLANES), 1) - half * dec_seq
    col_ok = (ucol >= 0) & (ucol < dec_seq)
    slabs = GROUP_W // LANES
    units = [(j, g) for j in range(slabs) for g in range(N_GROUPS)]
    scale = HEAD_DIM ** -0.5
    scores = []
    for j, g in units:
        win, dil = ATTN_GROUPS[g]
        wb = caches[g].shape[-1]
        c = g * slabs + j
        slope = jnp.where(row < SUBLANES, slope_ref[2 * j], slope_ref[2 * j + 1])
        q = qkv_ref[:, c * LANES:(c + 1) * LANES]
        kn = qkv_ref[:, ATTN_QKV + c * LANES:ATTN_QKV + (c + 1) * LANES]
        q2f = jnp.concatenate([jnp.where(first, q, 0.0), jnp.where(first, 0.0, q)], axis=0)
        kt = caches[g][0, j].astype(BF16)
        w = lax.broadcasted_iota(jnp.int32, (rows, wb), 1)
        delta = wb + t_row - w
        ok = row_ok & (delta <= win) & ((delta & (dil - 1)) == 0)
        s_b = jnp.where(ok, _dot(q2f.astype(BF16), kt) * scale - slope * delta.astype(F32), NEG_BIG)
        dn = t_row - ucol
        okn = row_ok & col_ok & (dn >= 0) & (dn <= win) & ((dn & (dil - 1)) == 0)
        s_n = jnp.where(okn, _dot_nt(q2f, kn) * scale - slope * dn.astype(F32), NEG_BIG)
        scores.append((s_b, s_n))
    probs = []
    for s_b, s_n in scores:
        m = jnp.maximum(jnp.max(s_b, axis=-1, keepdims=True), jnp.max(s_n, axis=-1, keepdims=True))
        p_b = jnp.exp(s_b - m)
        p_n = jnp.exp(s_n - m)
        l = jnp.sum(p_b, axis=-1, keepdims=True) + jnp.sum(p_n, axis=-1, keepdims=True)
        probs.append((p_b.astype(BF16), p_n, l, m + jnp.log(l)))
    o_u, l_u = {}, {}
    for (j, g), (p_b, p_n, l, lse) in zip(units, probs):
        c = g * slabs + j
        vn = qkv_ref[:, 2 * ATTN_QKV + c * LANES:2 * ATTN_QKV + (c + 1) * LANES]
        vt = caches[g][1, j].astype(BF16)
        o = (_dot_nt(p_b, vt) + _dot(p_n, vn)) / l
        o_u[j, g] = jnp.where(first, o[:SUBLANES], o[SUBLANES:])
        l_u[j, g] = jnp.where(first, lse[:SUBLANES], lse[SUBLANES:])
    out_slabs = []
    for j in range(slabs):
        m = jnp.maximum(jnp.maximum(l_u[j, 0], l_u[j, 1]), l_u[j, 2])
        ws = [jnp.exp(l_u[j, g] - m) for g in range(N_GROUPS)]
        out_slabs.append((ws[0] * o_u[j, 0] + ws[1] * o_u[j, 1] + ws[2] * o_u[j, 2]) / (ws[0] + ws[1] + ws[2]))
    res = jnp.concatenate(out_slabs, axis=1)
    mine = (lax.broadcasted_iota(jnp.int32, (SUBLANES, 1), 0) // dec_seq) == half

    @pl.when(half == 0)
    def _():
        o_ref[...] = res

    @pl.when(half != 0)
    def _():
        o_ref[...] = jnp.where(mine, res, o_ref[...])


def _attn_sample(qkv, caches_t, layer, attn, tp, n_seq, dec_seq):
    assert 2 * dec_seq == SUBLANES
    blk0 = tp // SUBLANES
    in_specs = [
        pl.BlockSpec(memory_space=pltpu.SMEM),
        pl.BlockSpec((SUBLANES, 3 * ATTN_QKV), lambda n: (blk0 + n // 2, 0)),
    ]
    for c in caches_t:
        in_specs.append(pl.BlockSpec((None, None) + c.shape[2:], lambda n, layer=layer: (layer, n, 0, 0, 0, 0)))
    in_specs.append(pl.BlockSpec(memory_space=pl.ANY))
    return pl.pallas_call(
        functools.partial(_attn_sample_kernel, dec_seq=dec_seq),
        grid=(n_seq,),
        in_specs=in_specs,
        out_specs=pl.BlockSpec((SUBLANES, GROUP_W), lambda n: (blk0 + n // 2, 0)),
        out_shape=jax.ShapeDtypeStruct(attn.shape, attn.dtype),
        input_output_aliases={5: 0},
        compiler_params=_cparams("arbitrary"),
        name="attn_sample",
    )(_alibi_slopes(), qkv, *caches_t, attn)


def _log_sigmoid(x):
    return jnp.minimum(x, 0.0) - jnp.log1p(jnp.exp(-jnp.abs(x)))


def _mlstm_chunks(items):
    ln = items[0][0].shape[0]
    ii = lax.broadcasted_iota(jnp.int32, (ln, ln), 0)
    jj = lax.broadcasted_iota(jnp.int32, (ln, ln), 1)
    eye = ii == jj
    causal = jj <= ii
    stage1 = []
    for q, k, v, i_col, lf_col, c_st, n_st, m_st in items:
        lf_row = jnp.sum(jnp.where(eye, lf_col, 0.0), axis=0, keepdims=True)
        i_row = jnp.sum(jnp.where(eye, i_col, 0.0), axis=0, keepdims=True)
        b_col = jnp.sum(jnp.where(causal, lf_row, 0.0), axis=1, keepdims=True)
        b_row = jnp.sum(jnp.where(ii <= jj, lf_col, 0.0), axis=0, keepdims=True)
        dm = jnp.where(causal, b_col - b_row + i_row, -jnp.inf)
        m_inter = b_col + m_st
        m_t = jnp.maximum(m_inter, jnp.max(dm, axis=1, keepdims=True))
        ks = k * (MLSTM_DK ** -0.5)
        qb = q.astype(BF16)
        vb = v.astype(BF16)
        qk = _dot_nt(qb, ks.astype(BF16))
        qc = _dot(qb, c_st.astype(BF16))
        stage1.append((dm, m_inter, m_t, b_col, ks, vb, qk, qc))
    stage2 = []
    for (dm, m_inter, m_t, _, _, vb, qk, _) in stage1:
        w_intra = jnp.exp(dm - m_t) * qk
        stage2.append((w_intra, jnp.exp(m_inter - m_t), _dot(w_intra.astype(BF16), vb)))
    stage3 = []
    for item, s1, (w_intra, w_inter, wv) in zip(items, stage1, stage2):
        q, _, _, i_col, _, c_st, n_st, m_st = item
        _, _, m_t, b_col, ks, vb, _, qc = s1
        num = wv + w_inter * qc
        den = jnp.sum(w_intra, axis=1, keepdims=True) + w_inter * jnp.sum(q * n_st, axis=1, keepdims=True)
        h = num / jnp.maximum(jnp.abs(den), jnp.exp(-m_t))
        b_last = b_col[ln - 1:ln, :]
        w_src = b_last - b_col + i_col
        m_new = jnp.maximum(b_last + m_st, jnp.max(w_src, axis=0, keepdims=True))
        decay = jnp.exp(b_last + m_st - m_new)
        kp = jnp.exp(w_src - m_new) * ks
        stage3.append((h, decay, kp, m_new, _dot_tn(kp.astype(BF16), vb)))
    outs = []
    for item, (h, decay, kp, m_new, kv) in zip(items, stage3):
        c_st, n_st = item[5], item[6]
        outs.append((h, decay * c_st + kv, decay * n_st + jnp.sum(kp, axis=0, keepdims=True), m_new))
    return outs


def _gate_columns(gates, head):
    lane = lax.broadcasted_iota(jnp.int32, gates.shape, 1)
    i_col = jnp.sum(jnp.where(lane == head, gates, 0.0), axis=1, keepdims=True)
    f_col = jnp.sum(jnp.where(lane == head + MLSTM_HEADS, gates, 0.0), axis=1, keepdims=True)
    return i_col, _log_sigmoid(f_col)


def _mlstm_prompt_kernel(ml_ref, bg_ref, gn_ref, prev_ref, hm_ref, c_out, n_out, m_out, c_s, n_s, m_s):
    chunk = pl.program_id(1)

    @pl.when(chunk == 0)
    def _():
        c_s[...] = jnp.zeros_like(c_s)
        n_s[...] = jnp.zeros_like(n_s)
        m_s[...] = jnp.zeros_like(m_s)

    gates = ml_ref[:, ML_W:ML_W + LANES] + bg_ref[...]
    k0 = MLSTM_HEADS * MLSTM_DK
    v0 = 2 * MLSTM_HEADS * MLSTM_DK
    o0 = v0 + MLSTM_HEADS * MLSTM_DV
    items = []
    for head in range(MLSTM_HEADS):
        i_col, lf_col = _gate_columns(gates, head)
        items.append((ml_ref[:, head * MLSTM_DK:(head + 1) * MLSTM_DK],
                      ml_ref[:, k0 + head * MLSTM_DK:k0 + (head + 1) * MLSTM_DK],
                      ml_ref[:, v0 + head * MLSTM_DV:v0 + (head + 1) * MLSTM_DV],
                      i_col, lf_col, c_s[head], n_s[head], m_s[head]))
    for head, (h, c_new, n_new, m_new) in enumerate(_mlstm_chunks(items)):
        c_s[head] = c_new
        n_s[head] = n_new
        m_s[head] = m_new
        cols = slice(head * MLSTM_DV, (head + 1) * MLSTM_DV)
        og = ml_ref[:, o0 + head * MLSTM_DV:o0 + (head + 1) * MLSTM_DV]
        hm_ref[:, cols] = _rms(h) * gn_ref[:, cols] * _sigmoid(og)

    @pl.when(chunk == pl.num_programs(1) - 1)
    def _():
        c_out[...] = c_s[...]
        n_out[...] = n_s[...]
        m_out[...] = m_s[...]


def _mlstm_prompt(ml, b_gates_pad, g_mlstm, hm, n_seq, seq_len):
    ln = MLSTM_CHUNK
    chunks = seq_len // ln
    st = lambda shp: pl.BlockSpec((None, MLSTM_HEADS) + shp, lambda s, c: (s, 0, 0, 0))
    return pl.pallas_call(
        _mlstm_prompt_kernel,
        grid=(n_seq, chunks),
        in_specs=[
            pl.BlockSpec((ln, ML_PAD_W), lambda s, c: (s * chunks + c, 0)),
            pl.BlockSpec((1, LANES), lambda s, c: (0, 0)),
            pl.BlockSpec((1, MLSTM_HEADS * MLSTM_DV), lambda s, c: (0, 0)),
            pl.BlockSpec(memory_space=pl.ANY),
        ],
        out_specs=[
            pl.BlockSpec((ln, MLSTM_HEADS * MLSTM_DV), lambda s, c: (s * chunks + c, 0)),
            st((MLSTM_DK, MLSTM_DV)), st((1, MLSTM_DK)), st((1, 1)),
        ],
        out_shape=[
            jax.ShapeDtypeStruct(hm.shape, hm.dtype),
            jax.ShapeDtypeStruct((n_seq, MLSTM_HEADS, MLSTM_DK, MLSTM_DV), F32),
            jax.ShapeDtypeStruct((n_seq, MLSTM_HEADS, 1, MLSTM_DK), F32),
            jax.ShapeDtypeStruct((n_seq, MLSTM_HEADS, 1, 1), F32),
        ],
        scratch_shapes=[pltpu.VMEM((MLSTM_HEADS, MLSTM_DK, MLSTM_DV), F32),
                        pltpu.VMEM((MLSTM_HEADS, 1, MLSTM_DK), F32),
                        pltpu.VMEM((MLSTM_HEADS, 1, 1), F32)],
        input_output_aliases={3: 0},
        compiler_params=_cparams("parallel", "arbitrary"),
        name="mlstm_prompt",
    )(ml, b_gates_pad, g_mlstm.reshape(1, -1), hm)


def _mlstm_sample_kernel(ml_ref, bg_ref, gn_ref, c0_ref, n0_ref, m0_ref, prev_ref, cprev_ref,
                         hm_ref, c_out, n_out, m_out, *, seq_rows):
    ln = ml_ref.shape[0]
    row = lax.broadcasted_iota(jnp.int32, (ln, 1), 0)
    gates = ml_ref[:, ML_W:ML_W + LANES] + bg_ref[...]
    k0 = MLSTM_HEADS * MLSTM_DK
    v0 = 2 * MLSTM_HEADS * MLSTM_DK
    o0 = v0 + MLSTM_HEADS * MLSTM_DV
    per_blk = ln // seq_rows
    items = []
    for head in range(MLSTM_HEADS):
        q = ml_ref[:, head * MLSTM_DK:(head + 1) * MLSTM_DK]
        k = ml_ref[:, k0 + head * MLSTM_DK:k0 + (head + 1) * MLSTM_DK]
        v = ml_ref[:, v0 + head * MLSTM_DV:v0 + (head + 1) * MLSTM_DV]
        i_col, lf_col = _gate_columns(gates, head)
        for j in range(per_blk):
            mine = (row // seq_rows) == j
            items.append((q, k, v, jnp.where(mine, i_col, NEG_BIG), jnp.where(mine, lf_col, 0.0),
                          c0_ref[j, head], n0_ref[j, head], m0_ref[j, head]))
    outs = _mlstm_chunks(items)
    for head in range(MLSTM_HEADS):
        h_all = None
        for j in range(per_blk):
            h, c_new, n_new, m_new = outs[head * per_blk + j]
            c_out[j, head] = c_new
            n_out[j, head] = n_new
            m_out[j, head] = m_new
            h_all = h if h_all is None else jnp.where((row // seq_rows) == j, h, h_all)
        cols = slice(head * MLSTM_DV, (head + 1) * MLSTM_DV)
        og = ml_ref[:, o0 + head * MLSTM_DV:o0 + (head + 1) * MLSTM_DV]
        hm_ref[:, cols] = _rms(h_all) * gn_ref[:, cols] * _sigmoid(og)


def _mlstm_sample(ml, b_gates_pad, g_mlstm, c0, n0, m0, layer, hm, c_new_all, row0, n_seq, seq_len):
    per_blk = SUBLANES // seq_len
    assert per_blk * seq_len == SUBLANES and n_seq % per_blk == 0 and row0 % SUBLANES == 0
    blk0 = row0 // SUBLANES
    st_in = lambda shp: pl.BlockSpec((None, per_blk, MLSTM_HEADS) + shp, lambda s: (layer, s, 0, 0, 0))
    st_out = lambda shp: pl.BlockSpec((per_blk, MLSTM_HEADS) + shp, lambda s: (s, 0, 0, 0))
    return pl.pallas_call(
        functools.partial(_mlstm_sample_kernel, seq_rows=seq_len),
        grid=(n_seq // per_blk,),
        in_specs=[
            pl.BlockSpec((SUBLANES, ML_PAD_W), lambda s: (blk0 + s, 0)),
            pl.BlockSpec((1, LANES), lambda s: (0, 0)),
            pl.BlockSpec((1, MLSTM_HEADS * MLSTM_DV), lambda s: (0, 0)),
            st_in((MLSTM_DK, MLSTM_DV)), st_in((1, MLSTM_DK)), st_in((1, 1)),
            pl.BlockSpec(memory_space=pl.ANY),
            pl.BlockSpec(memory_space=pl.ANY),
        ],
        out_specs=[
            pl.BlockSpec((SUBLANES, MLSTM_HEADS * MLSTM_DV), lambda s: (blk0 + s, 0)),
            st_in((MLSTM_DK, MLSTM_DV)), st_out((1, MLSTM_DK)), st_out((1, 1)),
        ],
        out_shape=[
            jax.ShapeDtypeStruct(hm.shape, hm.dtype),
            jax.ShapeDtypeStruct(c_new_all.shape, F32),
            jax.ShapeDtypeStruct(n0.shape[1:], F32),
            jax.ShapeDtypeStruct(m0.shape[1:], F32),
        ],
        input_output_aliases={6: 0, 7: 1},
        compiler_params=_cparams("parallel"),
        name="mlstm_sample",
    )(ml, b_gates_pad, g_mlstm.reshape(1, -1), c0, n0, m0, hm, c_new_all)


def _postmix_kernel(attn_ref, hm_ref, gt_ref, x_ref, gtp_ref, shp_ref, scp_ref, gts_ref, shs_ref, scs_ref,
                    gf_ref, wa_ref, wm_ref, wo_ref, wr_ref, br_ref, x1_ref, h2_ref, gw_ref, ei_ref, rk_ref, cnt_ref,
                    cnt_s, *, n_prompt_tiles):
    is_sample = pl.program_id(0) >= n_prompt_tiles
    d = x_ref.shape[1]
    tm = x_ref.shape[0]
    pr = tm // POSTMIX_PARTS
    parts = [slice(p * pr, (p + 1) * pr) for p in range(POSTMIX_PARTS)]

    def mod(ref_s, ref_p, rows):
        return jnp.where(is_sample, ref_s[rows], ref_p[...])

    @pl.when(pl.program_id(0) == 0)
    def _():
        cnt_s[...] = jnp.zeros_like(cnt_s)

    branch = [(_dot(attn_ref[rows].astype(BF16), wa_ref[...]), _dot(hm_ref[rows].astype(BF16), wm_ref[...]))
              for rows in parts]
    proj = []
    for rows, (y_attn, y_mlstm) in zip(parts, branch):
        merged = _sigmoid(gt_ref[rows, :d]) * y_attn + _sigmoid(gt_ref[rows, d:]) * y_mlstm
        proj.append(_dot(merged.astype(BF16), wo_ref[...]))
    lane = lax.broadcasted_iota(jnp.int32, (pr, LANES), 1)
    logits = []
    for rows, z in zip(parts, proj):
        x1 = x_ref[rows] + mod(gts_ref, gtp_ref, rows) * z
        x1_ref[rows] = x1
        h2 = (_rms(x1) * gf_ref[...] * (1.0 + mod(scs_ref, scp_ref, rows)) + mod(shs_ref, shp_ref, rows)).astype(BF16)
        bits = lax.bitcast_convert_type(h2.astype(F32), jnp.int32)
        h2_ref[rows] = lax.shift_right_logical(bits[:, :d // 2], 16) | bits[:, d // 2:]
        logits.append(jnp.where(lane < N_EXPERTS, _dot(h2, wr_ref[...]) + br_ref[...], -jnp.inf))

    ri = lax.broadcasted_iota(jnp.int32, (pr, pr), 0)
    ci = lax.broadcasted_iota(jnp.int32, (pr, pr), 1)
    before = jnp.where(ci < ri, 1.0, 0.0).astype(BF16)
    picks = []
    for lg in logits:
        vals, idxs = [], []
        chosen = jnp.zeros((pr, LANES), F32)
        for _ in range(TOP_K):
            m = jnp.max(lg, axis=-1, keepdims=True)
            idx = jnp.min(jnp.where(lg == m, lane, LANES), axis=-1, keepdims=True)
            hit = lane == idx
            vals.append(m)
            idxs.append(idx)
            chosen = jnp.where(hit, 1.0, chosen)
            lg = jnp.where(hit, -jnp.inf, lg)
        picks.append((vals, idxs, chosen, _dot(before, chosen.astype(BF16))))
    total = cnt_s[...]
    for rows, (vals, idxs, chosen, within) in zip(parts, picks):
        prefix = within + total
        ex = [jnp.exp(v - vals[0]) for v in vals]
        den = ex[0] + ex[1] + ex[2] + ex[3]
        for k in range(TOP_K):
            gw_ref[rows, k:k + 1] = ex[k] / den
            ei_ref[rows, k:k + 1] = idxs[k]
            rank = jnp.sum(jnp.where(lane == idxs[k], prefix, 0.0), axis=-1, keepdims=True)
            rk_ref[rows, k:k + 1] = rank.astype(jnp.int32)
        total = total + jnp.sum(chosen, axis=0, keepdims=True)
    cnt_s[...] = total
    cnt_ref[...] = total.astype(jnp.int32)


def _postmix(attn, hm, gt, x_all, mod_p, mod_s, layer, g_ffn, wa, wm, wo, wr, br, tp, n_batch):
    t_all, d = x_all.shape
    tm = ROW_TILE
    n_prompt_tiles = tp // tm
    tiles_per_batch = n_prompt_tiles // n_batch
    specs = [_mod_specs(c, tm, n_prompt_tiles, tiles_per_batch, n_batch, layer) for c in (2, 3, 4)]
    row = lambda w: pl.BlockSpec((tm, w), lambda i: (i, 0))
    full = lambda a: pl.BlockSpec(a.shape, lambda i: (0,) * a.ndim)
    gf = g_ffn.reshape(1, d)
    return pl.pallas_call(
        functools.partial(_postmix_kernel, n_prompt_tiles=n_prompt_tiles),
        grid=(t_all // tm,),
        in_specs=[row(GROUP_W), row(d), row(GATE_W), row(d),
                  specs[0][0], specs[1][0], specs[2][0], specs[0][1], specs[1][1], specs[2][1],
                  full(gf), full(wa), full(wm), full(wo), full(wr), full(br)],
        out_specs=[row(d), row(d // 2), row(TOP_K), row(TOP_K), row(TOP_K),
                   pl.BlockSpec((1, LANES), lambda i: (0, 0))],
        out_shape=[jax.ShapeDtypeStruct((t_all, d), F32), jax.ShapeDtypeStruct((t_all, d // 2), jnp.int32),
                   jax.ShapeDtypeStruct((t_all, TOP_K), F32), jax.ShapeDtypeStruct((t_all, TOP_K), jnp.int32),
                   jax.ShapeDtypeStruct((t_all, TOP_K), jnp.int32), jax.ShapeDtypeStruct((1, LANES), jnp.int32)],
        scratch_shapes=[pltpu.VMEM((1, LANES), F32)],
        compiler_params=_cparams("arbitrary"),
        name="postmix",
    )(attn, hm, gt, x_all, mod_p, mod_p, mod_p, mod_s, mod_s, mod_s, gf, wa, wm, wo, wr, br)


def _expert_kernel(be_ref, ne_ref, nused_ref, rows_ref, wgu_hbm, bgu_ref, wd_hbm, bd_ref, y_ref,
                   wgu_f, wd_f, wgu_s, wd_s, sem, *, layer):
    i = pl.program_id(0)
    e = be_ref[i]
    prev = be_ref[jnp.maximum(i - 1, 0)]

    def weight_copies(expert):
        return (pltpu.make_async_copy(wgu_hbm.at[layer, expert], wgu_f, sem.at[0]),
                pltpu.make_async_copy(wd_hbm.at[layer, expert], wd_f, sem.at[1]))

    @pl.when(i == 0)
    def _():
        for cp in weight_copies(e):
            cp.start()

    @pl.when((i == 0) | (e != prev))
    def _():
        for cp in weight_copies(e):
            cp.wait()
        wgu_s[...] = wgu_f[...].astype(BF16)
        wd_s[...] = wd_f[...].astype(BF16)
        nxt = ne_ref[i]

        @pl.when(nxt >= 0)
        def _():
            for cp in weight_copies(nxt):
                cp.start()

    @pl.when(i < nused_ref[0])
    def _():
        de = wd_s.shape[0]
        words = rows_ref[...]
        lo = lax.bitcast_convert_type(lax.shift_left(words, 16), F32).astype(BF16)
        hi = lax.bitcast_convert_type(words & jnp.int32(-65536), F32).astype(BF16)
        rows = jnp.concatenate([lo, hi], axis=1)
        gu = _dot(rows, wgu_s[...]) + bgu_ref[...]
        gate = jnp.minimum(gu[:, :de], SWIGLU_LIMIT)
        up = jnp.clip(gu[:, de:], -SWIGLU_LIMIT, SWIGLU_LIMIT)
        act = (up + 1.0) * gate * _sigmoid(SWIGLU_ALPHA * gate)
        y_ref[...] = _dot(act.astype(BF16), wd_s[...]) + bd_ref[...]

    @pl.when(i >= nused_ref[0])
    def _():
        y_ref[...] = jnp.zeros_like(y_ref)


def _experts(block_expert, next_expert, n_used, rows, w_gu, b_gu, w_down, b_down, layer):
    nrows = rows.shape[0]
    n_blocks = nrows // MOE_BLOCK
    d, de = w_down.shape[3], w_down.shape[2]
    grid_spec = pltpu.PrefetchScalarGridSpec(
        num_scalar_prefetch=3,
        grid=(n_blocks,),
        in_specs=[
            pl.BlockSpec((MOE_BLOCK, d // 2), lambda i, be, ne, nu: (i, 0)),
            pl.BlockSpec(memory_space=pl.ANY),
            pl.BlockSpec((None, None, 1, 2 * de), lambda i, be, ne, nu: (layer, be[i], 0, 0)),
            pl.BlockSpec(memory_space=pl.ANY),
            pl.BlockSpec((None, None, 1, d), lambda i, be, ne, nu: (layer, be[i], 0, 0)),
        ],
        out_specs=pl.BlockSpec((MOE_BLOCK, d), lambda i, be, ne, nu: (i, 0)),
        scratch_shapes=[pltpu.VMEM((d, 2 * de), F32), pltpu.VMEM((de, d), F32),
                        pltpu.VMEM((d, 2 * de), BF16), pltpu.VMEM((de, d), BF16),
                        pltpu.SemaphoreType.DMA((2,))],
    )
    return pl.pallas_call(
        functools.partial(_expert_kernel, layer=layer),
        grid_spec=grid_spec,
        out_shape=jax.ShapeDtypeStruct((nrows, d), F32),
        compiler_params=_cparams("arbitrary"),
        name="experts",
    )(block_expert, next_expert, n_used, rows, w_gu, b_gu.reshape(b_gu.shape[0], b_gu.shape[1], 1, -1),
      w_down, b_down.reshape(b_down.shape[0], b_down.shape[1], 1, -1))


def _sc_mesh():
    return plsc.VectorSubcoreMesh(core_axis_name="core", subcore_axis_name="subcore")


def _sc_scatter_rows(x, pos, n_rows):
    t, c = x.shape
    nk = pos.shape[0]
    nwin = t // SC_SCATTER_WIN
    assert nwin * SC_SCATTER_WIN == t
    idx = pos.reshape(nk * nwin, SC_SCATTER_WIN)

    @functools.partial(pl.kernel, out_type=jax.ShapeDtypeStruct((n_rows, c), x.dtype), mesh=_sc_mesh(),
                       scratch_types=[], name="sc_scatter_rows")
    def kern(x_hbm, i_hbm, o_hbm):
        def body(x_vmem, i_vmem):
            pltpu.sync_copy(x_vmem, o_hbm.at[i_vmem.at[0]])

        pltpu.emit_pipeline(
            body,
            grid=(nk * nwin,),
            in_specs=[pl.BlockSpec((SC_SCATTER_WIN, c), lambda i: (i % nwin, 0)),
                      pl.BlockSpec((1, SC_SCATTER_WIN), lambda i: (i, 0))],
            out_specs=[],
            core_axis_name=("core", "subcore"),
            dimension_semantics=(pltpu.PARALLEL,),
        )(x_hbm, i_hbm)

    return kern(x, idx)


def _sc_gather_rows(y, pos):
    nk, t = pos.shape
    c = y.shape[1]
    n = nk * t
    assert n % SC_GATHER_WIN == 0
    idx = pos.reshape(n // SC_GATHER_WIN, SC_GATHER_WIN)

    @functools.partial(pl.kernel, out_type=jax.ShapeDtypeStruct((n, c), y.dtype), mesh=_sc_mesh(),
                       scratch_types=[], name="sc_gather_rows")
    def kern(y_hbm, i_hbm, o_hbm):
        def body(i_vmem, o_vmem):
            pltpu.sync_copy(y_hbm.at[i_vmem.at[0]], o_vmem)

        pltpu.emit_pipeline(
            body,
            grid=(n // SC_GATHER_WIN,),
            in_specs=[pl.BlockSpec((1, SC_GATHER_WIN), lambda i: (i, 0))],
            out_specs=[pl.BlockSpec((SC_GATHER_WIN, c), lambda i: (i, 0))],
            core_axis_name=("core", "subcore"),
            dimension_semantics=(pltpu.PARALLEL,),
        )(i_hbm, o_hbm)

    return kern(y, idx).reshape(nk, t, c)


def _combine_kernel(yg_ref, gw_ref, x1_ref, gtp_ref, gts_ref, gfin_ref, *out_refs, n_prompt_tiles, final):
    is_sample = pl.program_id(0) >= n_prompt_tiles
    gate2 = jnp.where(is_sample, gts_ref[...], gtp_ref[...])
    gw = gw_ref[...]
    acc = gw[:, 0:1] * yg_ref[0]
    for k in range(1, TOP_K):
        acc = acc + gw[:, k:k + 1] * yg_ref[k]
    x2 = x1_ref[...] + gate2 * acc
    out_refs[0][...] = _rms(x2) * gfin_ref[...] if final else x2


def _combine(yg, gates, x1, mod_p, mod_s, layer, g_final, tp, n_batch, final):
    t_all, d = x1.shape
    tm = ROW_TILE
    n_prompt_tiles = tp // tm
    tiles_per_batch = n_prompt_tiles // n_batch
    gtp, gts = _mod_specs(5, tm, n_prompt_tiles, tiles_per_batch, n_batch, layer)
    row = lambda w: pl.BlockSpec((tm, w), lambda i: (i, 0))
    return pl.pallas_call(
        functools.partial(_combine_kernel, n_prompt_tiles=n_prompt_tiles, final=final),
        grid=(t_all // tm,),
        in_specs=[pl.BlockSpec((TOP_K, tm, d), lambda i: (0, i, 0)), row(TOP_K), row(d), gtp, gts,
                  pl.BlockSpec((1, d), lambda i: (0, 0))],
        out_specs=row(d),
        out_shape=jax.ShapeDtypeStruct((t_all, d), F32),
        compiler_params=_cparams("parallel"),
        name="moe_combine",
    )(yg, gates, x1, mod_p, mod_s, g_final.reshape(1, d))


def _positions(eidx, rank, counts):
    t = eidx.shape[0]
    n_blocks = t * TOP_K // MOE_BLOCK + N_EXPERTS
    padded = (counts + MOE_BLOCK - 1) // MOE_BLOCK * MOE_BLOCK
    pad_end = jnp.cumsum(padded)
    pad_start = pad_end - padded
    experts = jnp.arange(N_EXPERTS, dtype=jnp.int32)
    start = jnp.sum(jnp.where(eidx[:, :, None] == experts, pad_start, 0), axis=-1)
    pos = (start + rank).T.astype(jnp.int32)
    n_used = (pad_end[-1] // MOE_BLOCK).astype(jnp.int32)
    blk = jnp.minimum(jnp.arange(n_blocks, dtype=jnp.int32), n_used - 1)
    block_expert = jnp.sum((pad_end[None, :] <= (blk * MOE_BLOCK)[:, None]).astype(jnp.int32), axis=1)
    block_expert = jnp.minimum(block_expert, N_EXPERTS - 1).astype(jnp.int32)
    later_used = (counts[None, :] > 0) & (experts[None, :] > experts[:, None])
    nxt = jnp.min(jnp.where(later_used, experts[None, :], N_EXPERTS), axis=1)
    nxt = jnp.where(nxt < N_EXPERTS, nxt, -1)
    next_expert = jnp.sum(jnp.where(block_expert[:, None] == experts[None, :], nxt[None, :], 0), axis=1)
    return pos, block_expert, next_expert.astype(jnp.int32), n_used.reshape(1), n_blocks


def kernel(x_prompt, x_sample, cache_kv_w128, cache_kv_w512, cache_kv_w2048, state_mlstm_C, state_mlstm_n,
           state_mlstm_m, c_prompt, c_sample, w_ada, b_ada, g_mix, g_ffn, w_in, b_gates, g_mlstm, w_br_attn,
           w_br_mlstm, w_out, w_router, b_router, w_gu, b_gu, w_down, b_down, g_final):
    n_batch, seq, d = x_prompt.shape
    n_seq, dec_seq, _ = x_sample.shape
    depth = w_ada.shape[0]
    tp, ts = n_batch * seq, n_seq * dec_seq
    t_all = tp + ts

    x_all = jnp.concatenate([x_prompt.reshape(tp, d), x_sample.reshape(ts, d)], axis=0)
    mod_p = _ada_mod(c_prompt, w_ada, b_ada).reshape(depth, n_batch, 1, 6 * d)
    mod_s = _ada_mod(jnp.repeat(c_sample, dec_seq, axis=0), w_ada, b_ada)
    caches_t = [jnp.transpose(c, (0, 1, 3, 4, 5, 2)).reshape(c.shape[0], c.shape[1], 2, HEADS // 2, 2 * HEAD_DIM, c.shape[2])
                for c in (cache_kv_w128, cache_kv_w512, cache_kv_w2048)]
    sn = state_mlstm_n.reshape(depth, n_seq, MLSTM_HEADS, 1, MLSTM_DK)
    sm = state_mlstm_m.reshape(depth, n_seq, MLSTM_HEADS, 1, 1)
    wr_pad = jnp.pad(w_router, ((0, 0), (0, 0), (0, LANES - N_EXPERTS))).astype(BF16)
    br_pad = jnp.pad(b_router, ((0, 0), (0, LANES - N_EXPERTS))).reshape(depth, 1, LANES)
    bg_pad = jnp.pad(b_gates, ((0, 0), (0, LANES - b_gates.shape[1]))).reshape(depth, 1, LANES)

    kvs = [[] for _ in range(N_GROUPS)]
    mp = [[], [], []]
    ms = [[], []]
    attn = jnp.zeros((t_all, GROUP_W), F32)
    hm = jnp.zeros((t_all, MLSTM_HEADS * MLSTM_DV), F32)
    kvp = [jnp.zeros((depth, n_batch, 2, HEADS // 2, 2 * HEAD_DIM, min(win, seq)), F32) for win, _ in ATTN_GROUPS]
    c_s = jnp.zeros(state_mlstm_C.shape, F32)
    for l in range(depth):
        qkv, ml, gt = _inproj(x_all, g_mix[l], mod_p, mod_s, l, _split_w_in(w_in[l]), tp, n_batch)

        attn = _attn_prompt(qkv, attn, n_batch, seq)
        attn = _attn_sample(qkv, caches_t, l, attn, tp, n_seq, dec_seq)

        hm, c_p, n_p, m_p = _mlstm_prompt(ml, bg_pad[l], g_mlstm[l], hm, n_batch, seq)
        hm, c_s, n_s, m_s = _mlstm_sample(ml, bg_pad[l], g_mlstm[l], state_mlstm_C, sn, sm, l, hm, c_s,
                                          tp, n_seq, dec_seq)

        x1, h2w, gates, eidx, rank, counts = _postmix(
            attn, hm, gt, x_all, mod_p, mod_s, l, g_ffn[l], w_br_attn[l].astype(BF16), w_br_mlstm[l].astype(BF16),
            w_out[l].astype(BF16), wr_pad[l], br_pad[l], tp, n_batch)

        pos, block_expert, next_expert, n_used, n_blocks = _positions(eidx, rank, counts[0, :N_EXPERTS])
        rows = _sc_scatter_rows(h2w, pos, n_blocks * MOE_BLOCK)
        y = _experts(block_expert, next_expert, n_used, rows, w_gu, b_gu, w_down, b_down, l)
        yg = _sc_gather_rows(y, pos)
        x_all = _combine(yg, gates, x1, mod_p, mod_s, l, g_final, tp, n_batch, final=(l == depth - 1))

        kv_s = _kv_sample(qkv, tp, n_seq, dec_seq)
        for g in range(N_GROUPS):
            kvp[g] = _kv_prompt(qkv, g, seq, kvp[g], l)
            kvs[g].append(kv_s[g])
        for lst, val in zip(mp, (c_p, n_p.reshape(n_batch, MLSTM_HEADS, MLSTM_DK), m_p.reshape(n_batch, MLSTM_HEADS))):
            lst.append(val)
        for lst, val in zip(ms, (n_s.reshape(n_seq, MLSTM_HEADS, MLSTM_DK), m_s.reshape(n_seq, MLSTM_HEADS))):
            lst.append(val)

    y_prompt = x_all[:tp].reshape(n_batch, seq, d)
    y_sample = x_all[tp:].reshape(n_seq, dec_seq, d)

    def kv_prompt_out(a):
        a = a.reshape(depth, n_batch, 2, HEADS, HEAD_DIM, a.shape[-1])
        return jnp.transpose(a, (0, 1, 5, 2, 3, 4))

    def kv_sample_out(parts):
        a = jnp.stack(parts).reshape(depth, dec_seq, 2, HEADS, HEAD_DIM, n_seq)
        return jnp.transpose(a, (0, 5, 1, 2, 3, 4))

    return (y_prompt, y_sample,
            kv_prompt_out(kvp[0]), kv_prompt_out(kvp[1]), kv_prompt_out(kvp[2]),
            jnp.stack(mp[0]), jnp.stack(mp[1]), jnp.stack(mp[2]),
            kv_sample_out(kvs[0]), kv_sample_out(kvs[1]), kv_sample_out(kvs[2]),
            c_s, jnp.stack(ms[0]), jnp.stack(ms[1]))
```

```python
import functools

import jax
import jax.numpy as jnp
import numpy as np
from jax import lax
from jax.experimental import pallas as pl
from jax.experimental.pallas import tpu as pltpu
from jax.experimental.pallas import tpu_sc as plsc

F32 = jnp.float32
BF16 = jnp.bfloat16

ATTN_GROUPS = ((128, 1), (512, 4), (2048, 16))
N_GROUPS = len(ATTN_GROUPS)
HEADS = 8
HEAD_DIM = 64
ATTN_BLOCK = 128
GROUP_W = HEADS * HEAD_DIM
ATTN_QKV = N_GROUPS * GROUP_W
MLSTM_HEADS = 4
MLSTM_DK = 128
MLSTM_DV = 256
N_EXPERTS = 32
TOP_K = 4
SWIGLU_LIMIT = 7.0
SWIGLU_ALPHA = 1.702
RMS_EPS = 1e-6
NEG_BIG = -1e30

LANES = 128
SUBLANES = 8
VMEM_LIMIT_BYTES = 56 * 1024 * 1024

ROW_TILE = 256
ATTN_TILE = 2048
ATTN_UNROLL = 8
MLSTM_CHUNK = 256
MOE_BLOCK = 512
POSTMIX_PARTS = 2
SC_SCATTER_WIN = 48
SC_GATHER_WIN = 32

ML_W = 2 * MLSTM_HEADS * MLSTM_DK + 2 * MLSTM_HEADS * MLSTM_DV
ML_PAD_W = ML_W + LANES
GATE_W = 2048


def _cparams(*sem):
    return pltpu.CompilerParams(dimension_semantics=sem, vmem_limit_bytes=VMEM_LIMIT_BYTES)


def _dot(a, b):
    return jnp.dot(a, b, preferred_element_type=F32)


def _dot_nt(a, b):
    return lax.dot_general(a, b, (((1,), (1,)), ((), ())), preferred_element_type=F32)


def _dot_tn(a, b):
    return lax.dot_general(a, b, (((0,), (0,)), ((), ())), preferred_element_type=F32)


def _ada_kernel(c_ref, w_ref, b_ref, o_ref):
    c = c_ref[...]
    s = c * jax.nn.sigmoid(c)
    o_ref[...] = _dot(s.astype(BF16), w_ref[...].astype(BF16)) + b_ref[...]


def _ada_mod(c_all, w_ada, b_ada):
    depth, d, n = w_ada.shape
    bc = c_all.shape[0]
    tn = 1024
    return pl.pallas_call(
        _ada_kernel,
        grid=(depth, n // tn),
        in_specs=[
            pl.BlockSpec((bc, d), lambda l, j: (0, 0)),
            pl.BlockSpec((None, d, tn), lambda l, j: (l, 0, j)),
            pl.BlockSpec((None, 1, tn), lambda l, j: (l, 0, j)),
        ],
        out_specs=pl.BlockSpec((None, bc, tn), lambda l, j: (l, 0, j)),
        out_shape=jax.ShapeDtypeStruct((depth, bc, n), F32),
        compiler_params=_cparams("parallel", "parallel"),
        name="ada_mod",
    )(c_all, w_ada, b_ada.reshape(depth, 1, n))


def _mod_specs(chunk, tm, n_prompt_tiles, tiles_per_batch, n_batch, layer):
    d = 1024
    sp = pl.BlockSpec((None, None, 1, d),
                      lambda i: (layer, jnp.minimum(i // tiles_per_batch, n_batch - 1), 0, chunk))
    ss = pl.BlockSpec((None, tm, d), lambda i: (layer, jnp.maximum(i - n_prompt_tiles, 0), chunk))
    return sp, ss


def _sigmoid(x):
    return 0.5 * jnp.tanh(0.5 * x) + 0.5


def _rms(x):
    return x * lax.rsqrt(jnp.mean(x * x, axis=-1, keepdims=True) + RMS_EPS)


def _inproj_kernel(x_ref, g_ref, shp_ref, scp_ref, shs_ref, scs_ref, wa_ref, wif_ref, wg_ref,
                   qkv_ref, ml_ref, gt_ref, *, n_prompt_tiles):
    is_sample = pl.program_id(0) >= n_prompt_tiles
    sh = jnp.where(is_sample, shs_ref[...], shp_ref[...])
    sc = jnp.where(is_sample, scs_ref[...], scp_ref[...])
    h = (_rms(x_ref[...]) * g_ref[...] * (1.0 + sc) + sh).astype(BF16)
    cw = 512
    a = 3 * ATTN_QKV
    for c0 in range(0, a, cw):
        qkv_ref[:, c0:c0 + cw] = _dot(h, wa_ref[:, c0:c0 + cw])
    for c0 in range(0, ML_W, cw):
        ml_ref[:, c0:c0 + cw] = _dot(h, wa_ref[:, a + c0:a + c0 + cw])
    ml_ref[:, ML_W:] = _dot(h, wif_ref[...])
    for c0 in range(0, GATE_W, cw):
        gt_ref[:, c0:c0 + cw] = _dot(h, wg_ref[:, c0:c0 + cw])


def _split_w_in(w_in_l):
    a = 3 * ATTN_QKV + ML_W
    if_w = 2 * MLSTM_HEADS
    w_if = jnp.pad(w_in_l[:, a:a + if_w], ((0, 0), (0, LANES - if_w)))
    return w_in_l[:, :a].astype(BF16), w_if.astype(BF16), w_in_l[:, a + if_w:].astype(BF16)


def _inproj(x_all, g_mix, mod_p, mod_s, layer, w_parts, tp, n_batch):
    t_all, d = x_all.shape
    tm = ROW_TILE
    n_prompt_tiles = tp // tm
    tiles_per_batch = n_prompt_tiles // n_batch
    shp, shs = _mod_specs(0, tm, n_prompt_tiles, tiles_per_batch, n_batch, layer)
    scp, scs = _mod_specs(1, tm, n_prompt_tiles, tiles_per_batch, n_batch, layer)
    row = lambda w: pl.BlockSpec((tm, w), lambda i: (i, 0))
    return pl.pallas_call(
        functools.partial(_inproj_kernel, n_prompt_tiles=n_prompt_tiles),
        grid=(t_all // tm,),
        in_specs=[
            row(d),
            pl.BlockSpec((1, d), lambda i: (0, 0)),
            shp, scp, shs, scs,
        ] + [pl.BlockSpec(w.shape, lambda i: (0, 0), pipeline_mode=pl.Buffered(1)) for w in w_parts],
        out_specs=[row(3 * ATTN_QKV), row(ML_PAD_W), row(GATE_W)],
        out_shape=[
            jax.ShapeDtypeStruct((t_all, 3 * ATTN_QKV), F32),
            jax.ShapeDtypeStruct((t_all, ML_PAD_W), F32),
            jax.ShapeDtypeStruct((t_all, GATE_W), F32),
        ],
        compiler_params=_cparams("parallel"),
        name="inproj",
    )(x_all, g_mix.reshape(1, d), mod_p, mod_p, mod_s, mod_s, *w_parts)


def _attn_bias(slope, dil, valid_prev):
    qi = lax.broadcasted_iota(jnp.int32, (ATTN_BLOCK, 2 * ATTN_BLOCK), 0)
    ki = lax.broadcasted_iota(jnp.int32, (ATTN_BLOCK, 2 * ATTN_BLOCK), 1)
    dist = qi + ATTN_BLOCK - ki
    keep = (dist >= 0) & (dist <= ATTN_BLOCK) & ((ki >= ATTN_BLOCK) | valid_prev)
    return jnp.where(keep, -slope * (dil * dist).astype(F32), NEG_BIG)


def _attn_units(blocks):
    lane = lax.broadcasted_iota(jnp.int32, (ATTN_BLOCK, LANES), 1)
    first = lane < HEAD_DIM
    scores, values = [], []
    for q, k2, v2, bias_ref in blocks:
        k2b = k2.astype(BF16)
        qs = q * (HEAD_DIM ** -0.5)
        for e in range(2):
            qh = jnp.where(first if e == 0 else ~first, qs, 0.0).astype(BF16)
            scores.append(_dot_nt(qh, k2b) + bias_ref[e])
        values.append(v2.astype(BF16))
    probs = []
    for s in scores:
        m = jnp.max(s, axis=-1, keepdims=True)
        p = jnp.exp(s - m)
        l = jnp.sum(p, axis=-1, keepdims=True)
        probs.append((p.astype(BF16), l, m + jnp.log(l)))
    outs = []
    for u, v2b in enumerate(values):
        (p0, l0, lse0), (p1, l1, lse1) = probs[2 * u], probs[2 * u + 1]
        o0 = _dot(p0, v2b) / l0
        o1 = _dot(p1, v2b) / l1
        outs.append((jnp.where(first, o0, o1), jnp.where(first, lse0, lse1)))
    return outs


def _attn_prompt_kernel(slope_ref, *refs):
    ins, o_ref, o_s, l_s, bias_s, bias_first_s = refs[:15], refs[16], refs[17], refs[18], refs[19], refs[20]
    hp = pl.program_id(1)
    tile = pl.program_id(2)
    for g, (_, dil) in enumerate(ATTN_GROUPS):
        q_ref, kc_ref, vc_ref, kp_ref, vp_ref = ins[5 * g:5 * g + 5]
        sub = ATTN_BLOCK * dil
        n_first = dil
        n_units = ATTN_TILE // ATTN_BLOCK
        for e in range(2):
            bias_s[e] = _attn_bias(slope_ref[2 * hp + e], dil, True)
            bias_first_s[e] = _attn_bias(slope_ref[2 * hp + e], dil, tile > 0)

        def strided(ref, start, size):
            return ref[pl.ds(start, size, stride=dil), :] if dil > 1 else ref[pl.ds(start, size), :]

        def store(start, o, lse):
            if dil > 1:
                o_s[g, pl.ds(start, ATTN_BLOCK, stride=dil), :] = o
                l_s[g, pl.ds(start, ATTN_BLOCK, stride=dil), :] = lse
            else:
                o_s[g, pl.ds(start, ATTN_BLOCK), :] = o
                l_s[g, pl.ds(start, ATTN_BLOCK), :] = lse

        def first_block(r):
            k2 = jnp.concatenate([strided(kp_ref, r, ATTN_BLOCK), strided(kc_ref, r, ATTN_BLOCK)], axis=0)
            v2 = jnp.concatenate([strided(vp_ref, r, ATTN_BLOCK), strided(vc_ref, r, ATTN_BLOCK)], axis=0)
            return r, (strided(q_ref, r, ATTN_BLOCK), k2, v2, bias_first_s)

        def rest_block(u):
            j = u // dil
            start = j * sub + (u - j * dil)
            return start, (strided(q_ref, start, ATTN_BLOCK), strided(kc_ref, start - sub, 2 * ATTN_BLOCK),
                           strided(vc_ref, start - sub, 2 * ATTN_BLOCK), bias_s)

        def run_group(base, firsts):
            placed = [first_block(base + i) if f else rest_block(base + i) for i, f in enumerate(firsts)]
            for (start, _), (o, lse) in zip(placed, _attn_units([b for _, b in placed])):
                store(start, o, lse)

        patterns = [tuple(gi * ATTN_UNROLL + i < n_first for i in range(ATTN_UNROLL))
                    for gi in range(n_units // ATTN_UNROLL)]
        gi = 0
        while gi < len(patterns):
            end = gi
            while end < len(patterns) and patterns[end] == patterns[gi]:
                end += 1
            if end - gi == 1:
                run_group(gi * ATTN_UNROLL, patterns[gi])
            else:
                def body(it, carry, pattern=patterns[gi]):
                    run_group(it * ATTN_UNROLL, pattern)
                    return carry
                lax.fori_loop(gi, end, body, 0)
            gi = end

    m = jnp.maximum(jnp.maximum(l_s[0], l_s[1]), l_s[2])
    w0 = jnp.exp(l_s[0] - m)
    w1 = jnp.exp(l_s[1] - m)
    w2 = jnp.exp(l_s[2] - m)
    o_ref[...] = (w0 * o_s[0] + w1 * o_s[1] + w2 * o_s[2]) / (w0 + w1 + w2)


def _alibi_slopes():
    return jnp.asarray(2.0 ** (-8.0 * np.arange(1, HEADS + 1) / HEADS), dtype=F32)


def _attn_prompt(qkv, attn, n_batch, seq):
    tiles = seq // ATTN_TILE
    slabs = GROUP_W // LANES
    in_specs = [pl.BlockSpec(memory_space=pltpu.SMEM)]
    args = [_alibi_slopes()]
    for g, (_, dil) in enumerate(ATTN_GROUPS):
        sub = ATTN_BLOCK * dil
        per_tile = ATTN_TILE // sub
        for which in range(3):
            col = which * (ATTN_QKV // LANES) + g * slabs
            in_specs.append(pl.BlockSpec((ATTN_TILE, LANES), lambda b, hp, t, col=col: (b * tiles + t, col + hp)))
            args.append(qkv)
        for which in (1, 2):
            col = which * (ATTN_QKV // LANES) + g * slabs
            in_specs.append(pl.BlockSpec(
                (sub, LANES),
                lambda b, hp, t, col=col, per_tile=per_tile: (jnp.maximum((b * tiles + t) * per_tile - 1, 0), col + hp)))
            args.append(qkv)
    in_specs.append(pl.BlockSpec(memory_space=pl.ANY))
    args.append(attn)
    return pl.pallas_call(
        _attn_prompt_kernel,
        grid=(n_batch, slabs, tiles),
        in_specs=in_specs,
        out_specs=pl.BlockSpec((ATTN_TILE, LANES), lambda b, hp, t: (b * tiles + t, hp)),
        out_shape=jax.ShapeDtypeStruct(attn.shape, attn.dtype),
        input_output_aliases={len(args) - 1: 0},
        scratch_shapes=[pltpu.VMEM((N_GROUPS, ATTN_TILE, LANES), F32), pltpu.VMEM((N_GROUPS, ATTN_TILE, LANES), F32),
                        pltpu.VMEM((2, ATTN_BLOCK, 2 * ATTN_BLOCK), F32), pltpu.VMEM((2, ATTN_BLOCK, 2 * ATTN_BLOCK), F32)],
        compiler_params=_cparams("parallel", "parallel", "arbitrary"),
        name="attn_prompt",
    )(*args)


def _kv_prompt_kernel(k_ref, v_ref, prev_ref, after_ref, o_ref):
    for c, ref in enumerate((k_ref, v_ref)):
        for s in range(GROUP_W // LANES):
            o_ref[c, s] = ref[:, s * LANES:(s + 1) * LANES].T


def _kv_prompt(qkv, g, seq, buf, layer, after):
    n_batch, w = buf.shape[1], buf.shape[-1]
    tm = min(512, w)
    slabs = GROUP_W // LANES
    row0 = (seq - w) // tm
    col = lambda which: which * N_GROUPS + g
    return pl.pallas_call(
        _kv_prompt_kernel,
        grid=(n_batch, w // tm),
        in_specs=[pl.BlockSpec((tm, GROUP_W), lambda b, i: (b * (seq // tm) + row0 + i, col(1))),
                  pl.BlockSpec((tm, GROUP_W), lambda b, i: (b * (seq // tm) + row0 + i, col(2))),
                  pl.BlockSpec(memory_space=pl.ANY), pl.BlockSpec(memory_space=pl.ANY)],
        out_specs=pl.BlockSpec((None, None, 2, slabs, LANES, tm), lambda b, i: (layer, b, 0, 0, 0, i)),
        out_shape=jax.ShapeDtypeStruct(buf.shape, buf.dtype),
        input_output_aliases={2: 0},
        compiler_params=_cparams("parallel", "parallel"),
        name="kv_prompt",
    )(qkv, qkv, buf, after)


def _kv_sample_kernel(x_ref, after_ref, o_ref, *, dec_seq, n_seq):
    for t in range(dec_seq):
        o_ref[t] = x_ref[pl.ds(t, n_seq, stride=dec_seq), :].T


def _kv_sample(qkv, tp, n_seq, dec_seq, after):
    ts = n_seq * dec_seq
    slabs = GROUP_W // LANES
    assert tp % ts == 0
    return pl.pallas_call(
        functools.partial(_kv_sample_kernel, dec_seq=dec_seq, n_seq=n_seq),
        grid=(N_GROUPS, 2, slabs),
        in_specs=[pl.BlockSpec((ts, LANES), lambda g, c, s: (tp // ts, (c + 1) * N_GROUPS * slabs + g * slabs + s)),
                  pl.BlockSpec(memory_space=pl.ANY)],
        out_specs=pl.BlockSpec((None, dec_seq, None, None, LANES, n_seq), lambda g, c, s: (g, 0, c, s, 0, 0)),
        out_shape=jax.ShapeDtypeStruct((N_GROUPS, dec_seq, 2, slabs, LANES, n_seq), F32),
        compiler_params=_cparams("parallel", "parallel", "parallel"),
        name="kv_sample",
    )(qkv, after)


def _attn_sample_kernel(slope_ref, qkv_ref, c0_ref, c1_ref, c2_ref, prev_ref, o_ref, *, dec_seq):
    n = pl.program_id(0)
    half = n % 2
    caches = (c0_ref, c1_ref, c2_ref)
    rows = 2 * SUBLANES
    row = lax.broadcasted_iota(jnp.int32, (rows, 1), 0)
    t_row = (row % SUBLANES) - half * dec_seq
    row_ok = (t_row >= 0) & (t_row < dec_seq)
    lane = lax.broadcasted_iota(jnp.int32, (SUBLANES, LANES), 1)
    first = lane < HEAD_DIM
    ucol = lax.broadcasted_iota(jnp.int32, (rows, SUBLANES), 1) - half * dec_seq
    col_ok = (ucol >= 0) & (ucol < dec_seq)
    slabs = GROUP_W // LANES
    units = [(j, g) for j in range(slabs) for g in range(N_GROUPS)]
    scale = HEAD_DIM ** -0.5
    scores = []
    for j, g in units:
        win, dil = ATTN_GROUPS[g]
        wb = caches[g].shape[-1]
        c = g * slabs + j
        slope = jnp.where(row < SUBLANES, slope_ref[2 * j], slope_ref[2 * j + 1])
        q = qkv_ref[:, c * LANES:(c + 1) * LANES]
        kn = qkv_ref[:, ATTN_QKV + c * LANES:ATTN_QKV + (c + 1) * LANES]
        q2f = jnp.concatenate([jnp.where(first, q, 0.0), jnp.where(first, 0.0, q)], axis=0)
        kt = caches[g][0, j].astype(BF16)
        w = lax.broadcasted_iota(jnp.int32, (rows, wb), 1)
        delta = wb + t_row - w
        ok = row_ok & (delta <= win) & ((delta & (dil - 1)) == 0)
        s_b = jnp.where(ok, _dot(q2f.astype(BF16), kt) * scale - slope * delta.astype(F32), NEG_BIG)
        dn = t_row - ucol
        okn = row_ok & col_ok & (dn >= 0) & (dn <= win) & ((dn & (dil - 1)) == 0)
        s_n = jnp.where(okn, _dot_nt(q2f, kn) * scale - slope * dn.astype(F32), NEG_BIG)
        scores.append((s_b, s_n))
    probs = []
    for s_b, s_n in scores:
        m = jnp.maximum(jnp.max(s_b, axis=-1, keepdims=True), jnp.max(s_n, axis=-1, keepdims=True))
        p_b = jnp.exp(s_b - m)
        p_n = jnp.exp(s_n - m)
        l = jnp.sum(p_b, axis=-1, keepdims=True) + jnp.sum(p_n, axis=-1, keepdims=True)
        probs.append((p_b.astype(BF16), p_n, l, m + jnp.log(l)))
    o_u, l_u = {}, {}
    for (j, g), (p_b, p_n, l, lse) in zip(units, probs):
        c = g * slabs + j
        vn = qkv_ref[:, 2 * ATTN_QKV + c * LANES:2 * ATTN_QKV + (c + 1) * LANES]
        vt = caches[g][1, j].astype(BF16)
        o = (_dot_nt(p_b, vt) + _dot(p_n, vn)) / l
        o_u[j, g] = jnp.where(first, o[:SUBLANES], o[SUBLANES:])
        l_u[j, g] = jnp.where(first, lse[:SUBLANES], lse[SUBLANES:])
    out_slabs = []
    for j in range(slabs):
        m = jnp.maximum(jnp.maximum(l_u[j, 0], l_u[j, 1]), l_u[j, 2])
        ws = [jnp.exp(l_u[j, g] - m) for g in range(N_GROUPS)]
        out_slabs.append((ws[0] * o_u[j, 0] + ws[1] * o_u[j, 1] + ws[2] * o_u[j, 2]) / (ws[0] + ws[1] + ws[2]))
    res = jnp.concatenate(out_slabs, axis=1)
    mine = (lax.broadcasted_iota(jnp.int32, (SUBLANES, 1), 0) // dec_seq) == half

    @pl.when(half == 0)
    def _():
        o_ref[...] = res

    @pl.when(half != 0)
    def _():
        o_ref[...] = jnp.where(mine, res, o_ref[...])


def _attn_sample(qkv, caches_t, layer, attn, tp, n_seq, dec_seq):
    assert 2 * dec_seq == SUBLANES
    blk0 = tp // SUBLANES
    in_specs = [
        pl.BlockSpec(memory_space=pltpu.SMEM),
        pl.BlockSpec((SUBLANES, 3 * ATTN_QKV), lambda n: (blk0 + n // 2, 0)),
    ]
    for c in caches_t:
        in_specs.append(pl.BlockSpec((None, None) + c.shape[2:], lambda n, layer=layer: (layer, n, 0, 0, 0, 0)))
    in_specs.append(pl.BlockSpec(memory_space=pl.ANY))
    return pl.pallas_call(
        functools.partial(_attn_sample_kernel, dec_seq=dec_seq),
        grid=(n_seq,),
        in_specs=in_specs,
        out_specs=pl.BlockSpec((SUBLANES, GROUP_W), lambda n: (blk0 + n // 2, 0)),
        out_shape=jax.ShapeDtypeStruct(attn.shape, attn.dtype),
        input_output_aliases={5: 0},
        compiler_params=_cparams("arbitrary"),
        name="attn_sample",
    )(_alibi_slopes(), qkv, *caches_t, attn)


def _log_sigmoid(x):
    return jnp.minimum(x, 0.0) - jnp.log1p(jnp.exp(-jnp.abs(x)))


def _mlstm_chunks(items):
    ln = items[0][0].shape[0]
    ii = lax.broadcasted_iota(jnp.int32, (ln, ln), 0)
    jj = lax.broadcasted_iota(jnp.int32, (ln, ln), 1)
    eye = ii == jj
    causal = jj <= ii
    stage1 = []
    for q, k, v, i_col, lf_col, c_st, n_st, m_st in items:
        lf_row = jnp.sum(jnp.where(eye, lf_col, 0.0), axis=0, keepdims=True)
        i_row = jnp.sum(jnp.where(eye, i_col, 0.0), axis=0, keepdims=True)
        b_col = jnp.sum(jnp.where(causal, lf_row, 0.0), axis=1, keepdims=True)
        b_row = jnp.sum(jnp.where(ii <= jj, lf_col, 0.0), axis=0, keepdims=True)
        dm = jnp.where(causal, b_col - b_row + i_row, -jnp.inf)
        m_inter = b_col + m_st
        m_t = jnp.maximum(m_inter, jnp.max(dm, axis=1, keepdims=True))
        ks = k * (MLSTM_DK ** -0.5)
        qb = q.astype(BF16)
        vb = v.astype(BF16)
        qk = _dot_nt(qb, ks.astype(BF16))
        qc = _dot(qb, c_st.astype(BF16))
        stage1.append((dm, m_inter, m_t, b_col, ks, vb, qk, qc))
    stage2 = []
    for (dm, m_inter, m_t, _, _, vb, qk, _) in stage1:
        w_intra = jnp.exp(dm - m_t) * qk
        stage2.append((w_intra, jnp.exp(m_inter - m_t), _dot(w_intra.astype(BF16), vb)))
    stage3 = []
    for item, s1, (w_intra, w_inter, wv) in zip(items, stage1, stage2):
        q, _, _, i_col, _, c_st, n_st, m_st = item
        _, _, m_t, b_col, ks, vb, _, qc = s1
        num = wv + w_inter * qc
        den = jnp.sum(w_intra, axis=1, keepdims=True) + w_inter * jnp.sum(q * n_st, axis=1, keepdims=True)
        h = num / jnp.maximum(jnp.abs(den), jnp.exp(-m_t))
        b_last = b_col[ln - 1:ln, :]
        w_src = b_last - b_col + i_col
        m_new = jnp.maximum(b_last + m_st, jnp.max(w_src, axis=0, keepdims=True))
        decay = jnp.exp(b_last + m_st - m_new)
        kp = jnp.exp(w_src - m_new) * ks
        stage3.append((h, decay, kp, m_new, _dot_tn(kp.astype(BF16), vb)))
    outs = []
    for item, (h, decay, kp, m_new, kv) in zip(items, stage3):
        c_st, n_st = item[5], item[6]
        outs.append((h, decay * c_st + kv, decay * n_st + jnp.sum(kp, axis=0, keepdims=True), m_new))
    return outs


def _gate_columns(gates, head):
    lane = lax.broadcasted_iota(jnp.int32, gates.shape, 1)
    i_col = jnp.sum(jnp.where(lane == head, gates, 0.0), axis=1, keepdims=True)
    f_col = jnp.sum(jnp.where(lane == head + MLSTM_HEADS, gates, 0.0), axis=1, keepdims=True)
    return i_col, _log_sigmoid(f_col)


def _mlstm_prompt_kernel(ml_ref, bg_ref, gn_ref, prev_ref, hm_ref, c_out, n_out, m_out, c_s, n_s, m_s):
    chunk = pl.program_id(1)

    @pl.when(chunk == 0)
    def _():
        c_s[...] = jnp.zeros_like(c_s)
        n_s[...] = jnp.zeros_like(n_s)
        m_s[...] = jnp.zeros_like(m_s)

    gates = ml_ref[:, ML_W:ML_W + LANES] + bg_ref[...]
    k0 = MLSTM_HEADS * MLSTM_DK
    v0 = 2 * MLSTM_HEADS * MLSTM_DK
    o0 = v0 + MLSTM_HEADS * MLSTM_DV
    items = []
    for head in range(MLSTM_HEADS):
        i_col, lf_col = _gate_columns(gates, head)
        items.append((ml_ref[:, head * MLSTM_DK:(head + 1) * MLSTM_DK],
                      ml_ref[:, k0 + head * MLSTM_DK:k0 + (head + 1) * MLSTM_DK],
                      ml_ref[:, v0 + head * MLSTM_DV:v0 + (head + 1) * MLSTM_DV],
                      i_col, lf_col, c_s[head], n_s[head], m_s[head]))
    for head, (h, c_new, n_new, m_new) in enumerate(_mlstm_chunks(items)):
        c_s[head] = c_new
        n_s[head] = n_new
        m_s[head] = m_new
        cols = slice(head * MLSTM_DV, (head + 1) * MLSTM_DV)
        og = ml_ref[:, o0 + head * MLSTM_DV:o0 + (head + 1) * MLSTM_DV]
        hm_ref[:, cols] = _rms(h) * gn_ref[:, cols] * _sigmoid(og)

    @pl.when(chunk == pl.num_programs(1) - 1)
    def _():
        c_out[...] = c_s[...]
        n_out[...] = n_s[...]
        m_out[...] = m_s[...]


def _mlstm_prompt(ml, b_gates_pad, g_mlstm, hm, n_seq, seq_len):
    ln = MLSTM_CHUNK
    chunks = seq_len // ln
    st = lambda shp: pl.BlockSpec((None, MLSTM_HEADS) + shp, lambda s, c: (s, 0, 0, 0))
    return pl.pallas_call(
        _mlstm_prompt_kernel,
        grid=(n_seq, chunks),
        in_specs=[
            pl.BlockSpec((ln, ML_PAD_W), lambda s, c: (s * chunks + c, 0)),
            pl.BlockSpec((1, LANES), lambda s, c: (0, 0)),
            pl.BlockSpec((1, MLSTM_HEADS * MLSTM_DV), lambda s, c: (0, 0)),
            pl.BlockSpec(memory_space=pl.ANY),
        ],
        out_specs=[
            pl.BlockSpec((ln, MLSTM_HEADS * MLSTM_DV), lambda s, c: (s * chunks + c, 0)),
            st((MLSTM_DK, MLSTM_DV)), st((1, MLSTM_DK)), st((1, 1)),
        ],
        out_shape=[
            jax.ShapeDtypeStruct(hm.shape, hm.dtype),
            jax.ShapeDtypeStruct((n_seq, MLSTM_HEADS, MLSTM_DK, MLSTM_DV), F32),
            jax.ShapeDtypeStruct((n_seq, MLSTM_HEADS, 1, MLSTM_DK), F32),
            jax.ShapeDtypeStruct((n_seq, MLSTM_HEADS, 1, 1), F32),
        ],
        scratch_shapes=[pltpu.VMEM((MLSTM_HEADS, MLSTM_DK, MLSTM_DV), F32),
                        pltpu.VMEM((MLSTM_HEADS, 1, MLSTM_DK), F32),
                        pltpu.VMEM((MLSTM_HEADS, 1, 1), F32)],
        input_output_aliases={3: 0},
        compiler_params=_cparams("parallel", "arbitrary"),
        name="mlstm_prompt",
    )(ml, b_gates_pad, g_mlstm.reshape(1, -1), hm)


def _mlstm_sample_kernel(ml_ref, bg_ref, gn_ref, c0_ref, n0_ref, m0_ref, prev_ref, cprev_ref,
                         hm_ref, c_out, n_out, m_out, *, seq_rows):
    ln = ml_ref.shape[0]
    row = lax.broadcasted_iota(jnp.int32, (ln, 1), 0)
    gates = ml_ref[:, ML_W:ML_W + LANES] + bg_ref[...]
    k0 = MLSTM_HEADS * MLSTM_DK
    v0 = 2 * MLSTM_HEADS * MLSTM_DK
    o0 = v0 + MLSTM_HEADS * MLSTM_DV
    per_blk = ln // seq_rows
    items = []
    for head in range(MLSTM_HEADS):
        q = ml_ref[:, head * MLSTM_DK:(head + 1) * MLSTM_DK]
        k = ml_ref[:, k0 + head * MLSTM_DK:k0 + (head + 1) * MLSTM_DK]
        v = ml_ref[:, v0 + head * MLSTM_DV:v0 + (head + 1) * MLSTM_DV]
        i_col, lf_col = _gate_columns(gates, head)
        for j in range(per_blk):
            mine = (row // seq_rows) == j
            items.append((q, k, v, jnp.where(mine, i_col, NEG_BIG), jnp.where(mine, lf_col, 0.0),
                          c0_ref[j, head], n0_ref[j, head], m0_ref[j, head]))
    outs = _mlstm_chunks(items)
    for head in range(MLSTM_HEADS):
        h_all = None
        for j in range(per_blk):
            h, c_new, n_new, m_new = outs[head * per_blk + j]
            c_out[j, head] = c_new
            n_out[j, head] = n_new
            m_out[j, head] = m_new
            h_all = h if h_all is None else jnp.where((row // seq_rows) == j, h, h_all)
        cols = slice(head * MLSTM_DV, (head + 1) * MLSTM_DV)
        og = ml_ref[:, o0 + head * MLSTM_DV:o0 + (head + 1) * MLSTM_DV]
        hm_ref[:, cols] = _rms(h_all) * gn_ref[:, cols] * _sigmoid(og)


def _mlstm_sample(ml, b_gates_pad, g_mlstm, c0, n0, m0, layer, hm, c_new_all, row0, n_seq, seq_len):
    per_blk = SUBLANES // seq_len
    assert per_blk * seq_len == SUBLANES and n_seq % per_blk == 0 and row0 % SUBLANES == 0
    blk0 = row0 // SUBLANES
    st_in = lambda shp: pl.BlockSpec((None, per_blk, MLSTM_HEADS) + shp, lambda s: (layer, s, 0, 0, 0))
    st_out = lambda shp: pl.BlockSpec((per_blk, MLSTM_HEADS) + shp, lambda s: (s, 0, 0, 0))
    return pl.pallas_call(
        functools.partial(_mlstm_sample_kernel, seq_rows=seq_len),
        grid=(n_seq // per_blk,),
        in_specs=[
            pl.BlockSpec((SUBLANES, ML_PAD_W), lambda s: (blk0 + s, 0)),
            pl.BlockSpec((1, LANES), lambda s: (0, 0)),
            pl.BlockSpec((1, MLSTM_HEADS * MLSTM_DV), lambda s: (0, 0)),
            st_in((MLSTM_DK, MLSTM_DV)), st_in((1, MLSTM_DK)), st_in((1, 1)),
            pl.BlockSpec(memory_space=pl.ANY),
            pl.BlockSpec(memory_space=pl.ANY),
        ],
        out_specs=[
            pl.BlockSpec((SUBLANES, MLSTM_HEADS * MLSTM_DV), lambda s: (blk0 + s, 0)),
            st_in((MLSTM_DK, MLSTM_DV)), st_out((1, MLSTM_DK)), st_out((1, 1)),
        ],
        out_shape=[
            jax.ShapeDtypeStruct(hm.shape, hm.dtype),
            jax.ShapeDtypeStruct(c_new_all.shape, F32),
            jax.ShapeDtypeStruct(n0.shape[1:], F32),
            jax.ShapeDtypeStruct(m0.shape[1:], F32),
        ],
        input_output_aliases={6: 0, 7: 1},
        compiler_params=_cparams("parallel"),
        name="mlstm_sample",
    )(ml, b_gates_pad, g_mlstm.reshape(1, -1), c0, n0, m0, hm, c_new_all)


def _postmix_kernel(attn_ref, hm_ref, gt_ref, x_ref, gtp_ref, shp_ref, scp_ref, gts_ref, shs_ref, scs_ref,
                    gf_ref, wa_ref, wm_ref, wo_ref, wr_ref, br_ref, x1_ref, h2_ref, gw_ref, ei_ref, rk_ref, cnt_ref,
                    cnt_s, *, n_prompt_tiles):
    is_sample = pl.program_id(0) >= n_prompt_tiles
    d = x_ref.shape[1]
    tm = x_ref.shape[0]
    pr = tm // POSTMIX_PARTS
    parts = [slice(p * pr, (p + 1) * pr) for p in range(POSTMIX_PARTS)]

    def mod(ref_s, ref_p, rows):
        return jnp.where(is_sample, ref_s[rows], ref_p[...])

    @pl.when(pl.program_id(0) == 0)
    def _():
        cnt_s[...] = jnp.zeros_like(cnt_s)

    branch = [(_dot(attn_ref[rows].astype(BF16), wa_ref[...]), _dot(hm_ref[rows].astype(BF16), wm_ref[...]))
              for rows in parts]
    proj = []
    for rows, (y_attn, y_mlstm) in zip(parts, branch):
        merged = _sigmoid(gt_ref[rows, :d]) * y_attn + _sigmoid(gt_ref[rows, d:]) * y_mlstm
        proj.append(_dot(merged.astype(BF16), wo_ref[...]))
    lane = lax.broadcasted_iota(jnp.int32, (pr, LANES), 1)
    logits = []
    for rows, z in zip(parts, proj):
        x1 = x_ref[rows] + mod(gts_ref, gtp_ref, rows) * z
        x1_ref[rows] = x1
        h2 = (_rms(x1) * gf_ref[...] * (1.0 + mod(scs_ref, scp_ref, rows)) + mod(shs_ref, shp_ref, rows)).astype(BF16)
        bits = lax.bitcast_convert_type(h2.astype(F32), jnp.int32)
        h2_ref[rows] = lax.shift_right_logical(bits[:, :d // 2], 16) | bits[:, d // 2:]
        logits.append(jnp.where(lane < N_EXPERTS, _dot(h2, wr_ref[...]) + br_ref[...], -jnp.inf))

    ri = lax.broadcasted_iota(jnp.int32, (pr, pr), 0)
    ci = lax.broadcasted_iota(jnp.int32, (pr, pr), 1)
    before = jnp.where(ci < ri, 1.0, 0.0).astype(BF16)
    picks = []
    for lg in logits:
        vals, idxs = [], []
        chosen = jnp.zeros((pr, LANES), F32)
        for _ in range(TOP_K):
            m = jnp.max(lg, axis=-1, keepdims=True)
            idx = jnp.min(jnp.where(lg == m, lane, LANES), axis=-1, keepdims=True)
            hit = lane == idx
            vals.append(m)
            idxs.append(idx)
            chosen = jnp.where(hit, 1.0, chosen)
            lg = jnp.where(hit, -jnp.inf, lg)
        picks.append((vals, idxs, chosen, _dot(before, chosen.astype(BF16))))
    total = cnt_s[...]
    for rows, (vals, idxs, chosen, within) in zip(parts, picks):
        prefix = within + total
        ex = [jnp.exp(v - vals[0]) for v in vals]
        den = ex[0] + ex[1] + ex[2] + ex[3]
        for k in range(TOP_K):
            gw_ref[rows, k:k + 1] = ex[k] / den
            ei_ref[rows, k:k + 1] = idxs[k]
            rank = jnp.sum(jnp.where(lane == idxs[k], prefix, 0.0), axis=-1, keepdims=True)
            rk_ref[rows, k:k + 1] = rank.astype(jnp.int32)
        total = total + jnp.sum(chosen, axis=0, keepdims=True)
    cnt_s[...] = total
    cnt_ref[...] = total.astype(jnp.int32)


def _postmix(attn, hm, gt, x_all, mod_p, mod_s, layer, g_ffn, wa, wm, wo, wr, br, tp, n_batch):
    t_all, d = x_all.shape
    tm = ROW_TILE
    n_prompt_tiles = tp // tm
    tiles_per_batch = n_prompt_tiles // n_batch
    specs = [_mod_specs(c, tm, n_prompt_tiles, tiles_per_batch, n_batch, layer) for c in (2, 3, 4)]
    row = lambda w: pl.BlockSpec((tm, w), lambda i: (i, 0))
    full = lambda a: pl.BlockSpec(a.shape, lambda i: (0,) * a.ndim)
    gf = g_ffn.reshape(1, d)
    return pl.pallas_call(
        functools.partial(_postmix_kernel, n_prompt_tiles=n_prompt_tiles),
        grid=(t_all // tm,),
        in_specs=[row(GROUP_W), row(d), row(GATE_W), row(d),
                  specs[0][0], specs[1][0], specs[2][0], specs[0][1], specs[1][1], specs[2][1],
                  full(gf), full(wa), full(wm), full(wo), full(wr), full(br)],
        out_specs=[row(d), row(d // 2), row(TOP_K), row(TOP_K), row(TOP_K),
                   pl.BlockSpec((1, LANES), lambda i: (0, 0))],
        out_shape=[jax.ShapeDtypeStruct((t_all, d), F32), jax.ShapeDtypeStruct((t_all, d // 2), jnp.int32),
                   jax.ShapeDtypeStruct((t_all, TOP_K), F32), jax.ShapeDtypeStruct((t_all, TOP_K), jnp.int32),
                   jax.ShapeDtypeStruct((t_all, TOP_K), jnp.int32), jax.ShapeDtypeStruct((1, LANES), jnp.int32)],
        scratch_shapes=[pltpu.VMEM((1, LANES), F32)],
        compiler_params=_cparams("arbitrary"),
        name="postmix",
    )(attn, hm, gt, x_all, mod_p, mod_p, mod_p, mod_s, mod_s, mod_s, gf, wa, wm, wo, wr, br)


def _expert_kernel(be_ref, ne_ref, nused_ref, rows_ref, wgu_hbm, bgu_ref, wd_hbm, bd_ref, after_ref, y_ref,
                   wgu_f, wd_f, wgu_s, wd_s, sem, *, layer):
    i = pl.program_id(0)
    e = be_ref[i]
    prev = be_ref[jnp.maximum(i - 1, 0)]

    def weight_copies(expert):
        return (pltpu.make_async_copy(wgu_hbm.at[layer, expert], wgu_f, sem.at[0]),
                pltpu.make_async_copy(wd_hbm.at[layer, expert], wd_f, sem.at[1]))

    @pl.when(i == 0)
    def _():
        for cp in weight_copies(e):
            cp.start()

    @pl.when((i == 0) | (e != prev))
    def _():
        for cp in weight_copies(e):
            cp.wait()
        wgu_s[...] = wgu_f[...].astype(BF16)
        wd_s[...] = wd_f[...].astype(BF16)
        nxt = ne_ref[i]

        @pl.when(nxt >= 0)
        def _():
            for cp in weight_copies(nxt):
                cp.start()

    @pl.when(i < nused_ref[0])
    def _():
        de = wd_s.shape[0]
        words = rows_ref[...]
        lo = lax.bitcast_convert_type(lax.shift_left(words, 16), F32).astype(BF16)
        hi = lax.bitcast_convert_type(words & jnp.int32(-65536), F32).astype(BF16)
        rows = jnp.concatenate([lo, hi], axis=1)
        gu = _dot(rows, wgu_s[...]) + bgu_ref[...]
        gate = jnp.minimum(gu[:, :de], SWIGLU_LIMIT)
        up = jnp.clip(gu[:, de:], -SWIGLU_LIMIT, SWIGLU_LIMIT)
        act = (up + 1.0) * gate * _sigmoid(SWIGLU_ALPHA * gate)
        y_ref[...] = _dot(act.astype(BF16), wd_s[...]) + bd_ref[...]

    @pl.when(i >= nused_ref[0])
    def _():
        y_ref[...] = jnp.zeros_like(y_ref)


def _experts(block_expert, next_expert, n_used, rows, w_gu, b_gu, w_down, b_down, layer, after):
    nrows = rows.shape[0]
    n_blocks = nrows // MOE_BLOCK
    d, de = w_down.shape[3], w_down.shape[2]
    grid_spec = pltpu.PrefetchScalarGridSpec(
        num_scalar_prefetch=3,
        grid=(n_blocks,),
        in_specs=[
            pl.BlockSpec((MOE_BLOCK, d // 2), lambda i, be, ne, nu: (i, 0)),
            pl.BlockSpec(memory_space=pl.ANY),
            pl.BlockSpec((None, None, 1, 2 * de), lambda i, be, ne, nu: (layer, be[i], 0, 0)),
            pl.BlockSpec(memory_space=pl.ANY),
            pl.BlockSpec((None, None, 1, d), lambda i, be, ne, nu: (layer, be[i], 0, 0)),
            pl.BlockSpec(memory_space=pl.ANY),
        ],
        out_specs=pl.BlockSpec((MOE_BLOCK, d), lambda i, be, ne, nu: (i, 0)),
        scratch_shapes=[pltpu.VMEM((d, 2 * de), F32), pltpu.VMEM((de, d), F32),
                        pltpu.VMEM((d, 2 * de), BF16), pltpu.VMEM((de, d), BF16),
                        pltpu.SemaphoreType.DMA((2,))],
    )
    return pl.pallas_call(
        functools.partial(_expert_kernel, layer=layer),
        grid_spec=grid_spec,
        out_shape=jax.ShapeDtypeStruct((nrows, d), F32),
        compiler_params=_cparams("arbitrary"),
        name="experts",
    )(block_expert, next_expert, n_used, rows, w_gu, b_gu.reshape(b_gu.shape[0], b_gu.shape[1], 1, -1),
      w_down, b_down.reshape(b_down.shape[0], b_down.shape[1], 1, -1), after)


def _sc_mesh():
    return plsc.VectorSubcoreMesh(core_axis_name="core", subcore_axis_name="subcore")


def _sc_scatter_rows(x, pos, n_rows):
    t, c = x.shape
    nk = pos.shape[0]
    nwin = t // SC_SCATTER_WIN
    assert nwin * SC_SCATTER_WIN == t
    idx = pos.reshape(nk * nwin, SC_SCATTER_WIN)

    @functools.partial(pl.kernel, out_type=jax.ShapeDtypeStruct((n_rows, c), x.dtype), mesh=_sc_mesh(),
                       scratch_types=[], name="sc_scatter_rows")
    def kern(x_hbm, i_hbm, o_hbm):
        def body(x_vmem, i_vmem):
            pltpu.sync_copy(x_vmem, o_hbm.at[i_vmem.at[0]])

        pltpu.emit_pipeline(
            body,
            grid=(nk * nwin,),
            in_specs=[pl.BlockSpec((SC_SCATTER_WIN, c), lambda i: (i % nwin, 0)),
                      pl.BlockSpec((1, SC_SCATTER_WIN), lambda i: (i, 0))],
            out_specs=[],
            core_axis_name=("core", "subcore"),
            dimension_semantics=(pltpu.PARALLEL,),
        )(x_hbm, i_hbm)

    return kern(x, idx)


def _sc_gather_rows(y, pos):
    nk, t = pos.shape
    c = y.shape[1]
    n = nk * t
    assert n % SC_GATHER_WIN == 0
    idx = pos.reshape(n // SC_GATHER_WIN, SC_GATHER_WIN)

    @functools.partial(pl.kernel, out_type=jax.ShapeDtypeStruct((n, c), y.dtype), mesh=_sc_mesh(),
                       scratch_types=[], name="sc_gather_rows")
    def kern(y_hbm, i_hbm, o_hbm):
        def body(i_vmem, o_vmem):
            pltpu.sync_copy(y_hbm.at[i_vmem.at[0]], o_vmem)

        pltpu.emit_pipeline(
            body,
            grid=(n // SC_GATHER_WIN,),
            in_specs=[pl.BlockSpec((1, SC_GATHER_WIN), lambda i: (i, 0))],
            out_specs=[pl.BlockSpec((SC_GATHER_WIN, c), lambda i: (i, 0))],
            core_axis_name=("core", "subcore"),
            dimension_semantics=(pltpu.PARALLEL,),
        )(i_hbm, o_hbm)

    return kern(y, idx).reshape(nk, t, c)


def _combine_kernel(yg_ref, gw_ref, x1_ref, gtp_ref, gts_ref, gfin_ref, after_ref, o_ref, *, n_prompt_tiles, final):
    is_sample = pl.program_id(0) >= n_prompt_tiles
    gate2 = jnp.where(is_sample, gts_ref[...], gtp_ref[...])
    gw = gw_ref[...]
    acc = gw[:, 0:1] * yg_ref[0]
    for k in range(1, TOP_K):
        acc = acc + gw[:, k:k + 1] * yg_ref[k]
    x2 = x1_ref[...] + gate2 * acc
    o_ref[...] = _rms(x2) * gfin_ref[...] if final else x2


def _combine(yg, gates, x1, mod_p, mod_s, layer, g_final, tp, n_batch, final, after):
    t_all, d = x1.shape
    tm = ROW_TILE
    n_prompt_tiles = tp // tm
    tiles_per_batch = n_prompt_tiles // n_batch
    gtp, gts = _mod_specs(5, tm, n_prompt_tiles, tiles_per_batch, n_batch, layer)
    row = lambda w: pl.BlockSpec((tm, w), lambda i: (i, 0))
    return pl.pallas_call(
        functools.partial(_combine_kernel, n_prompt_tiles=n_prompt_tiles, final=final),
        grid=(t_all // tm,),
        in_specs=[pl.BlockSpec((TOP_K, tm, d), lambda i: (0, i, 0)), row(TOP_K), row(d), gtp, gts,
                  pl.BlockSpec((1, d), lambda i: (0, 0)), pl.BlockSpec(memory_space=pl.ANY)],
        out_specs=row(d),
        out_shape=jax.ShapeDtypeStruct((t_all, d), F32),
        compiler_params=_cparams("parallel"),
        name="moe_combine",
    )(yg, gates, x1, mod_p, mod_s, g_final.reshape(1, d), after)


def _positions(eidx, rank, counts):
    t = eidx.shape[0]
    n_blocks = t * TOP_K // MOE_BLOCK + N_EXPERTS
    padded = (counts + MOE_BLOCK - 1) // MOE_BLOCK * MOE_BLOCK
    pad_end = jnp.cumsum(padded)
    pad_start = pad_end - padded
    experts = jnp.arange(N_EXPERTS, dtype=jnp.int32)
    start = jnp.sum(jnp.where(eidx[:, :, None] == experts, pad_start, 0), axis=-1)
    pos = (start + rank).T.astype(jnp.int32)
    n_used = (pad_end[-1] // MOE_BLOCK).astype(jnp.int32)
    blk = jnp.minimum(jnp.arange(n_blocks, dtype=jnp.int32), n_used - 1)
    block_expert = jnp.sum((pad_end[None, :] <= (blk * MOE_BLOCK)[:, None]).astype(jnp.int32), axis=1)
    block_expert = jnp.minimum(block_expert, N_EXPERTS - 1).astype(jnp.int32)
    later_used = (counts[None, :] > 0) & (experts[None, :] > experts[:, None])
    nxt = jnp.min(jnp.where(later_used, experts[None, :], N_EXPERTS), axis=1)
    nxt = jnp.where(nxt < N_EXPERTS, nxt, -1)
    next_expert = jnp.sum(jnp.where(block_expert[:, None] == experts[None, :], nxt[None, :], 0), axis=1)
    return pos, block_expert, next_expert.astype(jnp.int32), n_used.reshape(1), n_blocks


def kernel(x_prompt, x_sample, cache_kv_w128, cache_kv_w512, cache_kv_w2048, state_mlstm_C, state_mlstm_n,
           state_mlstm_m, c_prompt, c_sample, w_ada, b_ada, g_mix, g_ffn, w_in, b_gates, g_mlstm, w_br_attn,
           w_br_mlstm, w_out, w_router, b_router, w_gu, b_gu, w_down, b_down, g_final):
    n_batch, seq, d = x_prompt.shape
    n_seq, dec_seq, _ = x_sample.shape
    depth = w_ada.shape[0]
    tp, ts = n_batch * seq, n_seq * dec_seq
    t_all = tp + ts

    x_all = jnp.concatenate([x_prompt.reshape(tp, d), x_sample.reshape(ts, d)], axis=0)
    mod_p = _ada_mod(c_prompt, w_ada, b_ada).reshape(depth, n_batch, 1, 6 * d)
    mod_s = _ada_mod(jnp.repeat(c_sample, dec_seq, axis=0), w_ada, b_ada)
    caches_t = [jnp.transpose(c, (0, 1, 3, 4, 5, 2)).reshape(c.shape[0], c.shape[1], 2, HEADS // 2, 2 * HEAD_DIM, c.shape[2])
                for c in (cache_kv_w128, cache_kv_w512, cache_kv_w2048)]
    sn = state_mlstm_n.reshape(depth, n_seq, MLSTM_HEADS, 1, MLSTM_DK)
    sm = state_mlstm_m.reshape(depth, n_seq, MLSTM_HEADS, 1, 1)
    wr_pad = jnp.pad(w_router, ((0, 0), (0, 0), (0, LANES - N_EXPERTS))).astype(BF16)
    br_pad = jnp.pad(b_router, ((0, 0), (0, LANES - N_EXPERTS))).reshape(depth, 1, LANES)
    bg_pad = jnp.pad(b_gates, ((0, 0), (0, LANES - b_gates.shape[1]))).reshape(depth, 1, LANES)

    kvs = [[] for _ in range(N_GROUPS)]
    mp = [[], [], []]
    ms = [[], []]
    attn = jnp.zeros((t_all, GROUP_W), F32)
    hm = jnp.zeros((t_all, MLSTM_HEADS * MLSTM_DV), F32)
    kvp = [jnp.zeros((depth, n_batch, 2, HEADS // 2, 2 * HEAD_DIM, min(win, seq)), F32) for win, _ in ATTN_GROUPS]
    c_s = jnp.zeros(state_mlstm_C.shape, F32)
    for l in range(depth):
        qkv, ml, gt = _inproj(x_all, g_mix[l], mod_p, mod_s, l, _split_w_in(w_in[l]), tp, n_batch)

        attn = _attn_prompt(qkv, attn, n_batch, seq)
        attn = _attn_sample(qkv, caches_t, l, attn, tp, n_seq, dec_seq)

        hm, c_p, n_p, m_p = _mlstm_prompt(ml, bg_pad[l], g_mlstm[l], hm, n_batch, seq)
        hm, c_s, n_s, m_s = _mlstm_sample(ml, bg_pad[l], g_mlstm[l], state_mlstm_C, sn, sm, l, hm, c_s,
                                          tp, n_seq, dec_seq)

        x1, h2w, gates, eidx, rank, counts = _postmix(
            attn, hm, gt, x_all, mod_p, mod_s, l, g_ffn[l], w_br_attn[l].astype(BF16), w_br_mlstm[l].astype(BF16),
            w_out[l].astype(BF16), wr_pad[l], br_pad[l], tp, n_batch)

        pos, block_expert, next_expert, n_used, n_blocks = _positions(eidx, rank, counts[0, :N_EXPERTS])
        rows = _sc_scatter_rows(h2w, pos, n_blocks * MOE_BLOCK)
        kvp[2] = _kv_prompt(qkv, 2, seq, kvp[2], l, after=x1)
        y = _experts(block_expert, next_expert, n_used, rows, w_gu, b_gu, w_down, b_down, l, after=kvp[2])
        yg = _sc_gather_rows(y, pos)
        kv_s = _kv_sample(qkv, tp, n_seq, dec_seq, after=y)
        kvp[1] = _kv_prompt(qkv, 1, seq, kvp[1], l, after=kv_s)
        kvp[0] = _kv_prompt(qkv, 0, seq, kvp[0], l, after=kvp[1])
        for g in range(N_GROUPS):
            kvs[g].append(kv_s[g])
        x_all = _combine(yg, gates, x1, mod_p, mod_s, l, g_final, tp, n_batch, final=(l == depth - 1), after=kvp[0])
        for lst, val in zip(mp, (c_p, n_p.reshape(n_batch, MLSTM_HEADS, MLSTM_DK), m_p.reshape(n_batch, MLSTM_HEADS))):
            lst.append(val)
        for lst, val in zip(ms, (n_s.reshape(n_seq, MLSTM_HEADS, MLSTM_DK), m_s.reshape(n_seq, MLSTM_HEADS))):
            lst.append(val)

    y_prompt = x_all[:tp].reshape(n_batch, seq, d)
    y_sample = x_all[tp:].reshape(n_seq, dec_seq, d)

    def kv_prompt_out(a):
        a = a.reshape(depth, n_batch, 2, HEADS, HEAD_DIM, a.shape[-1])
        return jnp.transpose(a, (0, 1, 5, 2, 3, 4))

    def kv_sample_out(parts):
        a = jnp.stack(parts).reshape(depth, dec_seq, 2, HEADS, HEAD_DIM, n_seq)
        return jnp.transpose(a, (0, 5, 1, 2, 3, 4))

    return (y_prompt, y_sample,
            kv_prompt_out(kvp[0]), kv_prompt_out(kvp[1]), kv_prompt_out(kvp[2]),
            jnp.stack(mp[0]), jnp.stack(mp[1]), jnp.stack(mp[2]),
            kv_sample_out(kvs[0]), kv_sample_out(kvs[1]), kv_sample_out(kvs[2]),
            c_s, jnp.stack(ms[0]), jnp.stack(ms[1]))
```

```python
import functools

import jax
import jax.numpy as jnp
import numpy as np
from jax import lax
from jax.experimental import pallas as pl
from jax.experimental.pallas import tpu as pltpu
from jax.experimental.pallas import tpu_sc as plsc

F32 = jnp.float32
BF16 = jnp.bfloat16

ATTN_GROUPS = ((128, 1), (512, 4), (2048, 16))
N_GROUPS = len(ATTN_GROUPS)
HEADS = 8
HEAD_DIM = 64
ATTN_BLOCK = 128
GROUP_W = HEADS * HEAD_DIM
ATTN_QKV = N_GROUPS * GROUP_W
MLSTM_HEADS = 4
MLSTM_DK = 128
MLSTM_DV = 256
N_EXPERTS = 32
TOP_K = 4
SWIGLU_LIMIT = 7.0
SWIGLU_ALPHA = 1.702
RMS_EPS = 1e-6
NEG_BIG = -1e30

LANES = 128
SUBLANES = 8
VMEM_LIMIT_BYTES = 56 * 1024 * 1024

ROW_TILE = 256
ATTN_TILE = 2048
ATTN_UNROLL = 8
MLSTM_CHUNK = 256
MOE_BLOCK = 512
POSTMIX_PARTS = 2
SC_SCATTER_WIN = 48
SC_GATHER_WIN = 48

ML_W = 2 * MLSTM_HEADS * MLSTM_DK + 2 * MLSTM_HEADS * MLSTM_DV
ML_PAD_W = ML_W + LANES
GATE_W = 2048


def _cparams(*sem):
    return pltpu.CompilerParams(dimension_semantics=sem, vmem_limit_bytes=VMEM_LIMIT_BYTES)


def _dot(a, b):
    return jnp.dot(a, b, preferred_element_type=F32)


def _dot_nt(a, b):
    return lax.dot_general(a, b, (((1,), (1,)), ((), ())), preferred_element_type=F32)


def _dot_tn(a, b):
    return lax.dot_general(a, b, (((0,), (0,)), ((), ())), preferred_element_type=F32)


def _ada_kernel(c_ref, w_ref, b_ref, o_ref):
    c = c_ref[...]
    s = c * jax.nn.sigmoid(c)
    o_ref[...] = _dot(s.astype(BF16), w_ref[...].astype(BF16)) + b_ref[...]


def _ada_mod(c_all, w_ada, b_ada):
    depth, d, n = w_ada.shape
    bc = c_all.shape[0]
    tn = 1024
    return pl.pallas_call(
        _ada_kernel,
        grid=(depth, n // tn),
        in_specs=[
            pl.BlockSpec((bc, d), lambda l, j: (0, 0)),
            pl.BlockSpec((None, d, tn), lambda l, j: (l, 0, j)),
            pl.BlockSpec((None, 1, tn), lambda l, j: (l, 0, j)),
        ],
        out_specs=pl.BlockSpec((None, bc, tn), lambda l, j: (l, 0, j)),
        out_shape=jax.ShapeDtypeStruct((depth, bc, n), F32),
        compiler_params=_cparams("parallel", "parallel"),
        name="ada_mod",
    )(c_all, w_ada, b_ada.reshape(depth, 1, n))


def _mod_specs(chunk, tm, n_prompt_tiles, tiles_per_batch, n_batch, layer):
    d = 1024
    sp = pl.BlockSpec((None, None, 1, d),
                      lambda i: (layer, jnp.minimum(i // tiles_per_batch, n_batch - 1), 0, chunk))
    ss = pl.BlockSpec((None, tm, d), lambda i: (layer, jnp.maximum(i - n_prompt_tiles, 0), chunk))
    return sp, ss


def _sigmoid(x):
    return 0.5 * jnp.tanh(0.5 * x) + 0.5


def _rms(x):
    return x * lax.rsqrt(jnp.mean(x * x, axis=-1, keepdims=True) + RMS_EPS)


def _inproj_kernel(x_ref, g_ref, shp_ref, scp_ref, shs_ref, scs_ref, wa_ref, wif_ref, wg_ref,
                   qkv_ref, ml_ref, gt_ref, *, n_prompt_tiles):
    is_sample = pl.program_id(0) >= n_prompt_tiles
    sh = jnp.where(is_sample, shs_ref[...], shp_ref[...])
    sc = jnp.where(is_sample, scs_ref[...], scp_ref[...])
    h = (_rms(x_ref[...]) * g_ref[...] * (1.0 + sc) + sh).astype(BF16)
    cw = 512
    a = 3 * ATTN_QKV
    for c0 in range(0, a, cw):
        qkv_ref[:, c0:c0 + cw] = _dot(h, wa_ref[:, c0:c0 + cw])
    for c0 in range(0, ML_W, cw):
        ml_ref[:, c0:c0 + cw] = _dot(h, wa_ref[:, a + c0:a + c0 + cw])
    ml_ref[:, ML_W:] = _dot(h, wif_ref[...])
    for c0 in range(0, GATE_W, cw):
        gt_ref[:, c0:c0 + cw] = _dot(h, wg_ref[:, c0:c0 + cw])


def _split_w_in(w_in_l):
    a = 3 * ATTN_QKV + ML_W
    if_w = 2 * MLSTM_HEADS
    w_if = jnp.pad(w_in_l[:, a:a + if_w], ((0, 0), (0, LANES - if_w)))
    return w_in_l[:, :a].astype(BF16), w_if.astype(BF16), w_in_l[:, a + if_w:].astype(BF16)


def _inproj(x_all, g_mix, mod_p, mod_s, layer, w_parts, tp, n_batch):
    t_all, d = x_all.shape
    tm = ROW_TILE
    n_prompt_tiles = tp // tm
    tiles_per_batch = n_prompt_tiles // n_batch
    shp, shs = _mod_specs(0, tm, n_prompt_tiles, tiles_per_batch, n_batch, layer)
    scp, scs = _mod_specs(1, tm, n_prompt_tiles, tiles_per_batch, n_batch, layer)
    row = lambda w: pl.BlockSpec((tm, w), lambda i: (i, 0))
    return pl.pallas_call(
        functools.partial(_inproj_kernel, n_prompt_tiles=n_prompt_tiles),
        grid=(t_all // tm,),
        in_specs=[
            row(d),
            pl.BlockSpec((1, d), lambda i: (0, 0)),
            shp, scp, shs, scs,
        ] + [pl.BlockSpec(w.shape, lambda i: (0, 0), pipeline_mode=pl.Buffered(1)) for w in w_parts],
        out_specs=[row(3 * ATTN_QKV), row(ML_PAD_W), row(GATE_W)],
        out_shape=[
            jax.ShapeDtypeStruct((t_all, 3 * ATTN_QKV), F32),
            jax.ShapeDtypeStruct((t_all, ML_PAD_W), F32),
            jax.ShapeDtypeStruct((t_all, GATE_W), F32),
        ],
        compiler_params=_cparams("parallel"),
        name="inproj",
    )(x_all, g_mix.reshape(1, d), mod_p, mod_p, mod_s, mod_s, *w_parts)


def _attn_bias(slope, dil, valid_prev):
    qi = lax.broadcasted_iota(jnp.int32, (ATTN_BLOCK, 2 * ATTN_BLOCK), 0)
    ki = lax.broadcasted_iota(jnp.int32, (ATTN_BLOCK, 2 * ATTN_BLOCK), 1)
    dist = qi + ATTN_BLOCK - ki
    keep = (dist >= 0) & (dist <= ATTN_BLOCK) & ((ki >= ATTN_BLOCK) | valid_prev)
    return jnp.where(keep, -slope * (dil * dist).astype(F32), NEG_BIG)


def _attn_units(blocks):
    lane = lax.broadcasted_iota(jnp.int32, (ATTN_BLOCK, LANES), 1)
    first = lane < HEAD_DIM
    scores, values = [], []
    for q, k2, v2, bias_ref in blocks:
        k2b = k2.astype(BF16)
        qs = q * (HEAD_DIM ** -0.5)
        for e in range(2):
            qh = jnp.where(first if e == 0 else ~first, qs, 0.0).astype(BF16)
            scores.append(_dot_nt(qh, k2b) + bias_ref[e])
        values.append(v2.astype(BF16))
    probs = []
    for s in scores:
        m = jnp.max(s, axis=-1, keepdims=True)
        p = jnp.exp(s - m)
        l = jnp.sum(p, axis=-1, keepdims=True)
        probs.append((p.astype(BF16), l, m + jnp.log(l)))
    outs = []
    for u, v2b in enumerate(values):
        (p0, l0, lse0), (p1, l1, lse1) = probs[2 * u], probs[2 * u + 1]
        o0 = _dot(p0, v2b) / l0
        o1 = _dot(p1, v2b) / l1
        outs.append((jnp.where(first, o0, o1), jnp.where(first, lse0, lse1)))
    return outs


def _attn_prompt_kernel(slope_ref, *refs):
    ins, o_ref, o_s, l_s, bias_s, bias_first_s, tmp_s, dint_s = refs[:15], *refs[16:23]
    hp = pl.program_id(1)
    tile = pl.program_id(2)
    for g, (_, dil) in enumerate(ATTN_GROUPS):
        q_ref, kc_ref, vc_ref, kp_ref, vp_ref = ins[5 * g:5 * g + 5]
        sub = ATTN_BLOCK * dil
        n_first = dil
        n_units = ATTN_TILE // ATTN_BLOCK
        for e in range(2):
            bias_s[e] = _attn_bias(slope_ref[2 * hp + e], dil, True)
            bias_first_s[e] = _attn_bias(slope_ref[2 * hp + e], dil, tile > 0)

        def strided(ref, start, size):
            return ref[pl.ds(start, size, stride=dil), :] if dil > 1 else ref[pl.ds(start, size), :]

        def store(start, o, lse):
            if dil > 1:
                o_s[g, pl.ds(start, ATTN_BLOCK, stride=dil), :] = o
                l_s[g, pl.ds(start, ATTN_BLOCK, stride=dil), :] = lse
            else:
                o_s[g, pl.ds(start, ATTN_BLOCK), :] = o
                l_s[g, pl.ds(start, ATTN_BLOCK), :] = lse

        split16 = dil == 16 and sub == ATTN_TILE
        if split16:
            for i, src in enumerate((q_ref, kc_ref, vc_ref, kp_ref, vp_ref)):
                for a in range(4):
                    tmp_s[i, a] = src[pl.ds(a, ATTN_TILE // 4, stride=4), :]
                for a in range(4):
                    for b in range(4):
                        dint_s[i, a + 4 * b] = tmp_s[i, a, pl.ds(b, ATTN_BLOCK, stride=4), :]

        def first_block(r):
            if split16:
                k2 = jnp.concatenate([dint_s[3, r], dint_s[1, r]], axis=0)
                v2 = jnp.concatenate([dint_s[4, r], dint_s[2, r]], axis=0)
                return r, (dint_s[0, r], k2, v2, bias_first_s)
            k2 = jnp.concatenate([strided(kp_ref, r, ATTN_BLOCK), strided(kc_ref, r, ATTN_BLOCK)], axis=0)
            v2 = jnp.concatenate([strided(vp_ref, r, ATTN_BLOCK), strided(vc_ref, r, ATTN_BLOCK)], axis=0)
            return r, (strided(q_ref, r, ATTN_BLOCK), k2, v2, bias_first_s)

        def rest_block(u):
            j = u // dil
            start = j * sub + (u - j * dil)
            return start, (strided(q_ref, start, ATTN_BLOCK), strided(kc_ref, start - sub, 2 * ATTN_BLOCK),
                           strided(vc_ref, start - sub, 2 * ATTN_BLOCK), bias_s)

        def run_group(base, firsts):
            placed = [first_block(base + i) if f else rest_block(base + i) for i, f in enumerate(firsts)]
            for (start, _), (o, lse) in zip(placed, _attn_units([b for _, b in placed])):
                store(start, o, lse)

        patterns = [tuple(gi * ATTN_UNROLL + i < n_first for i in range(ATTN_UNROLL))
                    for gi in range(n_units // ATTN_UNROLL)]
        gi = 0
        while gi < len(patterns):
            end = gi
            while end < len(patterns) and patterns[end] == patterns[gi]:
                end += 1
            if end - gi == 1:
                run_group(gi * ATTN_UNROLL, patterns[gi])
            else:
                def body(it, carry, pattern=patterns[gi]):
                    run_group(it * ATTN_UNROLL, pattern)
                    return carry
                lax.fori_loop(gi, end, body, 0)
            gi = end

    m = jnp.maximum(jnp.maximum(l_s[0], l_s[1]), l_s[2])
    w0 = jnp.exp(l_s[0] - m)
    w1 = jnp.exp(l_s[1] - m)
    w2 = jnp.exp(l_s[2] - m)
    o_ref[...] = (w0 * o_s[0] + w1 * o_s[1] + w2 * o_s[2]) / (w0 + w1 + w2)


def _alibi_slopes():
    return jnp.asarray(2.0 ** (-8.0 * np.arange(1, HEADS + 1) / HEADS), dtype=F32)


def _attn_prompt(qkv, attn, n_batch, seq):
    tiles = seq // ATTN_TILE
    slabs = GROUP_W // LANES
    in_specs = [pl.BlockSpec(memory_space=pltpu.SMEM)]
    args = [_alibi_slopes()]
    for g, (_, dil) in enumerate(ATTN_GROUPS):
        sub = ATTN_BLOCK * dil
        per_tile = ATTN_TILE // sub
        for which in range(3):
            col = which * (ATTN_QKV // LANES) + g * slabs
            in_specs.append(pl.BlockSpec((ATTN_TILE, LANES), lambda b, hp, t, col=col: (b * tiles + t, col + hp)))
            args.append(qkv)
        for which in (1, 2):
            col = which * (ATTN_QKV // LANES) + g * slabs
            in_specs.append(pl.BlockSpec(
                (sub, LANES),
                lambda b, hp, t, col=col, per_tile=per_tile: (jnp.maximum((b * tiles + t) * per_tile - 1, 0), col + hp)))
            args.append(qkv)
    in_specs.append(pl.BlockSpec(memory_space=pl.ANY))
    args.append(attn)
    return pl.pallas_call(
        _attn_prompt_kernel,
        grid=(n_batch, slabs, tiles),
        in_specs=in_specs,
        out_specs=pl.BlockSpec((ATTN_TILE, LANES), lambda b, hp, t: (b * tiles + t, hp)),
        out_shape=jax.ShapeDtypeStruct(attn.shape, attn.dtype),
        input_output_aliases={len(args) - 1: 0},
        scratch_shapes=[pltpu.VMEM((N_GROUPS, ATTN_TILE, LANES), F32), pltpu.VMEM((N_GROUPS, ATTN_TILE, LANES), F32),
                        pltpu.VMEM((2, ATTN_BLOCK, 2 * ATTN_BLOCK), F32), pltpu.VMEM((2, ATTN_BLOCK, 2 * ATTN_BLOCK), F32),
                        pltpu.VMEM((5, 4, ATTN_TILE // 4, LANES), F32), pltpu.VMEM((5, 16, ATTN_BLOCK, LANES), F32)],
        compiler_params=_cparams("parallel", "parallel", "arbitrary"),
        name="attn_prompt",
    )(*args)


def _kv_prompt_kernel(k_ref, v_ref, prev_ref, after_ref, o_ref):
    for c, ref in enumerate((k_ref, v_ref)):
        for s in range(GROUP_W // LANES):
            o_ref[c, s] = ref[:, s * LANES:(s + 1) * LANES].T


def _kv_prompt(qkv, g, seq, buf, layer, after):
    n_batch, w = buf.shape[1], buf.shape[-1]
    tm = min(512, w)
    slabs = GROUP_W // LANES
    row0 = (seq - w) // tm
    col = lambda which: which * N_GROUPS + g
    return pl.pallas_call(
        _kv_prompt_kernel,
        grid=(n_batch, w // tm),
        in_specs=[pl.BlockSpec((tm, GROUP_W), lambda b, i: (b * (seq // tm) + row0 + i, col(1))),
                  pl.BlockSpec((tm, GROUP_W), lambda b, i: (b * (seq // tm) + row0 + i, col(2))),
                  pl.BlockSpec(memory_space=pl.ANY), pl.BlockSpec(memory_space=pl.ANY)],
        out_specs=pl.BlockSpec((None, None, 2, slabs, LANES, tm), lambda b, i: (layer, b, 0, 0, 0, i)),
        out_shape=jax.ShapeDtypeStruct(buf.shape, buf.dtype),
        input_output_aliases={2: 0},
        compiler_params=_cparams("parallel", "parallel"),
        name="kv_prompt",
    )(qkv, qkv, buf, after)


def _kv_sample_kernel(x_ref, after_ref, o_ref, *, dec_seq, n_seq):
    for t in range(dec_seq):
        o_ref[t] = x_ref[pl.ds(t, n_seq, stride=dec_seq), :].T


def _kv_sample(qkv, tp, n_seq, dec_seq, after):
    ts = n_seq * dec_seq
    slabs = GROUP_W // LANES
    assert tp % ts == 0
    return pl.pallas_call(
        functools.partial(_kv_sample_kernel, dec_seq=dec_seq, n_seq=n_seq),
        grid=(N_GROUPS, 2, slabs),
        in_specs=[pl.BlockSpec((ts, LANES), lambda g, c, s: (tp // ts, (c + 1) * N_GROUPS * slabs + g * slabs + s)),
                  pl.BlockSpec(memory_space=pl.ANY)],
        out_specs=pl.BlockSpec((None, dec_seq, None, None, LANES, n_seq), lambda g, c, s: (g, 0, c, s, 0, 0)),
        out_shape=jax.ShapeDtypeStruct((N_GROUPS, dec_seq, 2, slabs, LANES, n_seq), F32),
        compiler_params=_cparams("parallel", "parallel", "parallel"),
        name="kv_sample",
    )(qkv, after)


def _attn_sample_kernel(slope_ref, qkv_ref, c0_ref, c1_ref, c2_ref, prev_ref, o_ref, *, dec_seq):
    n = pl.program_id(0)
    half = n % 2
    caches = (c0_ref, c1_ref, c2_ref)
    rows = 2 * SUBLANES
    row = lax.broadcasted_iota(jnp.int32, (rows, 1), 0)
    t_row = (row % SUBLANES) - half * dec_seq
    row_ok = (t_row >= 0) & (t_row < dec_seq)
    lane = lax.broadcasted_iota(jnp.int32, (SUBLANES, LANES), 1)
    first = lane < HEAD_DIM
    ucol = lax.broadcasted_iota(jnp.int32, (rows, SUBLANES), 1) - half * dec_seq
    col_ok = (ucol >= 0) & (ucol < dec_seq)
    slabs = GROUP_W // LANES
    units = [(j, g) for j in range(slabs) for g in range(N_GROUPS)]
    scale = HEAD_DIM ** -0.5
    scores = []
    for j, g in units:
        win, dil = ATTN_GROUPS[g]
        wb = caches[g].shape[-1]
        c = g * slabs + j
        slope = jnp.where(row < SUBLANES, slope_ref[2 * j], slope_ref[2 * j + 1])
        q = qkv_ref[:, c * LANES:(c + 1) * LANES]
        kn = qkv_ref[:, ATTN_QKV + c * LANES:ATTN_QKV + (c + 1) * LANES]
        q2f = jnp.concatenate([jnp.where(first, q, 0.0), jnp.where(first, 0.0, q)], axis=0)
        kt = caches[g][0, j].astype(BF16)
        w = lax.broadcasted_iota(jnp.int32, (rows, wb), 1)
        delta = wb + t_row - w
        ok = row_ok & (delta <= win) & ((delta & (dil - 1)) == 0)
        s_b = jnp.where(ok, _dot(q2f.astype(BF16), kt) * scale - slope * delta.astype(F32), NEG_BIG)
        dn = t_row - ucol
        okn = row_ok & col_ok & (dn >= 0) & (dn <= win) & ((dn & (dil - 1)) == 0)
        s_n = jnp.where(okn, _dot_nt(q2f, kn) * scale - slope * dn.astype(F32), NEG_BIG)
        scores.append((s_b, s_n))
    probs = []
    for s_b, s_n in scores:
        m = jnp.maximum(jnp.max(s_b, axis=-1, keepdims=True), jnp.max(s_n, axis=-1, keepdims=True))
        p_b = jnp.exp(s_b - m)
        p_n = jnp.exp(s_n - m)
        l = jnp.sum(p_b, axis=-1, keepdims=True) + jnp.sum(p_n, axis=-1, keepdims=True)
        probs.append((p_b.astype(BF16), p_n, l, m + jnp.log(l)))
    o_u, l_u = {}, {}
    for (j, g), (p_b, p_n, l, lse) in zip(units, probs):
        c = g * slabs + j
        vn = qkv_ref[:, 2 * ATTN_QKV + c * LANES:2 * ATTN_QKV + (c + 1) * LANES]
        vt = caches[g][1, j].astype(BF16)
        o = (_dot_nt(p_b, vt) + _dot(p_n, vn)) / l
        o_u[j, g] = jnp.where(first, o[:SUBLANES], o[SUBLANES:])
        l_u[j, g] = jnp.where(first, lse[:SUBLANES], lse[SUBLANES:])
    out_slabs = []
    for j in range(slabs):
        m = jnp.maximum(jnp.maximum(l_u[j, 0], l_u[j, 1]), l_u[j, 2])
        ws = [jnp.exp(l_u[j, g] - m) for g in range(N_GROUPS)]
        out_slabs.append((ws[0] * o_u[j, 0] + ws[1] * o_u[j, 1] + ws[2] * o_u[j, 2]) / (ws[0] + ws[1] + ws[2]))
    res = jnp.concatenate(out_slabs, axis=1)
    mine = (lax.broadcasted_iota(jnp.int32, (SUBLANES, 1), 0) // dec_seq) == half

    @pl.when(half == 0)
    def _():
        o_ref[...] = res

    @pl.when(half != 0)
    def _():
        o_ref[...] = jnp.where(mine, res, o_ref[...])


def _attn_sample(qkv, caches_t, layer, attn, tp, n_seq, dec_seq):
    assert 2 * dec_seq == SUBLANES
    blk0 = tp // SUBLANES
    in_specs = [
        pl.BlockSpec(memory_space=pltpu.SMEM),
        pl.BlockSpec((SUBLANES, 3 * ATTN_QKV), lambda n: (blk0 + n // 2, 0)),
    ]
    for c in caches_t:
        in_specs.append(pl.BlockSpec((None, None) + c.shape[2:], lambda n, layer=layer: (layer, n, 0, 0, 0, 0)))
    in_specs.append(pl.BlockSpec(memory_space=pl.ANY))
    return pl.pallas_call(
        functools.partial(_attn_sample_kernel, dec_seq=dec_seq),
        grid=(n_seq,),
        in_specs=in_specs,
        out_specs=pl.BlockSpec((SUBLANES, GROUP_W), lambda n: (blk0 + n // 2, 0)),
        out_shape=jax.ShapeDtypeStruct(attn.shape, attn.dtype),
        input_output_aliases={5: 0},
        compiler_params=_cparams("arbitrary"),
        name="attn_sample",
    )(_alibi_slopes(), qkv, *caches_t, attn)


def _log_sigmoid(x):
    return jnp.minimum(x, 0.0) - jnp.log1p(jnp.exp(-jnp.abs(x)))


def _mlstm_chunks(items):
    ln = items[0][0].shape[0]
    ii = lax.broadcasted_iota(jnp.int32, (ln, ln), 0)
    jj = lax.broadcasted_iota(jnp.int32, (ln, ln), 1)
    eye = ii == jj
    causal = jj <= ii
    stage1 = []
    for q, k, v, i_col, lf_col, c_st, n_st, m_st in items:
        lf_row = jnp.sum(jnp.where(eye, lf_col, 0.0), axis=0, keepdims=True)
        i_row = jnp.sum(jnp.where(eye, i_col, 0.0), axis=0, keepdims=True)
        b_col = jnp.sum(jnp.where(causal, lf_row, 0.0), axis=1, keepdims=True)
        b_row = jnp.sum(jnp.where(ii <= jj, lf_col, 0.0), axis=0, keepdims=True)
        dm = jnp.where(causal, b_col - b_row + i_row, -jnp.inf)
        m_inter = b_col + m_st
        m_t = jnp.maximum(m_inter, jnp.max(dm, axis=1, keepdims=True))
        ks = k * (MLSTM_DK ** -0.5)
        qb = q.astype(BF16)
        vb = v.astype(BF16)
        qk = _dot_nt(qb, ks.astype(BF16))
        qc = _dot(qb, c_st.astype(BF16))
        stage1.append((dm, m_inter, m_t, b_col, ks, vb, qk, qc))
    stage2 = []
    for (dm, m_inter, m_t, _, _, vb, qk, _) in stage1:
        w_intra = jnp.exp(dm - m_t) * qk
        stage2.append((w_intra, jnp.exp(m_inter - m_t), _dot(w_intra.astype(BF16), vb)))
    stage3 = []
    for item, s1, (w_intra, w_inter, wv) in zip(items, stage1, stage2):
        q, _, _, i_col, _, c_st, n_st, m_st = item
        _, _, m_t, b_col, ks, vb, _, qc = s1
        num = wv + w_inter * qc
        den = jnp.sum(w_intra, axis=1, keepdims=True) + w_inter * jnp.sum(q * n_st, axis=1, keepdims=True)
        h = num / jnp.maximum(jnp.abs(den), jnp.exp(-m_t))
        b_last = b_col[ln - 1:ln, :]
        w_src = b_last - b_col + i_col
        m_new = jnp.maximum(b_last + m_st, jnp.max(w_src, axis=0, keepdims=True))
        decay = jnp.exp(b_last + m_st - m_new)
        kp = jnp.exp(w_src - m_new) * ks
        stage3.append((h, decay, kp, m_new, _dot_tn(kp.astype(BF16), vb)))
    outs = []
    for item, (h, decay, kp, m_new, kv) in zip(items, stage3):
        c_st, n_st = item[5], item[6]
        outs.append((h, decay * c_st + kv, decay * n_st + jnp.sum(kp, axis=0, keepdims=True), m_new))
    return outs


def _gate_columns(gates, head):
    lane = lax.broadcasted_iota(jnp.int32, gates.shape, 1)
    i_col = jnp.sum(jnp.where(lane == head, gates, 0.0), axis=1, keepdims=True)
    f_col = jnp.sum(jnp.where(lane == head + MLSTM_HEADS, gates, 0.0), axis=1, keepdims=True)
    return i_col, _log_sigmoid(f_col)


def _mlstm_prompt_kernel(ml_ref, bg_ref, gn_ref, prev_ref, hm_ref, c_out, n_out, m_out, c_s, n_s, m_s):
    chunk = pl.program_id(1)

    @pl.when(chunk == 0)
    def _():
        c_s[...] = jnp.zeros_like(c_s)
        n_s[...] = jnp.zeros_like(n_s)
        m_s[...] = jnp.zeros_like(m_s)

    gates = ml_ref[:, ML_W:ML_W + LANES] + bg_ref[...]
    k0 = MLSTM_HEADS * MLSTM_DK
    v0 = 2 * MLSTM_HEADS * MLSTM_DK
    o0 = v0 + MLSTM_HEADS * MLSTM_DV
    items = []
    for head in range(MLSTM_HEADS):
        i_col, lf_col = _gate_columns(gates, head)
        items.append((ml_ref[:, head * MLSTM_DK:(head + 1) * MLSTM_DK],
                      ml_ref[:, k0 + head * MLSTM_DK:k0 + (head + 1) * MLSTM_DK],
                      ml_ref[:, v0 + head * MLSTM_DV:v0 + (head + 1) * MLSTM_DV],
                      i_col, lf_col, c_s[head], n_s[head], m_s[head]))
    for head, (h, c_new, n_new, m_new) in enumerate(_mlstm_chunks(items)):
        c_s[head] = c_new
        n_s[head] = n_new
        m_s[head] = m_new
        cols = slice(head * MLSTM_DV, (head + 1) * MLSTM_DV)
        og = ml_ref[:, o0 + head * MLSTM_DV:o0 + (head + 1) * MLSTM_DV]
        hm_ref[:, cols] = _rms(h) * gn_ref[:, cols] * _sigmoid(og)

    @pl.when(chunk == pl.num_programs(1) - 1)
    def _():
        c_out[...] = c_s[...]
        n_out[...] = n_s[...]
        m_out[...] = m_s[...]


def _mlstm_prompt(ml, b_gates_pad, g_mlstm, hm, n_seq, seq_len):
    ln = MLSTM_CHUNK
    chunks = seq_len // ln
    st = lambda shp: pl.BlockSpec((None, MLSTM_HEADS) + shp, lambda s, c: (s, 0, 0, 0))
    return pl.pallas_call(
        _mlstm_prompt_kernel,
        grid=(n_seq, chunks),
        in_specs=[
            pl.BlockSpec((ln, ML_PAD_W), lambda s, c: (s * chunks + c, 0)),
            pl.BlockSpec((1, LANES), lambda s, c: (0, 0)),
            pl.BlockSpec((1, MLSTM_HEADS * MLSTM_DV), lambda s, c: (0, 0)),
            pl.BlockSpec(memory_space=pl.ANY),
        ],
        out_specs=[
            pl.BlockSpec((ln, MLSTM_HEADS * MLSTM_DV), lambda s, c: (s * chunks + c, 0)),
            st((MLSTM_DK, MLSTM_DV)), st((1, MLSTM_DK)), st((1, 1)),
        ],
        out_shape=[
            jax.ShapeDtypeStruct(hm.shape, hm.dtype),
            jax.ShapeDtypeStruct((n_seq, MLSTM_HEADS, MLSTM_DK, MLSTM_DV), F32),
            jax.ShapeDtypeStruct((n_seq, MLSTM_HEADS, 1, MLSTM_DK), F32),
            jax.ShapeDtypeStruct((n_seq, MLSTM_HEADS, 1, 1), F32),
        ],
        scratch_shapes=[pltpu.VMEM((MLSTM_HEADS, MLSTM_DK, MLSTM_DV), F32),
                        pltpu.VMEM((MLSTM_HEADS, 1, MLSTM_DK), F32),
                        pltpu.VMEM((MLSTM_HEADS, 1, 1), F32)],
        input_output_aliases={3: 0},
        compiler_params=_cparams("parallel", "arbitrary"),
        name="mlstm_prompt",
    )(ml, b_gates_pad, g_mlstm.reshape(1, -1), hm)


def _mlstm_sample_kernel(ml_ref, bg_ref, gn_ref, c0_ref, n0_ref, m0_ref, prev_ref, cprev_ref,
                         hm_ref, c_out, n_out, m_out, *, seq_rows):
    ln = ml_ref.shape[0]
    row = lax.broadcasted_iota(jnp.int32, (ln, 1), 0)
    gates = ml_ref[:, ML_W:ML_W + LANES] + bg_ref[...]
    k0 = MLSTM_HEADS * MLSTM_DK
    v0 = 2 * MLSTM_HEADS * MLSTM_DK
    o0 = v0 + MLSTM_HEADS * MLSTM_DV
    per_blk = ln // seq_rows
    items = []
    for head in range(MLSTM_HEADS):
        q = ml_ref[:, head * MLSTM_DK:(head + 1) * MLSTM_DK]
        k = ml_ref[:, k0 + head * MLSTM_DK:k0 + (head + 1) * MLSTM_DK]
        v = ml_ref[:, v0 + head * MLSTM_DV:v0 + (head + 1) * MLSTM_DV]
        i_col, lf_col = _gate_columns(gates, head)
        for j in range(per_blk):
            mine = (row // seq_rows) == j
            items.append((q, k, v, jnp.where(mine, i_col, NEG_BIG), jnp.where(mine, lf_col, 0.0),
                          c0_ref[j, head], n0_ref[j, head], m0_ref[j, head]))
    outs = _mlstm_chunks(items)
    for head in range(MLSTM_HEADS):
        h_all = None
        for j in range(per_blk):
            h, c_new, n_new, m_new = outs[head * per_blk + j]
            c_out[j, head] = c_new
            n_out[j, head] = n_new
            m_out[j, head] = m_new
            h_all = h if h_all is None else jnp.where((row // seq_rows) == j, h, h_all)
        cols = slice(head * MLSTM_DV, (head + 1) * MLSTM_DV)
        og = ml_ref[:, o0 + head * MLSTM_DV:o0 + (head + 1) * MLSTM_DV]
        hm_ref[:, cols] = _rms(h_all) * gn_ref[:, cols] * _sigmoid(og)


def _mlstm_sample(ml, b_gates_pad, g_mlstm, c0, n0, m0, layer, hm, c_new_all, row0, n_seq, seq_len):
    per_blk = SUBLANES // seq_len
    assert per_blk * seq_len == SUBLANES and n_seq % per_blk == 0 and row0 % SUBLANES == 0
    blk0 = row0 // SUBLANES
    st_in = lambda shp: pl.BlockSpec((None, per_blk, MLSTM_HEADS) + shp, lambda s: (layer, s, 0, 0, 0))
    st_out = lambda shp: pl.BlockSpec((per_blk, MLSTM_HEADS) + shp, lambda s: (s, 0, 0, 0))
    return pl.pallas_call(
        functools.partial(_mlstm_sample_kernel, seq_rows=seq_len),
        grid=(n_seq // per_blk,),
        in_specs=[
            pl.BlockSpec((SUBLANES, ML_PAD_W), lambda s: (blk0 + s, 0)),
            pl.BlockSpec((1, LANES), lambda s: (0, 0)),
            pl.BlockSpec((1, MLSTM_HEADS * MLSTM_DV), lambda s: (0, 0)),
            st_in((MLSTM_DK, MLSTM_DV)), st_in((1, MLSTM_DK)), st_in((1, 1)),
            pl.BlockSpec(memory_space=pl.ANY),
            pl.BlockSpec(memory_space=pl.ANY),
        ],
        out_specs=[
            pl.BlockSpec((SUBLANES, MLSTM_HEADS * MLSTM_DV), lambda s: (blk0 + s, 0)),
            st_in((MLSTM_DK, MLSTM_DV)), st_out((1, MLSTM_DK)), st_out((1, 1)),
        ],
        out_shape=[
            jax.ShapeDtypeStruct(hm.shape, hm.dtype),
            jax.ShapeDtypeStruct(c_new_all.shape, F32),
            jax.ShapeDtypeStruct(n0.shape[1:], F32),
            jax.ShapeDtypeStruct(m0.shape[1:], F32),
        ],
        input_output_aliases={6: 0, 7: 1},
        compiler_params=_cparams("parallel"),
        name="mlstm_sample",
    )(ml, b_gates_pad, g_mlstm.reshape(1, -1), c0, n0, m0, hm, c_new_all)


def _postmix_kernel(attn_ref, hm_ref, gt_ref, x_ref, gtp_ref, shp_ref, scp_ref, gts_ref, shs_ref, scs_ref,
                    gf_ref, wa_ref, wm_ref, wo_ref, wr_ref, br_ref, x1_ref, h2_ref, gw_ref, ei_ref, rk_ref, cnt_ref,
                    cnt_s, *, n_prompt_tiles):
    is_sample = pl.program_id(0) >= n_prompt_tiles
    d = x_ref.shape[1]
    tm = x_ref.shape[0]
    pr = tm // POSTMIX_PARTS
    parts = [slice(p * pr, (p + 1) * pr) for p in range(POSTMIX_PARTS)]

    def mod(ref_s, ref_p, rows):
        return jnp.where(is_sample, ref_s[rows], ref_p[...])

    @pl.when(pl.program_id(0) == 0)
    def _():
        cnt_s[...] = jnp.zeros_like(cnt_s)

    branch = [(_dot(attn_ref[rows].astype(BF16), wa_ref[...]), _dot(hm_ref[rows].astype(BF16), wm_ref[...]))
              for rows in parts]
    proj = []
    for rows, (y_attn, y_mlstm) in zip(parts, branch):
        merged = _sigmoid(gt_ref[rows, :d]) * y_attn + _sigmoid(gt_ref[rows, d:]) * y_mlstm
        proj.append(_dot(merged.astype(BF16), wo_ref[...]))
    lane = lax.broadcasted_iota(jnp.int32, (pr, LANES), 1)
    logits = []
    for rows, z in zip(parts, proj):
        x1 = x_ref[rows] + mod(gts_ref, gtp_ref, rows) * z
        x1_ref[rows] = x1
        h2 = (_rms(x1) * gf_ref[...] * (1.0 + mod(scs_ref, scp_ref, rows)) + mod(shs_ref, shp_ref, rows)).astype(BF16)
        bits = lax.bitcast_convert_type(h2.astype(F32), jnp.int32)
        h2_ref[rows] = lax.shift_right_logical(bits[:, :d // 2], 16) | bits[:, d // 2:]
        logits.append(jnp.where(lane < N_EXPERTS, _dot(h2, wr_ref[...]) + br_ref[...], -jnp.inf))

    ri = lax.broadcasted_iota(jnp.int32, (pr, pr), 0)
    ci = lax.broadcasted_iota(jnp.int32, (pr, pr), 1)
    before = jnp.where(ci < ri, 1.0, 0.0).astype(BF16)
    picks = []
    for lg in logits:
        vals, idxs = [], []
        chosen = jnp.zeros((pr, LANES), F32)
        for _ in range(TOP_K):
            m = jnp.max(lg, axis=-1, keepdims=True)
            idx = jnp.min(jnp.where(lg == m, lane, LANES), axis=-1, keepdims=True)
            hit = lane == idx
            vals.append(m)
            idxs.append(idx)
            chosen = jnp.where(hit, 1.0, chosen)
            lg = jnp.where(hit, -jnp.inf, lg)
        picks.append((vals, idxs, chosen, _dot(before, chosen.astype(BF16))))
    total = cnt_s[...]
    for rows, (vals, idxs, chosen, within) in zip(parts, picks):
        prefix = within + total
        ex = [jnp.exp(v - vals[0]) for v in vals]
        den = ex[0] + ex[1] + ex[2] + ex[3]
        for k in range(TOP_K):
            gw_ref[rows, k:k + 1] = ex[k] / den
            ei_ref[rows, k:k + 1] = idxs[k]
            rank = jnp.sum(jnp.where(lane == idxs[k], prefix, 0.0), axis=-1, keepdims=True)
            rk_ref[rows, k:k + 1] = rank.astype(jnp.int32)
        total = total + jnp.sum(chosen, axis=0, keepdims=True)
    cnt_s[...] = total
    cnt_ref[...] = total.astype(jnp.int32)


def _postmix(attn, hm, gt, x_all, mod_p, mod_s, layer, g_ffn, wa, wm, wo, wr, br, tp, n_batch):
    t_all, d = x_all.shape
    tm = ROW_TILE
    n_prompt_tiles = tp // tm
    tiles_per_batch = n_prompt_tiles // n_batch
    specs = [_mod_specs(c, tm, n_prompt_tiles, tiles_per_batch, n_batch, layer) for c in (2, 3, 4)]
    row = lambda w: pl.BlockSpec((tm, w), lambda i: (i, 0))
    full = lambda a: pl.BlockSpec(a.shape, lambda i: (0,) * a.ndim)
    gf = g_ffn.reshape(1, d)
    return pl.pallas_call(
        functools.partial(_postmix_kernel, n_prompt_tiles=n_prompt_tiles),
        grid=(t_all // tm,),
        in_specs=[row(GROUP_W), row(d), row(GATE_W), row(d),
                  specs[0][0], specs[1][0], specs[2][0], specs[0][1], specs[1][1], specs[2][1],
                  full(gf), full(wa), full(wm), full(wo), full(wr), full(br)],
        out_specs=[row(d), row(d // 2), row(TOP_K), row(TOP_K), row(TOP_K),
                   pl.BlockSpec((1, LANES), lambda i: (0, 0))],
        out_shape=[jax.ShapeDtypeStruct((t_all, d), F32), jax.ShapeDtypeStruct((t_all, d // 2), jnp.int32),
                   jax.ShapeDtypeStruct((t_all, TOP_K), F32), jax.ShapeDtypeStruct((t_all, TOP_K), jnp.int32),
                   jax.ShapeDtypeStruct((t_all, TOP_K), jnp.int32), jax.ShapeDtypeStruct((1, LANES), jnp.int32)],
        scratch_shapes=[pltpu.VMEM((1, LANES), F32)],
        compiler_params=_cparams("arbitrary"),
        name="postmix",
    )(attn, hm, gt, x_all, mod_p, mod_p, mod_p, mod_s, mod_s, mod_s, gf, wa, wm, wo, wr, br)


def _expert_kernel(be_ref, ne_ref, nused_ref, rows_ref, wgu_hbm, bgu_ref, wd_hbm, bd_ref, after_ref, y_ref,
                   wgu_f, wd_f, wgu_s, wd_s, sem, *, layer):
    i = pl.program_id(0)
    e = be_ref[i]
    prev = be_ref[jnp.maximum(i - 1, 0)]

    def weight_copies(expert):
        return (pltpu.make_async_copy(wgu_hbm.at[layer, expert], wgu_f, sem.at[0]),
                pltpu.make_async_copy(wd_hbm.at[layer, expert], wd_f, sem.at[1]))

    @pl.when(i == 0)
    def _():
        for cp in weight_copies(e):
            cp.start()

    @pl.when((i == 0) | (e != prev))
    def _():
        for cp in weight_copies(e):
            cp.wait()
        wgu_s[...] = wgu_f[...].astype(BF16)
        wd_s[...] = wd_f[...].astype(BF16)
        nxt = ne_ref[i]

        @pl.when(nxt >= 0)
        def _():
            for cp in weight_copies(nxt):
                cp.start()

    @pl.when(i < nused_ref[0])
    def _():
        de = wd_s.shape[0]
        words = rows_ref[...]
        lo = lax.bitcast_convert_type(lax.shift_left(words, 16), F32).astype(BF16)
        hi = lax.bitcast_convert_type(words & jnp.int32(-65536), F32).astype(BF16)
        rows = jnp.concatenate([lo, hi], axis=1)
        gu = _dot(rows, wgu_s[...]) + bgu_ref[...]
        gate = jnp.minimum(gu[:, :de], SWIGLU_LIMIT)
        up = jnp.clip(gu[:, de:], -SWIGLU_LIMIT, SWIGLU_LIMIT)
        act = (up + 1.0) * gate * _sigmoid(SWIGLU_ALPHA * gate)
        y_ref[...] = _dot(act.astype(BF16), wd_s[...]) + bd_ref[...]

    @pl.when(i >= nused_ref[0])
    def _():
        y_ref[...] = jnp.zeros_like(y_ref)


def _experts(block_expert, next_expert, n_used, rows, w_gu, b_gu, w_down, b_down, layer, after):
    nrows = rows.shape[0]
    n_blocks = nrows // MOE_BLOCK
    d, de = w_down.shape[3], w_down.shape[2]
    grid_spec = pltpu.PrefetchScalarGridSpec(
        num_scalar_prefetch=3,
        grid=(n_blocks,),
        in_specs=[
            pl.BlockSpec((MOE_BLOCK, d // 2), lambda i, be, ne, nu: (i, 0)),
            pl.BlockSpec(memory_space=pl.ANY),
            pl.BlockSpec((None, None, 1, 2 * de), lambda i, be, ne, nu: (layer, be[i], 0, 0)),
            pl.BlockSpec(memory_space=pl.ANY),
            pl.BlockSpec((None, None, 1, d), lambda i, be, ne, nu: (layer, be[i], 0, 0)),
            pl.BlockSpec(memory_space=pl.ANY),
        ],
        out_specs=pl.BlockSpec((MOE_BLOCK, d), lambda i, be, ne, nu: (i, 0)),
        scratch_shapes=[pltpu.VMEM((d, 2 * de), F32), pltpu.VMEM((de, d), F32),
                        pltpu.VMEM((d, 2 * de), BF16), pltpu.VMEM((de, d), BF16),
                        pltpu.SemaphoreType.DMA((2,))],
    )
    return pl.pallas_call(
        functools.partial(_expert_kernel, layer=layer),
        grid_spec=grid_spec,
        out_shape=jax.ShapeDtypeStruct((nrows, d), F32),
        compiler_params=_cparams("arbitrary"),
        name="experts",
    )(block_expert, next_expert, n_used, rows, w_gu, b_gu.reshape(b_gu.shape[0], b_gu.shape[1], 1, -1),
      w_down, b_down.reshape(b_down.shape[0], b_down.shape[1], 1, -1), after)


def _sc_mesh():
    return plsc.VectorSubcoreMesh(core_axis_name="core", subcore_axis_name="subcore")


def _sc_scatter_rows(x, pos, n_rows):
    t, c = x.shape
    nk = pos.shape[0]
    nwin = t // SC_SCATTER_WIN
    assert nwin * SC_SCATTER_WIN == t
    idx = pos.reshape(nk * nwin, SC_SCATTER_WIN)

    @functools.partial(pl.kernel, out_type=jax.ShapeDtypeStruct((n_rows, c), x.dtype), mesh=_sc_mesh(),
                       scratch_types=[], name="sc_scatter_rows")
    def kern(x_hbm, i_hbm, o_hbm):
        def body(x_vmem, i_vmem):
            pltpu.sync_copy(x_vmem, o_hbm.at[i_vmem.at[0]])

        pltpu.emit_pipeline(
            body,
            grid=(nk * nwin,),
            in_specs=[pl.BlockSpec((SC_SCATTER_WIN, c), lambda i: (i % nwin, 0)),
                      pl.BlockSpec((1, SC_SCATTER_WIN), lambda i: (i, 0))],
            out_specs=[],
            core_axis_name=("core", "subcore"),
            dimension_semantics=(pltpu.PARALLEL,),
        )(x_hbm, i_hbm)

    return kern(x, idx)


def _sc_gather_rows(y, pos):
    nk, t = pos.shape
    c = y.shape[1]
    n = nk * t
    assert n % SC_GATHER_WIN == 0
    idx = pos.reshape(n // SC_GATHER_WIN, SC_GATHER_WIN)

    @functools.partial(pl.kernel, out_type=jax.ShapeDtypeStruct((n, c), y.dtype), mesh=_sc_mesh(),
                       scratch_types=[], name="sc_gather_rows")
    def kern(y_hbm, i_hbm, o_hbm):
        def body(i_vmem, o_vmem):
            pltpu.sync_copy(y_hbm.at[i_vmem.at[0]], o_vmem)

        pltpu.emit_pipeline(
            body,
            grid=(n // SC_GATHER_WIN,),
            in_specs=[pl.BlockSpec((1, SC_GATHER_WIN), lambda i: (i, 0))],
            out_specs=[pl.BlockSpec((SC_GATHER_WIN, c), lambda i: (i, 0))],
            core_axis_name=("core", "subcore"),
            dimension_semantics=(pltpu.PARALLEL,),
        )(i_hbm, o_hbm)

    return kern(y, idx).reshape(nk, t, c)


def _combine_kernel(yg_ref, gw_ref, x1_ref, gtp_ref, gts_ref, gfin_ref, after_ref, o_ref, *, n_prompt_tiles, final):
    is_sample = pl.program_id(0) >= n_prompt_tiles
    gate2 = jnp.where(is_sample, gts_ref[...], gtp_ref[...])
    gw = gw_ref[...]
    acc = gw[:, 0:1] * yg_ref[0]
    for k in range(1, TOP_K):
        acc = acc + gw[:, k:k + 1] * yg_ref[k]
    x2 = x1_ref[...] + gate2 * acc
    o_ref[...] = _rms(x2) * gfin_ref[...] if final else x2


def _combine(yg, gates, x1, mod_p, mod_s, layer, g_final, tp, n_batch, final, after):
    t_all, d = x1.shape
    tm = ROW_TILE
    n_prompt_tiles = tp // tm
    tiles_per_batch = n_prompt_tiles // n_batch
    gtp, gts = _mod_specs(5, tm, n_prompt_tiles, tiles_per_batch, n_batch, layer)
    row = lambda w: pl.BlockSpec((tm, w), lambda i: (i, 0))
    return pl.pallas_call(
        functools.partial(_combine_kernel, n_prompt_tiles=n_prompt_tiles, final=final),
        grid=(t_all // tm,),
        in_specs=[pl.BlockSpec((TOP_K, tm, d), lambda i: (0, i, 0)), row(TOP_K), row(d), gtp, gts,
                  pl.BlockSpec((1, d), lambda i: (0, 0)), pl.BlockSpec(memory_space=pl.ANY)],
        out_specs=row(d),
        out_shape=jax.ShapeDtypeStruct((t_all, d), F32),
        compiler_params=_cparams("parallel"),
        name="moe_combine",
    )(yg, gates, x1, mod_p, mod_s, g_final.reshape(1, d), after)


def _positions(eidx, rank, counts):
    t = eidx.shape[0]
    n_blocks = t * TOP_K // MOE_BLOCK + N_EXPERTS
    padded = (counts + MOE_BLOCK - 1) // MOE_BLOCK * MOE_BLOCK
    pad_end = jnp.cumsum(padded)
    pad_start = pad_end - padded
    experts = jnp.arange(N_EXPERTS, dtype=jnp.int32)
    start = jnp.sum(jnp.where(eidx[:, :, None] == experts, pad_start, 0), axis=-1)
    pos = (start + rank).T.astype(jnp.int32)
    n_used = (pad_end[-1] // MOE_BLOCK).astype(jnp.int32)
    blk = jnp.minimum(jnp.arange(n_blocks, dtype=jnp.int32), n_used - 1)
    block_expert = jnp.sum((pad_end[None, :] <= (blk * MOE_BLOCK)[:, None]).astype(jnp.int32), axis=1)
    block_expert = jnp.minimum(block_expert, N_EXPERTS - 1).astype(jnp.int32)
    later_used = (counts[None, :] > 0) & (experts[None, :] > experts[:, None])
    nxt = jnp.min(jnp.where(later_used, experts[None, :], N_EXPERTS), axis=1)
    nxt = jnp.where(nxt < N_EXPERTS, nxt, -1)
    next_expert = jnp.sum(jnp.where(block_expert[:, None] == experts[None, :], nxt[None, :], 0), axis=1)
    return pos, block_expert, next_expert.astype(jnp.int32), n_used.reshape(1), n_blocks


def kernel(x_prompt, x_sample, cache_kv_w128, cache_kv_w512, cache_kv_w2048, state_mlstm_C, state_mlstm_n,
           state_mlstm_m, c_prompt, c_sample, w_ada, b_ada, g_mix, g_ffn, w_in, b_gates, g_mlstm, w_br_attn,
           w_br_mlstm, w_out, w_router, b_router, w_gu, b_gu, w_down, b_down, g_final):
    n_batch, seq, d = x_prompt.shape
    n_seq, dec_seq, _ = x_sample.shape
    depth = w_ada.shape[0]
    tp, ts = n_batch * seq, n_seq * dec_seq
    t_all = tp + ts

    x_all = jnp.concatenate([x_prompt.reshape(tp, d), x_sample.reshape(ts, d)], axis=0)
    mod_p = _ada_mod(c_prompt, w_ada, b_ada).reshape(depth, n_batch, 1, 6 * d)
    mod_s = _ada_mod(jnp.repeat(c_sample, dec_seq, axis=0), w_ada, b_ada)
    caches_t = [jnp.transpose(c, (0, 1, 3, 4, 5, 2)).reshape(c.shape[0], c.shape[1], 2, HEADS // 2, 2 * HEAD_DIM, c.shape[2])
                for c in (cache_kv_w128, cache_kv_w512, cache_kv_w2048)]
    sn = state_mlstm_n.reshape(depth, n_seq, MLSTM_HEADS, 1, MLSTM_DK)
    sm = state_mlstm_m.reshape(depth, n_seq, MLSTM_HEADS, 1, 1)
    wr_pad = jnp.pad(w_router, ((0, 0), (0, 0), (0, LANES - N_EXPERTS))).astype(BF16)
    br_pad = jnp.pad(b_router, ((0, 0), (0, LANES - N_EXPERTS))).reshape(depth, 1, LANES)
    bg_pad = jnp.pad(b_gates, ((0, 0), (0, LANES - b_gates.shape[1]))).reshape(depth, 1, LANES)

    kvs = [[] for _ in range(N_GROUPS)]
    mp = [[], [], []]
    ms = [[], []]
    attn = jnp.zeros((t_all, GROUP_W), F32)
    hm = jnp.zeros((t_all, MLSTM_HEADS * MLSTM_DV), F32)
    kvp = [jnp.zeros((depth, n_batch, 2, HEADS // 2, 2 * HEAD_DIM, min(win, seq)), F32) for win, _ in ATTN_GROUPS]
    c_s = jnp.zeros(state_mlstm_C.shape, F32)
    for l in range(depth):
        qkv, ml, gt = _inproj(x_all, g_mix[l], mod_p, mod_s, l, _split_w_in(w_in[l]), tp, n_batch)

        attn = _attn_prompt(qkv, attn, n_batch, seq)
        attn = _attn_sample(qkv, caches_t, l, attn, tp, n_seq, dec_seq)

        hm, c_p, n_p, m_p = _mlstm_prompt(ml, bg_pad[l], g_mlstm[l], hm, n_batch, seq)
        hm, c_s, n_s, m_s = _mlstm_sample(ml, bg_pad[l], g_mlstm[l], state_mlstm_C, sn, sm, l, hm, c_s,
                                          tp, n_seq, dec_seq)

        x1, h2w, gates, eidx, rank, counts = _postmix(
            attn, hm, gt, x_all, mod_p, mod_s, l, g_ffn[l], w_br_attn[l].astype(BF16), w_br_mlstm[l].astype(BF16),
            w_out[l].astype(BF16), wr_pad[l], br_pad[l], tp, n_batch)

        pos, block_expert, next_expert, n_used, n_blocks = _positions(eidx, rank, counts[0, :N_EXPERTS])
        rows = _sc_scatter_rows(h2w, pos, n_blocks * MOE_BLOCK)
        kvp[2] = _kv_prompt(qkv, 2, seq, kvp[2], l, after=x1)
        y = _experts(block_expert, next_expert, n_used, rows, w_gu, b_gu, w_down, b_down, l, after=kvp[2])
        yg = _sc_gather_rows(y, pos)
        kv_s = _kv_sample(qkv, tp, n_seq, dec_seq, after=y)
        kvp[1] = _kv_prompt(qkv, 1, seq, kvp[1], l, after=kv_s)
        kvp[0] = _kv_prompt(qkv, 0, seq, kvp[0], l, after=kvp[1])
        for g in range(N_GROUPS):
            kvs[g].append(kv_s[g])
        x_all = _combine(yg, gates, x1, mod_p, mod_s, l, g_final, tp, n_batch, final=(l == depth - 1), after=kvp[0])
        for lst, val in zip(mp, (c_p, n_p.reshape(n_batch, MLSTM_HEADS, MLSTM_DK), m_p.reshape(n_batch, MLSTM_HEADS))):
            lst.append(val)
        for lst, val in zip(ms, (n_s.reshape(n_seq, MLSTM_HEADS, MLSTM_DK), m_s.reshape(n_seq, MLSTM_HEADS))):
            lst.append(val)

    y_prompt = x_all[:tp].reshape(n_batch, seq, d)
    y_sample = x_all[tp:].reshape(n_seq, dec_seq, d)

    def kv_prompt_out(a):
        a = a.reshape(depth, n_batch, 2, HEADS, HEAD_DIM, a.shape[-1])
        return jnp.transpose(a, (0, 1, 5, 2, 3, 4))

    def kv_sample_out(parts):
        a = jnp.stack(parts).reshape(depth, dec_seq, 2, HEADS, HEAD_DIM, n_seq)
        return jnp.transpose(a, (0, 5, 1, 2, 3, 4))

    return (y_prompt, y_sample,
            kv_prompt_out(kvp[0]), kv_prompt_out(kvp[1]), kv_prompt_out(kvp[2]),
            jnp.stack(mp[0]), jnp.stack(mp[1]), jnp.stack(mp[2]),
            kv_sample_out(kvs[0]), kv_sample_out(kvs[1]), kv_sample_out(kvs[2]),
            c_s, jnp.stack(ms[0]), jnp.stack(ms[1]))
```

```python
import functools

import jax
import jax.numpy as jnp
import numpy as np
from jax import lax
from jax.experimental import pallas as pl
from jax.experimental.pallas import tpu as pltpu
from jax.experimental.pallas import tpu_sc as plsc

F32 = jnp.float32
BF16 = jnp.bfloat16

ATTN_GROUPS = ((128, 1), (512, 4), (2048, 16))
N_GROUPS = len(ATTN_GROUPS)
HEADS = 8
HEAD_DIM = 64
ATTN_BLOCK = 128
GROUP_W = HEADS * HEAD_DIM
ATTN_QKV = N_GROUPS * GROUP_W
MLSTM_HEADS = 4
MLSTM_DK = 128
MLSTM_DV = 256
N_EXPERTS = 32
TOP_K = 4
SWIGLU_LIMIT = 7.0
SWIGLU_ALPHA = 1.702
RMS_EPS = 1e-6
NEG_BIG = -1e30

LANES = 128
SUBLANES = 8
VMEM_BYTES = 64 * 1024 * 1024
VMEM_LIMIT_BYTES = VMEM_BYTES - 8 * 1024 * 1024

ROW_TILE = 256
COMBINE_TILE = 512
ATTN_TILE = 2048
ATTN_UNROLL = 8
MLSTM_CHUNK = 256
MOE_BLOCK = 512
POSTMIX_PARTS = 2
SC_SCATTER_WIN = 48
SC_GATHER_WIN = 48

ML_W = 2 * MLSTM_HEADS * MLSTM_DK + 2 * MLSTM_HEADS * MLSTM_DV
ML_PAD_W = ML_W + LANES
GATE_W = 2048


def _cparams(*sem):
    return pltpu.CompilerParams(dimension_semantics=sem, vmem_limit_bytes=VMEM_LIMIT_BYTES)


def _dot(a, b):
    return jnp.dot(a, b, preferred_element_type=F32)


def _dot_nt(a, b):
    return lax.dot_general(a, b, (((1,), (1,)), ((), ())), preferred_element_type=F32)


def _dot_tn(a, b):
    return lax.dot_general(a, b, (((0,), (0,)), ((), ())), preferred_element_type=F32)


def _ada_kernel(c_ref, w_ref, b_ref, o_ref):
    c = c_ref[...]
    s = c * jax.nn.sigmoid(c)
    o_ref[...] = _dot(s.astype(BF16), w_ref[...].astype(BF16)) + b_ref[...]


def _ada_mod(c_all, w_ada, b_ada):
    depth, d, n = w_ada.shape
    bc = c_all.shape[0]
    tn = 1024
    return pl.pallas_call(
        _ada_kernel,
        grid=(depth, n // tn),
        in_specs=[
            pl.BlockSpec((bc, d), lambda l, j: (0, 0)),
            pl.BlockSpec((None, d, tn), lambda l, j: (l, 0, j)),
            pl.BlockSpec((None, 1, tn), lambda l, j: (l, 0, j)),
        ],
        out_specs=pl.BlockSpec((None, bc, tn), lambda l, j: (l, 0, j)),
        out_shape=jax.ShapeDtypeStruct((depth, bc, n), F32),
        compiler_params=_cparams("parallel", "parallel"),
        name="ada_mod",
    )(c_all, w_ada, b_ada.reshape(depth, 1, n))


def _mod_specs(chunk, tm, n_prompt_tiles, tiles_per_batch, n_batch, layer):
    d = 1024
    sp = pl.BlockSpec((None, None, 1, d),
                      lambda i: (layer, jnp.minimum(i // tiles_per_batch, n_batch - 1), 0, chunk))
    ss = pl.BlockSpec((None, tm, d), lambda i: (layer, jnp.maximum(i - n_prompt_tiles, 0), chunk))
    return sp, ss


def _sigmoid(x):
    return 0.5 * jnp.tanh(0.5 * x) + 0.5


def _rms(x):
    return x * lax.rsqrt(jnp.mean(x * x, axis=-1, keepdims=True) + RMS_EPS)


def _inproj_kernel(x_ref, g_ref, shp_ref, scp_ref, shs_ref, scs_ref, wa_ref, wif_ref, wg_ref,
                   qkv_ref, ml_ref, gt_ref, *, n_prompt_tiles):
    is_sample = pl.program_id(0) >= n_prompt_tiles
    sh = jnp.where(is_sample, shs_ref[...], shp_ref[...])
    sc = jnp.where(is_sample, scs_ref[...], scp_ref[...])
    h = (_rms(x_ref[...]) * g_ref[...] * (1.0 + sc) + sh).astype(BF16)
    cw = 512
    a = 3 * ATTN_QKV
    for c0 in range(0, a, cw):
        qkv_ref[:, c0:c0 + cw] = _dot(h, wa_ref[:, c0:c0 + cw])
    for c0 in range(0, ML_W, cw):
        ml_ref[:, c0:c0 + cw] = _dot(h, wa_ref[:, a + c0:a + c0 + cw])
    ml_ref[:, ML_W:] = _dot(h, wif_ref[...])
    for c0 in range(0, GATE_W, cw):
        gt_ref[:, c0:c0 + cw] = _dot(h, wg_ref[:, c0:c0 + cw])


def _split_w_in(w_in_l):
    a = 3 * ATTN_QKV + ML_W
    if_w = 2 * MLSTM_HEADS
    w_if = jnp.pad(w_in_l[:, a:a + if_w], ((0, 0), (0, LANES - if_w)))
    return w_in_l[:, :a].astype(BF16), w_if.astype(BF16), w_in_l[:, a + if_w:].astype(BF16)


def _inproj(x_all, g_mix, mod_p, mod_s, layer, w_parts, tp, n_batch):
    t_all, d = x_all.shape
    tm = ROW_TILE
    assert t_all % tm == 0 and tp % (tm * n_batch) == 0
    n_prompt_tiles = tp // tm
    tiles_per_batch = n_prompt_tiles // n_batch
    shp, shs = _mod_specs(0, tm, n_prompt_tiles, tiles_per_batch, n_batch, layer)
    scp, scs = _mod_specs(1, tm, n_prompt_tiles, tiles_per_batch, n_batch, layer)
    row = lambda w: pl.BlockSpec((tm, w), lambda i: (i, 0))
    return pl.pallas_call(
        functools.partial(_inproj_kernel, n_prompt_tiles=n_prompt_tiles),
        grid=(t_all // tm,),
        in_specs=[
            row(d),
            pl.BlockSpec((1, d), lambda i: (0, 0)),
            shp, scp, shs, scs,
        ] + [pl.BlockSpec(w.shape, lambda i: (0, 0), pipeline_mode=pl.Buffered(1)) for w in w_parts],
        out_specs=[row(3 * ATTN_QKV), row(ML_PAD_W), row(GATE_W)],
        out_shape=[
            jax.ShapeDtypeStruct((t_all, 3 * ATTN_QKV), F32),
            jax.ShapeDtypeStruct((t_all, ML_PAD_W), F32),
            jax.ShapeDtypeStruct((t_all, GATE_W), F32),
        ],
        compiler_params=_cparams("parallel"),
        name="inproj",
    )(x_all, g_mix.reshape(1, d), mod_p, mod_p, mod_s, mod_s, *w_parts)


def _attn_bias(slope, dil, valid_prev):
    qi = lax.broadcasted_iota(jnp.int32, (ATTN_BLOCK, 2 * ATTN_BLOCK), 0)
    ki = lax.broadcasted_iota(jnp.int32, (ATTN_BLOCK, 2 * ATTN_BLOCK), 1)
    dist = qi + ATTN_BLOCK - ki
    keep = (dist >= 0) & (dist <= ATTN_BLOCK) & ((ki >= ATTN_BLOCK) | valid_prev)
    return jnp.where(keep, -slope * (dil * dist).astype(F32), NEG_BIG)


def _attn_units(blocks):
    lane = lax.broadcasted_iota(jnp.int32, (ATTN_BLOCK, LANES), 1)
    first = lane < HEAD_DIM
    scores, values = [], []
    for q, k2, v2, bias_ref in blocks:
        k2b = k2.astype(BF16)
        qs = q * (HEAD_DIM ** -0.5)
        for e in range(2):
            qh = jnp.where(first if e == 0 else ~first, qs, 0.0).astype(BF16)
            scores.append(_dot_nt(qh, k2b) + bias_ref[e])
        values.append(v2.astype(BF16))
    probs = []
    for s in scores:
        m = jnp.max(s, axis=-1, keepdims=True)
        p = jnp.exp(s - m)
        l = jnp.sum(p, axis=-1, keepdims=True)
        probs.append((p.astype(BF16), l, m + jnp.log(l)))
    outs = []
    for u, v2b in enumerate(values):
        (p0, l0, lse0), (p1, l1, lse1) = probs[2 * u], probs[2 * u + 1]
        o0 = _dot(p0, v2b) / l0
        o1 = _dot(p1, v2b) / l1
        outs.append((jnp.where(first, o0, o1), jnp.where(first, lse0, lse1)))
    return outs


def _attn_prompt_kernel(slope_ref, *refs):
    ins, o_ref, o_s, l_s, bias_s, bias_first_s, tmp_s, dint_s = refs[:15], *refs[16:23]
    hp = pl.program_id(1)
    tile = pl.program_id(2)
    for g, (_, dil) in enumerate(ATTN_GROUPS):
        q_ref, kc_ref, vc_ref, kp_ref, vp_ref = ins[5 * g:5 * g + 5]
        sub = ATTN_BLOCK * dil
        n_first = dil
        n_units = ATTN_TILE // ATTN_BLOCK
        for e in range(2):
            bias_s[e] = _attn_bias(slope_ref[2 * hp + e], dil, True)
            bias_first_s[e] = _attn_bias(slope_ref[2 * hp + e], dil, tile > 0)

        def strided(ref, start, size):
            return ref[pl.ds(start, size, stride=dil), :] if dil > 1 else ref[pl.ds(start, size), :]

        def store(start, o, lse):
            if dil > 1:
                o_s[g, pl.ds(start, ATTN_BLOCK, stride=dil), :] = o
                l_s[g, pl.ds(start, ATTN_BLOCK, stride=dil), :] = lse
            else:
                o_s[g, pl.ds(start, ATTN_BLOCK), :] = o
                l_s[g, pl.ds(start, ATTN_BLOCK), :] = lse

        split16 = dil == 16 and sub == ATTN_TILE
        if split16:
            for i, src in enumerate((q_ref, kc_ref, vc_ref, kp_ref, vp_ref)):
                for a in range(4):
                    tmp_s[i, a] = src[pl.ds(a, ATTN_TILE // 4, stride=4), :]
                for a in range(4):
                    for b in range(4):
                        dint_s[i, a + 4 * b] = tmp_s[i, a, pl.ds(b, ATTN_BLOCK, stride=4), :]

        def first_block(r):
            if split16:
                k2 = jnp.concatenate([dint_s[3, r], dint_s[1, r]], axis=0)
                v2 = jnp.concatenate([dint_s[4, r], dint_s[2, r]], axis=0)
                return r, (dint_s[0, r], k2, v2, bias_first_s)
            k2 = jnp.concatenate([strided(kp_ref, r, ATTN_BLOCK), strided(kc_ref, r, ATTN_BLOCK)], axis=0)
            v2 = jnp.concatenate([strided(vp_ref, r, ATTN_BLOCK), strided(vc_ref, r, ATTN_BLOCK)], axis=0)
            return r, (strided(q_ref, r, ATTN_BLOCK), k2, v2, bias_first_s)

        def rest_block(u):
            j = u // dil
            start = j * sub + (u - j * dil)
            return start, (strided(q_ref, start, ATTN_BLOCK), strided(kc_ref, start - sub, 2 * ATTN_BLOCK),
                           strided(vc_ref, start - sub, 2 * ATTN_BLOCK), bias_s)

        def run_group(base, firsts):
            placed = [first_block(base + i) if f else rest_block(base + i) for i, f in enumerate(firsts)]
            for (start, _), (o, lse) in zip(placed, _attn_units([b for _, b in placed])):
                store(start, o, lse)

        patterns = [tuple(gi * ATTN_UNROLL + i < n_first for i in range(ATTN_UNROLL))
                    for gi in range(n_units // ATTN_UNROLL)]
        gi = 0
        while gi < len(patterns):
            end = gi
            while end < len(patterns) and patterns[end] == patterns[gi]:
                end += 1
            if end - gi == 1:
                run_group(gi * ATTN_UNROLL, patterns[gi])
            else:
                def body(it, carry, pattern=patterns[gi]):
                    run_group(it * ATTN_UNROLL, pattern)
                    return carry
                lax.fori_loop(gi, end, body, 0)
            gi = end

    m = jnp.maximum(jnp.maximum(l_s[0], l_s[1]), l_s[2])
    w0 = jnp.exp(l_s[0] - m)
    w1 = jnp.exp(l_s[1] - m)
    w2 = jnp.exp(l_s[2] - m)
    o_ref[...] = (w0 * o_s[0] + w1 * o_s[1] + w2 * o_s[2]) / (w0 + w1 + w2)


def _alibi_slopes():
    return jnp.asarray(2.0 ** (-8.0 * np.arange(1, HEADS + 1) / HEADS), dtype=F32)


def _attn_prompt(qkv, attn, n_batch, seq):
    tiles = seq // ATTN_TILE
    slabs = GROUP_W // LANES
    in_specs = [pl.BlockSpec(memory_space=pltpu.SMEM)]
    args = [_alibi_slopes()]
    for g, (_, dil) in enumerate(ATTN_GROUPS):
        sub = ATTN_BLOCK * dil
        per_tile = ATTN_TILE // sub
        for which in range(3):
            col = which * (ATTN_QKV // LANES) + g * slabs
            in_specs.append(pl.BlockSpec((ATTN_TILE, LANES), lambda b, hp, t, col=col: (b * tiles + t, col + hp)))
            args.append(qkv)
        for which in (1, 2):
            col = which * (ATTN_QKV // LANES) + g * slabs
            in_specs.append(pl.BlockSpec(
                (sub, LANES),
                lambda b, hp, t, col=col, per_tile=per_tile: (jnp.maximum((b * tiles + t) * per_tile - 1, 0), col + hp)))
            args.append(qkv)
    in_specs.append(pl.BlockSpec(memory_space=pl.ANY))
    args.append(attn)
    return pl.pallas_call(
        _attn_prompt_kernel,
        grid=(n_batch, slabs, tiles),
        in_specs=in_specs,
        out_specs=pl.BlockSpec((ATTN_TILE, LANES), lambda b, hp, t: (b * tiles + t, hp)),
        out_shape=jax.ShapeDtypeStruct(attn.shape, attn.dtype),
        input_output_aliases={len(args) - 1: 0},
        scratch_shapes=[pltpu.VMEM((N_GROUPS, ATTN_TILE, LANES), F32), pltpu.VMEM((N_GROUPS, ATTN_TILE, LANES), F32),
                        pltpu.VMEM((2, ATTN_BLOCK, 2 * ATTN_BLOCK), F32), pltpu.VMEM((2, ATTN_BLOCK, 2 * ATTN_BLOCK), F32),
                        pltpu.VMEM((5, 4, ATTN_TILE // 4, LANES), F32), pltpu.VMEM((5, 16, ATTN_BLOCK, LANES), F32)],
        compiler_params=_cparams("parallel", "parallel", "arbitrary"),
        name="attn_prompt",
    )(*args)


def _kv_prompt_kernel(k_ref, v_ref, prev_ref, after_ref, o_ref):
    for c, ref in enumerate((k_ref, v_ref)):
        for s in range(GROUP_W // LANES):
            o_ref[c, s] = ref[:, s * LANES:(s + 1) * LANES].T


def _kv_prompt(qkv, g, seq, buf, layer, after):
    n_batch, w = buf.shape[1], buf.shape[-1]
    tm = min(512, w)
    slabs = GROUP_W // LANES
    row0 = (seq - w) // tm
    col = lambda which: which * N_GROUPS + g
    return pl.pallas_call(
        _kv_prompt_kernel,
        grid=(n_batch, w // tm),
        in_specs=[pl.BlockSpec((tm, GROUP_W), lambda b, i: (b * (seq // tm) + row0 + i, col(1))),
                  pl.BlockSpec((tm, GROUP_W), lambda b, i: (b * (seq // tm) + row0 + i, col(2))),
                  pl.BlockSpec(memory_space=pl.ANY), pl.BlockSpec(memory_space=pl.ANY)],
        out_specs=pl.BlockSpec((None, None, 2, slabs, LANES, tm), lambda b, i: (layer, b, 0, 0, 0, i)),
        out_shape=jax.ShapeDtypeStruct(buf.shape, buf.dtype),
        input_output_aliases={2: 0},
        compiler_params=_cparams("parallel", "parallel"),
        name="kv_prompt",
    )(qkv, qkv, buf, after)


def _kv_sample_kernel(x_ref, after_ref, o_ref, *, dec_seq, n_seq):
    for t in range(dec_seq):
        o_ref[t] = x_ref[pl.ds(t, n_seq, stride=dec_seq), :].T


def _kv_sample(qkv, tp, n_seq, dec_seq, after):
    ts = n_seq * dec_seq
    slabs = GROUP_W // LANES
    assert tp % ts == 0
    return pl.pallas_call(
        functools.partial(_kv_sample_kernel, dec_seq=dec_seq, n_seq=n_seq),
        grid=(N_GROUPS, 2, slabs),
        in_specs=[pl.BlockSpec((ts, LANES), lambda g, c, s: (tp // ts, (c + 1) * N_GROUPS * slabs + g * slabs + s)),
                  pl.BlockSpec(memory_space=pl.ANY)],
        out_specs=pl.BlockSpec((None, dec_seq, None, None, LANES, n_seq), lambda g, c, s: (g, 0, c, s, 0, 0)),
        out_shape=jax.ShapeDtypeStruct((N_GROUPS, dec_seq, 2, slabs, LANES, n_seq), F32),
        compiler_params=_cparams("parallel", "parallel", "parallel"),
        name="kv_sample",
    )(qkv, after)


def _attn_sample_kernel(slope_ref, qkv_ref, c0_ref, c1_ref, c2_ref, prev_ref, o_ref, *, dec_seq):
    n = pl.program_id(0)
    half = n % 2
    caches = (c0_ref, c1_ref, c2_ref)
    rows = 2 * SUBLANES
    row = lax.broadcasted_iota(jnp.int32, (rows, 1), 0)
    t_row = (row % SUBLANES) - half * dec_seq
    row_ok = (t_row >= 0) & (t_row < dec_seq)
    lane = lax.broadcasted_iota(jnp.int32, (SUBLANES, LANES), 1)
    first = lane < HEAD_DIM
    ucol = lax.broadcasted_iota(jnp.int32, (rows, SUBLANES), 1) - half * dec_seq
    col_ok = (ucol >= 0) & (ucol < dec_seq)
    slabs = GROUP_W // LANES
    units = [(j, g) for j in range(slabs) for g in range(N_GROUPS)]
    scale = HEAD_DIM ** -0.5
    scores = []
    for j, g in units:
        win, dil = ATTN_GROUPS[g]
        wb = caches[g].shape[-1]
        c = g * slabs + j
        slope = jnp.where(row < SUBLANES, slope_ref[2 * j], slope_ref[2 * j + 1])
        q = qkv_ref[:, c * LANES:(c + 1) * LANES]
        kn = qkv_ref[:, ATTN_QKV + c * LANES:ATTN_QKV + (c + 1) * LANES]
        q2f = jnp.concatenate([jnp.where(first, q, 0.0), jnp.where(first, 0.0, q)], axis=0)
        kt = caches[g][0, j].astype(BF16)
        w = lax.broadcasted_iota(jnp.int32, (rows, wb), 1)
        delta = wb + t_row - w
        ok = row_ok & (delta <= win) & ((delta & (dil - 1)) == 0)
        s_b = jnp.where(ok, _dot(q2f.astype(BF16), kt) * scale - slope * delta.astype(F32), NEG_BIG)
        dn = t_row - ucol
        okn = row_ok & col_ok & (dn >= 0) & (dn <= win) & ((dn & (dil - 1)) == 0)
        s_n = jnp.where(okn, _dot_nt(q2f, kn) * scale - slope * dn.astype(F32), NEG_BIG)
        scores.append((s_b, s_n))
    probs = []
    for s_b, s_n in scores:
        m = jnp.maximum(jnp.max(s_b, axis=-1, keepdims=True), jnp.max(s_n, axis=-1, keepdims=True))
        p_b = jnp.exp(s_b - m)
        p_n = jnp.exp(s_n - m)
        l = jnp.sum(p_b, axis=-1, keepdims=True) + jnp.sum(p_n, axis=-1, keepdims=True)
        probs.append((p_b.astype(BF16), p_n, l, m + jnp.log(l)))
    o_u, l_u = {}, {}
    for (j, g), (p_b, p_n, l, lse) in zip(units, probs):
        c = g * slabs + j
        vn = qkv_ref[:, 2 * ATTN_QKV + c * LANES:2 * ATTN_QKV + (c + 1) * LANES]
        vt = caches[g][1, j].astype(BF16)
        o = (_dot_nt(p_b, vt) + _dot(p_n, vn)) / l
        o_u[j, g] = jnp.where(first, o[:SUBLANES], o[SUBLANES:])
        l_u[j, g] = jnp.where(first, lse[:SUBLANES], lse[SUBLANES:])
    out_slabs = []
    for j in range(slabs):
        m = jnp.maximum(jnp.maximum(l_u[j, 0], l_u[j, 1]), l_u[j, 2])
        ws = [jnp.exp(l_u[j, g] - m) for g in range(N_GROUPS)]
        out_slabs.append((ws[0] * o_u[j, 0] + ws[1] * o_u[j, 1] + ws[2] * o_u[j, 2]) / (ws[0] + ws[1] + ws[2]))
    res = jnp.concatenate(out_slabs, axis=1)
    mine = (lax.broadcasted_iota(jnp.int32, (SUBLANES, 1), 0) // dec_seq) == half

    @pl.when(half == 0)
    def _():
        o_ref[...] = res

    @pl.when(half != 0)
    def _():
        o_ref[...] = jnp.where(mine, res, o_ref[...])


def _attn_sample(qkv, caches_t, layer, attn, tp, n_seq, dec_seq):
    assert 2 * dec_seq == SUBLANES
    blk0 = tp // SUBLANES
    in_specs = [
        pl.BlockSpec(memory_space=pltpu.SMEM),
        pl.BlockSpec((SUBLANES, 3 * ATTN_QKV), lambda n: (blk0 + n // 2, 0)),
    ]
    for c in caches_t:
        in_specs.append(pl.BlockSpec((None, None) + c.shape[2:], lambda n, layer=layer: (layer, n, 0, 0, 0, 0)))
    in_specs.append(pl.BlockSpec(memory_space=pl.ANY))
    return pl.pallas_call(
        functools.partial(_attn_sample_kernel, dec_seq=dec_seq),
        grid=(n_seq,),
        in_specs=in_specs,
        out_specs=pl.BlockSpec((SUBLANES, GROUP_W), lambda n: (blk0 + n // 2, 0)),
        out_shape=jax.ShapeDtypeStruct(attn.shape, attn.dtype),
        input_output_aliases={5: 0},
        compiler_params=_cparams("arbitrary"),
        name="attn_sample",
    )(_alibi_slopes(), qkv, *caches_t, attn)


def _log_sigmoid(x):
    return jnp.minimum(x, 0.0) - jnp.log1p(jnp.exp(-jnp.abs(x)))


def _mlstm_chunks(items):
    ln = items[0][0].shape[0]
    ii = lax.broadcasted_iota(jnp.int32, (ln, ln), 0)
    jj = lax.broadcasted_iota(jnp.int32, (ln, ln), 1)
    eye = ii == jj
    causal = jj <= ii
    stage1 = []
    for q, k, v, i_col, lf_col, c_st, n_st, m_st in items:
        lf_row = jnp.sum(jnp.where(eye, lf_col, 0.0), axis=0, keepdims=True)
        i_row = jnp.sum(jnp.where(eye, i_col, 0.0), axis=0, keepdims=True)
        b_col = jnp.sum(jnp.where(causal, lf_row, 0.0), axis=1, keepdims=True)
        b_row = jnp.sum(jnp.where(ii <= jj, lf_col, 0.0), axis=0, keepdims=True)
        dm = jnp.where(causal, b_col - b_row + i_row, -jnp.inf)
        m_inter = b_col + m_st
        m_t = jnp.maximum(m_inter, jnp.max(dm, axis=1, keepdims=True))
        ks = k * (MLSTM_DK ** -0.5)
        qb = q.astype(BF16)
        vb = v.astype(BF16)
        qk = _dot_nt(qb, ks.astype(BF16))
        qc = _dot(qb, c_st.astype(BF16))
        stage1.append((dm, m_inter, m_t, b_col, ks, vb, qk, qc))
    stage2 = []
    for (dm, m_inter, m_t, _, _, vb, qk, _) in stage1:
        w_intra = jnp.exp(dm - m_t) * qk
        stage2.append((w_intra, jnp.exp(m_inter - m_t), _dot(w_intra.astype(BF16), vb)))
    stage3 = []
    for item, s1, (w_intra, w_inter, wv) in zip(items, stage1, stage2):
        q, _, _, i_col, _, c_st, n_st, m_st = item
        _, _, m_t, b_col, ks, vb, _, qc = s1
        num = wv + w_inter * qc
        den = jnp.sum(w_intra, axis=1, keepdims=True) + w_inter * jnp.sum(q * n_st, axis=1, keepdims=True)
        h = num / jnp.maximum(jnp.abs(den), jnp.exp(-m_t))
        b_last = b_col[ln - 1:ln, :]
        w_src = b_last - b_col + i_col
        m_new = jnp.maximum(b_last + m_st, jnp.max(w_src, axis=0, keepdims=True))
        decay = jnp.exp(b_last + m_st - m_new)
        kp = jnp.exp(w_src - m_new) * ks
        stage3.append((h, decay, kp, m_new, _dot_tn(kp.astype(BF16), vb)))
    outs = []
    for item, (h, decay, kp, m_new, kv) in zip(items, stage3):
        c_st, n_st = item[5], item[6]
        outs.append((h, decay * c_st + kv, decay * n_st + jnp.sum(kp, axis=0, keepdims=True), m_new))
    return outs


def _gate_columns(gates, head):
    lane = lax.broadcasted_iota(jnp.int32, gates.shape, 1)
    i_col = jnp.sum(jnp.where(lane == head, gates, 0.0), axis=1, keepdims=True)
    f_col = jnp.sum(jnp.where(lane == head + MLSTM_HEADS, gates, 0.0), axis=1, keepdims=True)
    return i_col, _log_sigmoid(f_col)


def _mlstm_prompt_kernel(ml_ref, bg_ref, gn_ref, prev_ref, hm_ref, c_out, n_out, m_out, c_s, n_s, m_s):
    chunk = pl.program_id(1)

    @pl.when(chunk == 0)
    def _():
        c_s[...] = jnp.zeros_like(c_s)
        n_s[...] = jnp.zeros_like(n_s)
        m_s[...] = jnp.zeros_like(m_s)

    gates = ml_ref[:, ML_W:ML_W + LANES] + bg_ref[...]
    k0 = MLSTM_HEADS * MLSTM_DK
    v0 = 2 * MLSTM_HEADS * MLSTM_DK
    o0 = v0 + MLSTM_HEADS * MLSTM_DV
    items = []
    for head in range(MLSTM_HEADS):
        i_col, lf_col = _gate_columns(gates, head)
        items.append((ml_ref[:, head * MLSTM_DK:(head + 1) * MLSTM_DK],
                      ml_ref[:, k0 + head * MLSTM_DK:k0 + (head + 1) * MLSTM_DK],
                      ml_ref[:, v0 + head * MLSTM_DV:v0 + (head + 1) * MLSTM_DV],
                      i_col, lf_col, c_s[head], n_s[head], m_s[head]))
    for head, (h, c_new, n_new, m_new) in enumerate(_mlstm_chunks(items)):
        c_s[head] = c_new
        n_s[head] = n_new
        m_s[head] = m_new
        cols = slice(head * MLSTM_DV, (head + 1) * MLSTM_DV)
        og = ml_ref[:, o0 + head * MLSTM_DV:o0 + (head + 1) * MLSTM_DV]
        hm_ref[:, cols] = _rms(h) * gn_ref[:, cols] * _sigmoid(og)

    @pl.when(chunk == pl.num_programs(1) - 1)
    def _():
        c_out[...] = c_s[...]
        n_out[...] = n_s[...]
        m_out[...] = m_s[...]


def _mlstm_prompt(ml, b_gates_pad, g_mlstm, hm, n_seq, seq_len):
    ln = MLSTM_CHUNK
    chunks = seq_len // ln
    st = lambda shp: pl.BlockSpec((None, MLSTM_HEADS) + shp, lambda s, c: (s, 0, 0, 0))
    return pl.pallas_call(
        _mlstm_prompt_kernel,
        grid=(n_seq, chunks),
        in_specs=[
            pl.BlockSpec((ln, ML_PAD_W), lambda s, c: (s * chunks + c, 0)),
            pl.BlockSpec((1, LANES), lambda s, c: (0, 0)),
            pl.BlockSpec((1, MLSTM_HEADS * MLSTM_DV), lambda s, c: (0, 0)),
            pl.BlockSpec(memory_space=pl.ANY),
        ],
        out_specs=[
            pl.BlockSpec((ln, MLSTM_HEADS * MLSTM_DV), lambda s, c: (s * chunks + c, 0)),
            st((MLSTM_DK, MLSTM_DV)), st((1, MLSTM_DK)), st((1, 1)),
        ],
        out_shape=[
            jax.ShapeDtypeStruct(hm.shape, hm.dtype),
            jax.ShapeDtypeStruct((n_seq, MLSTM_HEADS, MLSTM_DK, MLSTM_DV), F32),
            jax.ShapeDtypeStruct((n_seq, MLSTM_HEADS, 1, MLSTM_DK), F32),
            jax.ShapeDtypeStruct((n_seq, MLSTM_HEADS, 1, 1), F32),
        ],
        scratch_shapes=[pltpu.VMEM((MLSTM_HEADS, MLSTM_DK, MLSTM_DV), F32),
                        pltpu.VMEM((MLSTM_HEADS, 1, MLSTM_DK), F32),
                        pltpu.VMEM((MLSTM_HEADS, 1, 1), F32)],
        input_output_aliases={3: 0},
        compiler_params=_cparams("parallel", "arbitrary"),
        name="mlstm_prompt",
    )(ml, b_gates_pad, g_mlstm.reshape(1, -1), hm)


def _mlstm_sample_kernel(ml_ref, bg_ref, gn_ref, c0_ref, n0_ref, m0_ref, prev_ref, cprev_ref,
                         hm_ref, c_out, n_out, m_out, *, seq_rows):
    ln = ml_ref.shape[0]
    row = lax.broadcasted_iota(jnp.int32, (ln, 1), 0)
    gates = ml_ref[:, ML_W:ML_W + LANES] + bg_ref[...]
    k0 = MLSTM_HEADS * MLSTM_DK
    v0 = 2 * MLSTM_HEADS * MLSTM_DK
    o0 = v0 + MLSTM_HEADS * MLSTM_DV
    per_blk = ln // seq_rows
    items = []
    for head in range(MLSTM_HEADS):
        q = ml_ref[:, head * MLSTM_DK:(head + 1) * MLSTM_DK]
        k = ml_ref[:, k0 + head * MLSTM_DK:k0 + (head + 1) * MLSTM_DK]
        v = ml_ref[:, v0 + head * MLSTM_DV:v0 + (head + 1) * MLSTM_DV]
        i_col, lf_col = _gate_columns(gates, head)
        for j in range(per_blk):
            mine = (row // seq_rows) == j
            items.append((q, k, v, jnp.where(mine, i_col, NEG_BIG), jnp.where(mine, lf_col, 0.0),
                          c0_ref[j, head], n0_ref[j, head], m0_ref[j, head]))
    outs = _mlstm_chunks(items)
    for head in range(MLSTM_HEADS):
        h_all = None
        for j in range(per_blk):
            h, c_new, n_new, m_new = outs[head * per_blk + j]
            c_out[j, head] = c_new
            n_out[j, head] = n_new
            m_out[j, head] = m_new
            h_all = h if h_all is None else jnp.where((row // seq_rows) == j, h, h_all)
        cols = slice(head * MLSTM_DV, (head + 1) * MLSTM_DV)
        og = ml_ref[:, o0 + head * MLSTM_DV:o0 + (head + 1) * MLSTM_DV]
        hm_ref[:, cols] = _rms(h_all) * gn_ref[:, cols] * _sigmoid(og)


def _mlstm_sample(ml, b_gates_pad, g_mlstm, c0, n0, m0, layer, hm, c_new_all, row0, n_seq, seq_len):
    per_blk = SUBLANES // seq_len
    assert per_blk * seq_len == SUBLANES and n_seq % per_blk == 0 and row0 % SUBLANES == 0
    blk0 = row0 // SUBLANES
    st_in = lambda shp: pl.BlockSpec((None, per_blk, MLSTM_HEADS) + shp, lambda s: (layer, s, 0, 0, 0))
    st_out = lambda shp: pl.BlockSpec((per_blk, MLSTM_HEADS) + shp, lambda s: (s, 0, 0, 0))
    return pl.pallas_call(
        functools.partial(_mlstm_sample_kernel, seq_rows=seq_len),
        grid=(n_seq // per_blk,),
        in_specs=[
            pl.BlockSpec((SUBLANES, ML_PAD_W), lambda s: (blk0 + s, 0)),
            pl.BlockSpec((1, LANES), lambda s: (0, 0)),
            pl.BlockSpec((1, MLSTM_HEADS * MLSTM_DV), lambda s: (0, 0)),
            st_in((MLSTM_DK, MLSTM_DV)), st_in((1, MLSTM_DK)), st_in((1, 1)),
            pl.BlockSpec(memory_space=pl.ANY),
            pl.BlockSpec(memory_space=pl.ANY),
        ],
        out_specs=[
            pl.BlockSpec((SUBLANES, MLSTM_HEADS * MLSTM_DV), lambda s: (blk0 + s, 0)),
            st_in((MLSTM_DK, MLSTM_DV)), st_out((1, MLSTM_DK)), st_out((1, 1)),
        ],
        out_shape=[
            jax.ShapeDtypeStruct(hm.shape, hm.dtype),
            jax.ShapeDtypeStruct(c_new_all.shape, F32),
            jax.ShapeDtypeStruct(n0.shape[1:], F32),
            jax.ShapeDtypeStruct(m0.shape[1:], F32),
        ],
        input_output_aliases={6: 0, 7: 1},
        compiler_params=_cparams("parallel"),
        name="mlstm_sample",
    )(ml, b_gates_pad, g_mlstm.reshape(1, -1), c0, n0, m0, hm, c_new_all)


def _postmix_kernel(attn_ref, hm_ref, gt_ref, x_ref, gtp_ref, shp_ref, scp_ref, gts_ref, shs_ref, scs_ref,
                    gf_ref, wa_ref, wm_ref, wo_ref, wr_ref, br_ref, x1_ref, h2_ref, gw_ref, ei_ref, rk_ref, cnt_ref,
                    cnt_s, *, n_prompt_tiles):
    is_sample = pl.program_id(0) >= n_prompt_tiles
    d = x_ref.shape[1]
    tm = x_ref.shape[0]
    pr = tm // POSTMIX_PARTS
    parts = [slice(p * pr, (p + 1) * pr) for p in range(POSTMIX_PARTS)]

    def mod(ref_s, ref_p, rows):
        return jnp.where(is_sample, ref_s[rows], ref_p[...])

    @pl.when(pl.program_id(0) == 0)
    def _():
        cnt_s[...] = jnp.zeros_like(cnt_s)

    branch = [(_dot(attn_ref[rows].astype(BF16), wa_ref[...]), _dot(hm_ref[rows].astype(BF16), wm_ref[...]))
              for rows in parts]
    proj = []
    for rows, (y_attn, y_mlstm) in zip(parts, branch):
        merged = _sigmoid(gt_ref[rows, :d]) * y_attn + _sigmoid(gt_ref[rows, d:]) * y_mlstm
        proj.append(_dot(merged.astype(BF16), wo_ref[...]))
    lane = lax.broadcasted_iota(jnp.int32, (pr, LANES), 1)
    logits = []
    for rows, z in zip(parts, proj):
        x1 = x_ref[rows] + mod(gts_ref, gtp_ref, rows) * z
        x1_ref[rows] = x1
        h2 = (_rms(x1) * gf_ref[...] * (1.0 + mod(scs_ref, scp_ref, rows)) + mod(shs_ref, shp_ref, rows)).astype(BF16)
        bits = lax.bitcast_convert_type(h2.astype(F32), jnp.int32)
        h2_ref[rows] = lax.shift_right_logical(bits[:, :d // 2], 16) | bits[:, d // 2:]
        logits.append(jnp.where(lane < N_EXPERTS, _dot(h2, wr_ref[...]) + br_ref[...], -jnp.inf))

    ri = lax.broadcasted_iota(jnp.int32, (pr, pr), 0)
    ci = lax.broadcasted_iota(jnp.int32, (pr, pr), 1)
    before = jnp.where(ci < ri, 1.0, 0.0).astype(BF16)
    picks = []
    for lg in logits:
        vals, idxs = [], []
        chosen = jnp.zeros((pr, LANES), F32)
        for _ in range(TOP_K):
            m = jnp.max(lg, axis=-1, keepdims=True)
            idx = jnp.min(jnp.where(lg == m, lane, LANES), axis=-1, keepdims=True)
            hit = lane == idx
            vals.append(m)
            idxs.append(idx)
            chosen = jnp.where(hit, 1.0, chosen)
            lg = jnp.where(hit, -jnp.inf, lg)
        picks.append((vals, idxs, chosen, _dot(before, chosen.astype(BF16))))
    total = cnt_s[...]
    for rows, (vals, idxs, chosen, within) in zip(parts, picks):
        prefix = within + total
        ex = [jnp.exp(v - vals[0]) for v in vals]
        den = ex[0] + ex[1] + ex[2] + ex[3]
        for k in range(TOP_K):
            gw_ref[rows, k:k + 1] = ex[k] / den
            ei_ref[rows, k:k + 1] = idxs[k]
            rank = jnp.sum(jnp.where(lane == idxs[k], prefix, 0.0), axis=-1, keepdims=True)
            rk_ref[rows, k:k + 1] = rank.astype(jnp.int32)
        total = total + jnp.sum(chosen, axis=0, keepdims=True)
    cnt_s[...] = total
    cnt_ref[...] = total.astype(jnp.int32)


def _postmix(attn, hm, gt, x_all, mod_p, mod_s, layer, g_ffn, wa, wm, wo, wr, br, tp, n_batch):
    t_all, d = x_all.shape
    tm = ROW_TILE
    assert t_all % tm == 0 and tp % (tm * n_batch) == 0
    n_prompt_tiles = tp // tm
    tiles_per_batch = n_prompt_tiles // n_batch
    specs = [_mod_specs(c, tm, n_prompt_tiles, tiles_per_batch, n_batch, layer) for c in (2, 3, 4)]
    row = lambda w: pl.BlockSpec((tm, w), lambda i: (i, 0))
    full = lambda a: pl.BlockSpec(a.shape, lambda i: (0,) * a.ndim)
    gf = g_ffn.reshape(1, d)
    return pl.pallas_call(
        functools.partial(_postmix_kernel, n_prompt_tiles=n_prompt_tiles),
        grid=(t_all // tm,),
        in_specs=[row(GROUP_W), row(d), row(GATE_W), row(d),
                  specs[0][0], specs[1][0], specs[2][0], specs[0][1], specs[1][1], specs[2][1],
                  full(gf), full(wa), full(wm), full(wo), full(wr), full(br)],
        out_specs=[row(d), row(d // 2), row(TOP_K), row(TOP_K), row(TOP_K),
                   pl.BlockSpec((1, LANES), lambda i: (0, 0))],
        out_shape=[jax.ShapeDtypeStruct((t_all, d), F32), jax.ShapeDtypeStruct((t_all, d // 2), jnp.int32),
                   jax.ShapeDtypeStruct((t_all, TOP_K), F32), jax.ShapeDtypeStruct((t_all, TOP_K), jnp.int32),
                   jax.ShapeDtypeStruct((t_all, TOP_K), jnp.int32), jax.ShapeDtypeStruct((1, LANES), jnp.int32)],
        scratch_shapes=[pltpu.VMEM((1, LANES), F32)],
        compiler_params=_cparams("arbitrary"),
        name="postmix",
    )(attn, hm, gt, x_all, mod_p, mod_p, mod_p, mod_s, mod_s, mod_s, gf, wa, wm, wo, wr, br)


def _expert_kernel(be_ref, ne_ref, nused_ref, rows_ref, wgu_hbm, bgu_ref, wd_hbm, bd_ref, after_ref, y_ref,
                   wgu_f, wd_f, wgu_s, wd_s, sem, *, layer):
    i = pl.program_id(0)
    e = be_ref[i]
    prev = be_ref[jnp.maximum(i - 1, 0)]

    def weight_copies(expert):
        return (pltpu.make_async_copy(wgu_hbm.at[layer, expert], wgu_f, sem.at[0]),
                pltpu.make_async_copy(wd_hbm.at[layer, expert], wd_f, sem.at[1]))

    @pl.when(i == 0)
    def _():
        for cp in weight_copies(e):
            cp.start()

    @pl.when((i == 0) | (e != prev))
    def _():
        for cp in weight_copies(e):
            cp.wait()
        wgu_s[...] = wgu_f[...].astype(BF16)
        wd_s[...] = wd_f[...].astype(BF16)
        nxt = ne_ref[i]

        @pl.when(nxt >= 0)
        def _():
            for cp in weight_copies(nxt):
                cp.start()

    @pl.when(i < nused_ref[0])
    def _():
        de = wd_s.shape[0]
        words = rows_ref[...]
        lo = lax.bitcast_convert_type(lax.shift_left(words, 16), F32).astype(BF16)
        hi = lax.bitcast_convert_type(words & jnp.int32(-65536), F32).astype(BF16)
        rows = jnp.concatenate([lo, hi], axis=1)
        gu = _dot(rows, wgu_s[...]) + bgu_ref[...]
        gate = jnp.minimum(gu[:, :de], SWIGLU_LIMIT)
        up = jnp.clip(gu[:, de:], -SWIGLU_LIMIT, SWIGLU_LIMIT)
        act = (up + 1.0) * gate * _sigmoid(SWIGLU_ALPHA * gate)
        y_ref[...] = _dot(act.astype(BF16), wd_s[...]) + bd_ref[...]

    @pl.when(i >= nused_ref[0])
    def _():
        y_ref[...] = jnp.zeros_like(y_ref)


def _experts(block_expert, next_expert, n_used, rows, w_gu, b_gu, w_down, b_down, layer, after):
    nrows = rows.shape[0]
    n_blocks = nrows // MOE_BLOCK
    d, de = w_down.shape[3], w_down.shape[2]
    grid_spec = pltpu.PrefetchScalarGridSpec(
        num_scalar_prefetch=3,
        grid=(n_blocks,),
        in_specs=[
            pl.BlockSpec((MOE_BLOCK, d // 2), lambda i, be, ne, nu: (i, 0)),
            pl.BlockSpec(memory_space=pl.ANY),
            pl.BlockSpec((None, None, 1, 2 * de), lambda i, be, ne, nu: (layer, be[i], 0, 0)),
            pl.BlockSpec(memory_space=pl.ANY),
            pl.BlockSpec((None, None, 1, d), lambda i, be, ne, nu: (layer, be[i], 0, 0)),
            pl.BlockSpec(memory_space=pl.ANY),
        ],
        out_specs=pl.BlockSpec((MOE_BLOCK, d), lambda i, be, ne, nu: (i, 0)),
        scratch_shapes=[pltpu.VMEM((d, 2 * de), F32), pltpu.VMEM((de, d), F32),
                        pltpu.VMEM((d, 2 * de), BF16), pltpu.VMEM((de, d), BF16),
                        pltpu.SemaphoreType.DMA((2,))],
    )
    return pl.pallas_call(
        functools.partial(_expert_kernel, layer=layer),
        grid_spec=grid_spec,
        out_shape=jax.ShapeDtypeStruct((nrows, d), F32),
        compiler_params=_cparams("arbitrary"),
        name="experts",
    )(block_expert, next_expert, n_used, rows, w_gu, b_gu.reshape(b_gu.shape[0], b_gu.shape[1], 1, -1),
      w_down, b_down.reshape(b_down.shape[0], b_down.shape[1], 1, -1), after)


def _sc_mesh():
    return plsc.VectorSubcoreMesh(core_axis_name="core", subcore_axis_name="subcore")


def _sc_scatter_rows(x, pos, n_rows):
    t, c = x.shape
    nk = pos.shape[0]
    nwin = t // SC_SCATTER_WIN
    assert nwin * SC_SCATTER_WIN == t
    idx = pos.reshape(nk * nwin, SC_SCATTER_WIN)

    @functools.partial(pl.kernel, out_type=jax.ShapeDtypeStruct((n_rows, c), x.dtype), mesh=_sc_mesh(),
                       scratch_types=[], name="sc_scatter_rows")
    def kern(x_hbm, i_hbm, o_hbm):
        def body(x_vmem, i_vmem):
            pltpu.sync_copy(x_vmem, o_hbm.at[i_vmem.at[0]])

        pltpu.emit_pipeline(
            body,
            grid=(nk * nwin,),
            in_specs=[pl.BlockSpec((SC_SCATTER_WIN, c), lambda i: (i % nwin, 0)),
                      pl.BlockSpec((1, SC_SCATTER_WIN), lambda i: (i, 0))],
            out_specs=[],
            core_axis_name=("core", "subcore"),
            dimension_semantics=(pltpu.PARALLEL,),
        )(x_hbm, i_hbm)

    return kern(x, idx)


def _sc_gather_rows(y, pos):
    nk, t = pos.shape
    c = y.shape[1]
    n = nk * t
    assert n % SC_GATHER_WIN == 0
    idx = pos.reshape(n // SC_GATHER_WIN, SC_GATHER_WIN)

    @functools.partial(pl.kernel, out_type=jax.ShapeDtypeStruct((n, c), y.dtype), mesh=_sc_mesh(),
                       scratch_types=[], name="sc_gather_rows")
    def kern(y_hbm, i_hbm, o_hbm):
        def body(i_vmem, o_vmem):
            pltpu.sync_copy(y_hbm.at[i_vmem.at[0]], o_vmem)

        pltpu.emit_pipeline(
            body,
            grid=(n // SC_GATHER_WIN,),
            in_specs=[pl.BlockSpec((1, SC_GATHER_WIN), lambda i: (i, 0))],
            out_specs=[pl.BlockSpec((SC_GATHER_WIN, c), lambda i: (i, 0))],
            core_axis_name=("core", "subcore"),
            dimension_semantics=(pltpu.PARALLEL,),
        )(i_hbm, o_hbm)

    return kern(y, idx).reshape(nk, t, c)


def _combine_kernel(yg_ref, gw_ref, x1_ref, gtp_ref, gts_ref, gfin_ref, after_ref, o_ref, *, n_prompt_tiles, final):
    is_sample = pl.program_id(0) >= n_prompt_tiles
    gate2 = jnp.where(is_sample, gts_ref[...], gtp_ref[...])
    gw = gw_ref[...]
    acc = gw[:, 0:1] * yg_ref[0]
    for k in range(1, TOP_K):
        acc = acc + gw[:, k:k + 1] * yg_ref[k]
    x2 = x1_ref[...] + gate2 * acc
    o_ref[...] = _rms(x2) * gfin_ref[...] if final else x2


def _combine(yg, gates, x1, mod_p, mod_s, layer, g_final, tp, n_batch, final, after):
    t_all, d = x1.shape
    tm = COMBINE_TILE
    assert t_all % tm == 0 and tp % (tm * n_batch) == 0
    n_prompt_tiles = tp // tm
    tiles_per_batch = n_prompt_tiles // n_batch
    gtp, gts = _mod_specs(5, tm, n_prompt_tiles, tiles_per_batch, n_batch, layer)
    row = lambda w: pl.BlockSpec((tm, w), lambda i: (i, 0))
    return pl.pallas_call(
        functools.partial(_combine_kernel, n_prompt_tiles=n_prompt_tiles, final=final),
        grid=(t_all // tm,),
        in_specs=[pl.BlockSpec((TOP_K, tm, d), lambda i: (0, i, 0)), row(TOP_K), row(d), gtp, gts,
                  pl.BlockSpec((1, d), lambda i: (0, 0)), pl.BlockSpec(memory_space=pl.ANY)],
        out_specs=row(d),
        out_shape=jax.ShapeDtypeStruct((t_all, d), F32),
        compiler_params=_cparams("parallel"),
        name="moe_combine",
    )(yg, gates, x1, mod_p, mod_s, g_final.reshape(1, d), after)


def _positions(eidx, rank, counts):
    t = eidx.shape[0]
    n_blocks = t * TOP_K // MOE_BLOCK + N_EXPERTS
    padded = (counts + MOE_BLOCK - 1) // MOE_BLOCK * MOE_BLOCK
    pad_end = jnp.cumsum(padded)
    pad_start = pad_end - padded
    experts = jnp.arange(N_EXPERTS, dtype=jnp.int32)
    start = jnp.sum(jnp.where(eidx[:, :, None] == experts, pad_start, 0), axis=-1)
    pos = (start + rank).T.astype(jnp.int32)
    n_used = (pad_end[-1] // MOE_BLOCK).astype(jnp.int32)
    blk = jnp.minimum(jnp.arange(n_blocks, dtype=jnp.int32), n_used - 1)
    block_expert = jnp.sum((pad_end[None, :] <= (blk * MOE_BLOCK)[:, None]).astype(jnp.int32), axis=1)
    block_expert = jnp.minimum(block_expert, N_EXPERTS - 1).astype(jnp.int32)
    later_used = (counts[None, :] > 0) & (experts[None, :] > experts[:, None])
    nxt = jnp.min(jnp.where(later_used, experts[None, :], N_EXPERTS), axis=1)
    nxt = jnp.where(nxt < N_EXPERTS, nxt, -1)
    next_expert = jnp.sum(jnp.where(block_expert[:, None] == experts[None, :], nxt[None, :], 0), axis=1)
    return pos, block_expert, next_expert.astype(jnp.int32), n_used.reshape(1), n_blocks


def kernel(x_prompt, x_sample, cache_kv_w128, cache_kv_w512, cache_kv_w2048, state_mlstm_C, state_mlstm_n,
           state_mlstm_m, c_prompt, c_sample, w_ada, b_ada, g_mix, g_ffn, w_in, b_gates, g_mlstm, w_br_attn,
           w_br_mlstm, w_out, w_router, b_router, w_gu, b_gu, w_down, b_down, g_final):
    n_batch, seq, d = x_prompt.shape
    n_seq, dec_seq, _ = x_sample.shape
    depth = w_ada.shape[0]
    tp, ts = n_batch * seq, n_seq * dec_seq
    t_all = tp + ts

    x_all = jnp.concatenate([x_prompt.reshape(tp, d), x_sample.reshape(ts, d)], axis=0)
    mod_p = _ada_mod(c_prompt, w_ada, b_ada).reshape(depth, n_batch, 1, 6 * d)
    mod_s = _ada_mod(jnp.repeat(c_sample, dec_seq, axis=0), w_ada, b_ada)
    caches_t = [jnp.transpose(c, (0, 1, 3, 4, 5, 2)).reshape(c.shape[0], c.shape[1], 2, HEADS // 2, 2 * HEAD_DIM, c.shape[2])
                for c in (cache_kv_w128, cache_kv_w512, cache_kv_w2048)]
    sn = state_mlstm_n.reshape(depth, n_seq, MLSTM_HEADS, 1, MLSTM_DK)
    sm = state_mlstm_m.reshape(depth, n_seq, MLSTM_HEADS, 1, 1)
    wr_pad = jnp.pad(w_router, ((0, 0), (0, 0), (0, LANES - N_EXPERTS))).astype(BF16)
    br_pad = jnp.pad(b_router, ((0, 0), (0, LANES - N_EXPERTS))).reshape(depth, 1, LANES)
    bg_pad = jnp.pad(b_gates, ((0, 0), (0, LANES - b_gates.shape[1]))).reshape(depth, 1, LANES)

    kvs = [[] for _ in range(N_GROUPS)]
    mp = [[], [], []]
    ms = [[], []]
    attn = jnp.zeros((t_all, GROUP_W), F32)
    hm = jnp.zeros((t_all, MLSTM_HEADS * MLSTM_DV), F32)
    kvp = [jnp.zeros((depth, n_batch, 2, HEADS // 2, 2 * HEAD_DIM, min(win, seq)), F32) for win, _ in ATTN_GROUPS]
    c_s = jnp.zeros(state_mlstm_C.shape, F32)
    for l in range(depth):
        qkv, ml, gt = _inproj(x_all, g_mix[l], mod_p, mod_s, l, _split_w_in(w_in[l]), tp, n_batch)

        attn = _attn_prompt(qkv, attn, n_batch, seq)
        attn = _attn_sample(qkv, caches_t, l, attn, tp, n_seq, dec_seq)

        hm, c_p, n_p, m_p = _mlstm_prompt(ml, bg_pad[l], g_mlstm[l], hm, n_batch, seq)
        hm, c_s, n_s, m_s = _mlstm_sample(ml, bg_pad[l], g_mlstm[l], state_mlstm_C, sn, sm, l, hm, c_s,
                                          tp, n_seq, dec_seq)

        x1, h2w, gates, eidx, rank, counts = _postmix(
            attn, hm, gt, x_all, mod_p, mod_s, l, g_ffn[l], w_br_attn[l].astype(BF16), w_br_mlstm[l].astype(BF16),
            w_out[l].astype(BF16), wr_pad[l], br_pad[l], tp, n_batch)

        pos, block_expert, next_expert, n_used, n_blocks = _positions(eidx, rank, counts[0, :N_EXPERTS])
        rows = _sc_scatter_rows(h2w, pos, n_blocks * MOE_BLOCK)
        kvp[2] = _kv_prompt(qkv, 2, seq, kvp[2], l, after=x1)
        y = _experts(block_expert, next_expert, n_used, rows, w_gu, b_gu, w_down, b_down, l, after=kvp[2])
        yg = _sc_gather_rows(y, pos)
        kv_s = _kv_sample(qkv, tp, n_seq, dec_seq, after=y)
        kvp[1] = _kv_prompt(qkv, 1, seq, kvp[1], l, after=kv_s)
        kvp[0] = _kv_prompt(qkv, 0, seq, kvp[0], l, after=kvp[1])
        for g in range(N_GROUPS):
            kvs[g].append(kv_s[g])
        x_all = _combine(yg, gates, x1, mod_p, mod_s, l, g_final, tp, n_batch, final=(l == depth - 1), after=kvp[0])
        for lst, val in zip(mp, (c_p, n_p.reshape(n_batch, MLSTM_HEADS, MLSTM_DK), m_p.reshape(n_batch, MLSTM_HEADS))):
            lst.append(val)
        for lst, val in zip(ms, (n_s.reshape(n_seq, MLSTM_HEADS, MLSTM_DK), m_s.reshape(n_seq, MLSTM_HEADS))):
            lst.append(val)

    y_prompt = x_all[:tp].reshape(n_batch, seq, d)
    y_sample = x_all[tp:].reshape(n_seq, dec_seq, d)

    def kv_prompt_out(a):
        a = a.reshape(depth, n_batch, 2, HEADS, HEAD_DIM, a.shape[-1])
        return jnp.transpose(a, (0, 1, 5, 2, 3, 4))

    def kv_sample_out(parts):
        a = jnp.stack(parts).reshape(depth, dec_seq, 2, HEADS, HEAD_DIM, n_seq)
        return jnp.transpose(a, (0, 5, 1, 2, 3, 4))

    return (y_prompt, y_sample,
            kv_prompt_out(kvp[0]), kv_prompt_out(kvp[1]), kv_prompt_out(kvp[2]),
            jnp.stack(mp[0]), jnp.stack(mp[1]), jnp.stack(mp[2]),
            kv_sample_out(kvs[0]), kv_sample_out(kvs[1]), kv_sample_out(kvs[2]),
            c_s, jnp.stack(ms[0]), jnp.stack(ms[1]))
```
